```python
import jax, jax.numpy as jnp
from jax import lax
import numpy as np

D_MODEL = 1024
BATCH = 8
SEQ = 4096
DEPTH = 1
DEC_BATCH = 128
DEC_SEQ = 8
PAST_LEN = 16384
PAGE_SIZE = 128

HEAD_DIM = 64
D_RWKV = D_MODEL // 2
D_ATTN = D_MODEL - D_RWKV
N_RWKV_HEADS = D_RWKV // HEAD_DIM
N_Q_HEADS = D_ATTN // HEAD_DIM
N_KV_HEADS = 2
Q_PER_KV = N_Q_HEADS // N_KV_HEADS
WINDOW = 128
D_DECAY_LORA = 64
D_AAA_LORA = 64
D_PLE = 256
D_SHIFT = 3 * D_RWKV + D_DECAY_LORA + D_AAA_LORA
D_KV = N_KV_HEADS * HEAD_DIM
D_IN = D_SHIFT + D_RWKV + D_ATTN + 2 * D_KV + D_ATTN
NORM_EPS = 1e-6
LNX_EPS = 64e-5
NEG_INF = -1e30

kernel_name = 'hymba_rwkv7_swa_sink_step'


def _f32(t):
    return t.astype(jnp.float32)


def _rms(x, g, eps=NORM_EPS):
    xf = _f32(x)
    y = xf * lax.rsqrt(jnp.mean(xf * xf, axis=-1, keepdims=True) + eps)
    return (y * _f32(g)).astype(x.dtype)


def _in_proj(x, g_norm, w_in):
    h = _rms(x, g_norm) @ w_in
    sizes = [D_SHIFT, D_RWKV, D_ATTN, D_KV, D_KV, D_ATTN]
    idx = np.cumsum(sizes)[:-1].tolist()
    return jnp.split(h, idx, axis=-1)


def _rwkv_branch(f, prev, s0, mu, w0, w_dec2, a0, w_a2, k_k, k_a, r_k, lnx_w, lnx_b):
    B, T, _ = f.shape
    f_prev = jnp.concatenate([prev.astype(f.dtype), f[:, :-1]], axis=1)
    fs = _f32(f + (f_prev - f) * mu)
    r, k, v, wl, al = jnp.split(fs, [D_RWKV, 2 * D_RWKV, 3 * D_RWKV, 3 * D_RWKV + D_DECAY_LORA], axis=-1)
    logw = -jax.nn.softplus(-(_f32(w0) + jnp.tanh(wl) @ _f32(w_dec2))) - 0.5
    decay = jnp.exp(-jnp.exp(logw))
    a = jax.nn.sigmoid(_f32(a0) + al @ _f32(w_a2))
    hs = lambda t: t.reshape(B, T, N_RWKV_HEADS, HEAD_DIM)
    kk = hs(k * _f32(k_k))
    kk = kk * lax.rsqrt(jnp.maximum(jnp.sum(kk * kk, axis=-1, keepdims=True), 1e-24))
    k = k * (1.0 + (a - 1.0) * _f32(k_a))
    r_h, k_h, v_h, w_h, a_h = hs(r), hs(k), hs(v), hs(decay), hs(a)
    aa = -kk
    bb = kk * a_h

    def step(S, xs):
        r_t, w_t, k_t, v_t, a_t, b_t = xs
        Sa = jnp.einsum('bhvk,bhk->bhv', S, a_t)
        S = S * w_t[:, :, None, :] + Sa[..., None] * b_t[:, :, None, :] + v_t[..., None] * k_t[:, :, None, :]
        return S, jnp.einsum('bhvk,bhk->bhv', S, r_t)

    tm = lambda t: jnp.swapaxes(t, 0, 1)
    S_T, y = lax.scan(step, _f32(s0), (tm(r_h), tm(w_h), tm(k_h), tm(v_h), tm(aa), tm(bb)))
    y = tm(y)
    mean = jnp.mean(y, axis=-1, keepdims=True)
    var = jnp.mean(jnp.square(y - mean), axis=-1, keepdims=True)
    y = ((y - mean) * lax.rsqrt(var + LNX_EPS)).reshape(B, T, D_RWKV) * _f32(lnx_w) + _f32(lnx_b)
    bonus = jnp.sum(r_h * k_h * _f32(r_k), axis=-1, keepdims=True) * v_h
    y = y + bonus.reshape(B, T, D_RWKV)
    return y, S_T, f[:, -1:]


def _qkv(q, k, v, q_norm_w, k_norm_w):
    B, T, _ = q.shape
    q = _rms(q.reshape(B, T, N_KV_HEADS, Q_PER_KV, HEAD_DIM), q_norm_w)
    k = _rms(k.reshape(B, T, N_KV_HEADS, HEAD_DIM), k_norm_w)
    v = v.reshape(B, T, N_KV_HEADS, HEAD_DIM)
    return q, k, v


def _sink_attention(q, k, v, valid, sinks):
    s = jnp.einsum('...qkgd,...jkd->...kgqj', _f32(q), _f32(k)) * (HEAD_DIM ** -0.5)
    s = jnp.where(valid, s, NEG_INF)
    sink = _f32(sinks).reshape(N_KV_HEADS, Q_PER_KV, 1, 1)
    m = jnp.maximum(jnp.max(s, axis=-1, keepdims=True), sink)
    p = jnp.exp(s - m)
    denom = jnp.sum(p, axis=-1, keepdims=True) + jnp.exp(sink - m)
    return jnp.einsum('...kgqj,...jkd->...qkgd', (p / denom).astype(v.dtype), v)


def _swa_prompt(q, k, v, sinks):
    B, S = q.shape[:2]
    nb = S // WINDOW
    qb = q.reshape(B, nb, WINDOW, N_KV_HEADS, Q_PER_KV, HEAD_DIM)
    kb = k.reshape(B, nb, WINDOW, N_KV_HEADS, HEAD_DIM)
    vb = v.reshape(B, nb, WINDOW, N_KV_HEADS, HEAD_DIM)
    pad = jnp.zeros_like(kb[:, :1])
    kband = jnp.concatenate([jnp.concatenate([pad, kb[:, :-1]], axis=1), kb], axis=2)
    vband = jnp.concatenate([jnp.concatenate([pad, vb[:, :-1]], axis=1), vb], axis=2)
    blk = jnp.arange(nb)[:, None] * WINDOW
    qpos = blk + jnp.arange(WINDOW)[None]
    kpos = blk + jnp.arange(2 * WINDOW)[None] - WINDOW
    dist = qpos[:, :, None] - kpos[:, None, :]
    valid = (dist >= 0) & (dist < WINDOW) & (kpos[:, None, :] >= 0)
    o = _sink_attention(qb, kband, vband, valid[:, None, None], sinks)
    return o.reshape(B, S, D_ATTN)


def _swa_sample(q, k, v, ck, cv, sinks):
    B, T = q.shape[:2]
    wb = ck.shape[1]
    keys = jnp.concatenate([ck.astype(k.dtype), k], axis=1)
    vals = jnp.concatenate([cv.astype(v.dtype), v], axis=1)
    qpos = PAST_LEN + jnp.arange(T)
    kpos = PAST_LEN - wb + jnp.arange(wb + T)
    dist = qpos[:, None] - kpos[None, :]
    valid = (dist >= 0) & (dist < WINDOW)
    o = _sink_attention(q, keys, vals, valid, sinks)
    return o.reshape(B, T, D_ATTN), keys[:, -wb:], vals[:, -wb:]


def _merge(x, o_r, z_r, o_a, z_a, w_out, pl, g_ple, w_ple_gate, w_ple_proj):
    o = jnp.concatenate([o_r.astype(x.dtype) * jax.nn.silu(z_r), o_a.astype(x.dtype) * jax.nn.silu(z_a)], axis=-1)
    h = x + o @ w_out
    gate = jax.nn.sigmoid(_rms(h, g_ple) @ w_ple_gate)
    return h + gate * (pl.astype(x.dtype) @ w_ple_proj)


def setup_inputs(seed: int = 0) -> dict:
    key = jax.random.key(seed)
    ks = iter(jax.random.split(key, 40))
    nrm = lambda shape, s: jax.random.normal(next(ks), shape, jnp.float32) * s
    uni = lambda shape, lo, hi: jax.random.uniform(next(ks), shape, jnp.float32, lo, hi)
    wb = min(WINDOW, PAST_LEN)
    return {
        'x_prompt': nrm((BATCH, SEQ, D_MODEL), 1.0),
        'x_sample': nrm((DEC_BATCH, DEC_SEQ, D_MODEL), 1.0),
        'state_rwkv': nrm((DEPTH, DEC_BATCH, N_RWKV_HEADS, HEAD_DIM, HEAD_DIM), 1.0),
        'state_shift': nrm((DEPTH, DEC_BATCH, 1, D_SHIFT), 1.0),
        'cache_k': nrm((DEPTH, DEC_BATCH, wb, N_KV_HEADS, HEAD_DIM), 1.0),
        'cache_v': nrm((DEPTH, DEC_BATCH, wb, N_KV_HEADS, HEAD_DIM), 1.0),
        'p_prompt': nrm((DEPTH, BATCH, SEQ, D_PLE), 1.0),
        'p_sample': nrm((DEPTH, DEC_BATCH, DEC_SEQ, D_PLE), 1.0),
        'g_norm': 1.0 + nrm((DEPTH, D_MODEL), 0.02),
        'w_in': nrm((DEPTH, D_MODEL, D_IN), D_MODEL ** -0.5),
        'mu_shift': uni((DEPTH, D_SHIFT), 0.0, 1.0),
        'w0': uni((DEPTH, D_RWKV), -6.0, -1.0),
        'w_dec2': nrm((DEPTH, D_DECAY_LORA, D_RWKV), 0.1),
        'a0': nrm((DEPTH, D_RWKV), 0.5),
        'w_a2': nrm((DEPTH, D_AAA_LORA, D_RWKV), 0.1),
        'k_k': 0.85 + nrm((DEPTH, D_RWKV), 0.05),
        'k_a': 1.0 + nrm((DEPTH, D_RWKV), 0.05),
        'r_k': nrm((DEPTH, N_RWKV_HEADS, HEAD_DIM), 0.1),
        'lnx_w': 1.0 + nrm((DEPTH, D_RWKV), 0.02),
        'lnx_b': nrm((DEPTH, D_RWKV), 0.02),
        'q_norm_w': 1.0 + nrm((DEPTH, HEAD_DIM), 0.02),
        'k_norm_w': 1.0 + nrm((DEPTH, HEAD_DIM), 0.02),
        'sinks': nrm((DEPTH, N_Q_HEADS), 0.5),
        'w_out': nrm((DEPTH, D_MODEL, D_MODEL), D_MODEL ** -0.5),
        'g_ple': 1.0 + nrm((DEPTH, D_MODEL), 0.02),
        'w_ple_gate': nrm((DEPTH, D_MODEL, D_MODEL), D_MODEL ** -0.5),
        'w_ple_proj': nrm((DEPTH, D_PLE, D_MODEL), D_PLE ** -0.5),
    }


def reference(x_prompt, x_sample, state_rwkv, state_shift, cache_k, cache_v, p_prompt, p_sample,
              g_norm, w_in, mu_shift, w0, w_dec2, a0, w_a2, k_k, k_a, r_k, lnx_w, lnx_b,
              q_norm_w, k_norm_w, sinks, w_out, g_ple, w_ple_gate, w_ple_proj):
    xp, xs = x_prompt, x_sample
    bp, bs = xp.shape[0], xs.shape[0]
    s_p_list, s_s_list, sh_p_list, sh_s_list = [], [], [], []
    kp_list, ks_list, vp_list, vs_list = [], [], [], []
    for i in range(DEPTH):
        rw = (mu_shift[i], w0[i], w_dec2[i], a0[i], w_a2[i], k_k[i], k_a[i], r_k[i], lnx_w[i], lnx_b[i])
        f, z_r, q, k, v, z_a = _in_proj(xp, g_norm[i], w_in[i])
        s0 = jnp.zeros((bp, N_RWKV_HEADS, HEAD_DIM, HEAD_DIM), jnp.float32)
        o_r, S_p, sh_p = _rwkv_branch(f, jnp.zeros_like(f[:, :1]), s0, *rw)
        qh, kh, vh = _qkv(q, k, v, q_norm_w[i], k_norm_w[i])
        o_a = _swa_prompt(qh, kh, vh, sinks[i])
        wbp = min(WINDOW, xp.shape[1])
        kp_list.append(kh[:, -wbp:].astype(cache_k.dtype))
        vp_list.append(vh[:, -wbp:].astype(cache_v.dtype))
        s_p_list.append(S_p.astype(state_rwkv.dtype))
        sh_p_list.append(sh_p.astype(state_shift.dtype))
        xp = _merge(xp, o_r, z_r, o_a, z_a, w_out[i], p_prompt[i], g_ple[i], w_ple_gate[i], w_ple_proj[i])
        f, z_r, q, k, v, z_a = _in_proj(xs, g_norm[i], w_in[i])
        o_r, S_s, sh_s = _rwkv_branch(f, state_shift[i], state_rwkv[i], *rw)
        qh, kh, vh = _qkv(q, k, v, q_norm_w[i], k_norm_w[i])
        o_a, k_buf, v_buf = _swa_sample(qh, kh, vh, cache_k[i], cache_v[i], sinks[i])
        ks_list.append(k_buf.astype(cache_k.dtype))
        vs_list.append(v_buf.astype(cache_v.dtype))
        s_s_list.append(S_s.astype(state_rwkv.dtype))
        sh_s_list.append(sh_s.astype(state_shift.dtype))
        xs = _merge(xs, o_r, z_r, o_a, z_a, w_out[i], p_sample[i], g_ple[i], w_ple_gate[i], w_ple_proj[i])
    return (xp, xs,
            jnp.stack(s_p_list), jnp.stack(s_s_list),
            jnp.stack(sh_p_list), jnp.stack(sh_s_list),
            jnp.stack(kp_list), jnp.stack(ks_list),
            jnp.stack(vp_list), jnp.stack(vs_list))
```

```python
import functools
import math

import jax
import jax.numpy as jnp
from jax import lax
from jax.experimental import pallas as pl
from jax.experimental.pallas import tpu as pltpu

F32 = jnp.float32
BF16 = jnp.bfloat16

D_MODEL = 1024
HEAD_DIM = 64
D_RWKV = 512
D_ATTN = 512
N_KV_HEADS = 2
N_Q_HEADS = 8
Q_PER_KV = N_Q_HEADS // N_KV_HEADS
D_KV = N_KV_HEADS * HEAD_DIM
WINDOW = 128
D_LORA = 64
D_SHIFT = 3 * D_RWKV + 2 * D_LORA
D_PLE = 256
D_IN = D_SHIFT + D_RWKV + D_ATTN + 2 * D_KV + D_ATTN
NORM_EPS = 1e-6
LNX_EPS = 64e-5
NEG_INF = -1e30

LANES = 128
N_PAIRS = D_RWKV // LANES
PROMPT_CHUNK = 64
VMEM_LIMIT = 56 * 1024 * 1024


def _dot(a, b):
    return jnp.dot(a.astype(BF16), b.astype(BF16), preferred_element_type=F32)


def _dot_nt(a, b):
    return lax.dot_general(a.astype(BF16), b.astype(BF16), (((1,), (1,)), ((), ())), preferred_element_type=F32)


def _dot_tn(a, b):
    return lax.dot_general(a.astype(BF16), b.astype(BF16), (((0,), (0,)), ((), ())), preferred_element_type=F32)


def _split_dot(x, w):
    hi = x.astype(BF16)
    lo = (x - hi.astype(F32)).astype(BF16)
    return jnp.dot(hi, w, preferred_element_type=F32) + jnp.dot(lo, w, preferred_element_type=F32)


def _split3_dot_left(w, x):
    hi = x.astype(BF16)
    r1 = x - hi.astype(F32)
    mid = r1.astype(BF16)
    lo = (r1 - mid.astype(F32)).astype(BF16)
    d = lambda t: jnp.dot(w, t, preferred_element_type=F32)
    return d(hi) + d(mid) + d(lo)


def _head_ones():
    r = lax.broadcasted_iota(jnp.int32, (LANES, LANES), 0) // HEAD_DIM
    c = lax.broadcasted_iota(jnp.int32, (LANES, LANES), 1) // HEAD_DIM
    return (r == c).astype(BF16)


def _head_sum(x, ones):
    tiles = [_split_dot(x[:, i:i + LANES], ones) for i in range(0, x.shape[1], LANES)]
    return tiles[0] if len(tiles) == 1 else jnp.concatenate(tiles, axis=1)


def _rms_rows(x, g):
    return x * lax.rsqrt(jnp.mean(x * x, axis=-1, keepdims=True) + NORM_EPS) * g


_IN_SPLITS = (D_SHIFT, D_RWKV, D_ATTN, D_KV, D_KV, D_ATTN)


def _in_proj_kernel(x_ref, g_ref, w_ref, *out_refs):
    h = _dot(_rms_rows(x_ref[...], g_ref[...]), w_ref[...])
    off = 0
    for o_ref, width in zip(out_refs, _IN_SPLITS):
        o_ref[...] = h[:, off:off + width]
        off += width


def _in_proj(x, g_norm, w_in_bf16, tm):
    m = x.shape[0]
    return pl.pallas_call(
        _in_proj_kernel,
        grid=(m // tm,),
        in_specs=[pl.BlockSpec((tm, D_MODEL), lambda i: (i, 0)),
                  pl.BlockSpec((1, D_MODEL), lambda i: (0, 0)),
                  pl.BlockSpec((D_MODEL, D_IN), lambda i: (0, 0))],
        out_specs=[pl.BlockSpec((tm, w), lambda i: (i, 0)) for w in _IN_SPLITS],
        out_shape=[jax.ShapeDtypeStruct((m, w), F32) for w in _IN_SPLITS],
        compiler_params=pltpu.CompilerParams(dimension_semantics=("arbitrary",), vmem_limit_bytes=VMEM_LIMIT),
        name="in_proj",
    )(x, g_norm, w_in_bf16)


def _rwkv_chunk(at, rt, bt, kt, v, w_end, s_ref, masks, n_double):
    c = at.shape[0]
    half0, bd_strict, bd_incl, eye2, state_mask = masks
    ys = []
    for p in range(N_PAIRS):
        sl = slice(p * LANES, (p + 1) * LANES)
        at_p, rt_p, bt_p, kt_p, v_p = at[:, sl], rt[:, sl], bt[:, sl], kt[:, sl], v[:, sl]
        at2 = jnp.concatenate([at_p, at_p], axis=0).astype(BF16)
        rt2 = jnp.concatenate([rt_p, rt_p], axis=0).astype(BF16)
        v2 = jnp.concatenate([v_p, v_p], axis=0).astype(BF16)
        bt_m = jnp.concatenate([jnp.where(half0, bt_p, 0.0), jnp.where(half0, 0.0, bt_p)], axis=0).astype(BF16)
        kt_m = jnp.concatenate([jnp.where(half0, kt_p, 0.0), jnp.where(half0, 0.0, kt_p)], axis=0).astype(BF16)
        g_ab = jnp.where(bd_strict, _dot_nt(at2, bt_m), 0.0)
        g_ak = jnp.where(bd_strict, _dot_nt(at2, kt_m), 0.0)
        g_rb = jnp.where(bd_incl, _dot_nt(rt2, bt_m), 0.0)
        g_rk = jnp.where(bd_incl, _dot_nt(rt2, kt_m), 0.0)
        t_inv = eye2 + g_ab
        a_pow = g_ab
        for _ in range(n_double):
            a_pow = _dot(a_pow, a_pow)
            t_inv = t_inv + _dot(a_pow, t_inv)
        s_bd = s_ref[p]
        x_s = _dot_nt(at2, s_bd) + _dot(g_ak, v2)
        p_s = _dot(t_inv, x_s)
        y_s = _dot_nt(rt2, s_bd) + _dot(g_rb, p_s) + _dot(g_rk, v2)
        ys.append(jnp.where(half0, y_s[:c], y_s[c:]))
        p_p = jnp.where(half0, p_s[:c], p_s[c:])
        upd = _dot_tn(jnp.concatenate([p_p, v_p], axis=0), jnp.concatenate([bt_p, kt_p], axis=0))
        s_ref[p] = w_end[:, sl] * (s_bd + jnp.where(state_mask, upd, 0.0))
    return jnp.concatenate(ys, axis=1)


def _rwkv_kernel(chunk, has_state, f_ref, prev0_ref, mu_ref, w0_ref, a0_ref, lora_ref, kk_ref, ka_ref, rk_ref,
                 lnw_ref, lnb_ref, *rest):
    if has_state:
        s0_ref, o_ref, s_out_ref, s_scr, prev_scr = rest
    else:
        o_ref, s_out_ref, s_scr, prev_scr = rest
    j = pl.program_id(1)
    tb = f_ref.shape[0]

    @pl.when(j == 0)
    def _():
        prev_scr[...] = prev0_ref[0]
        if has_state:
            s_scr[...] = s0_ref[0]
        else:
            s_scr[...] = jnp.zeros(s_scr.shape, F32)

    f = f_ref[...]
    row = lax.broadcasted_iota(jnp.int32, (tb, 1), 0)
    f_prev = jnp.where(row == 0, prev_scr[...], pltpu.roll(f, 1, 0))
    prev_scr[...] = f[tb - 1:tb, :]
    fs = f + (f_prev - f) * mu_ref[...]
    r = fs[:, 0:D_RWKV]
    k = fs[:, D_RWKV:2 * D_RWKV]
    v = fs[:, 2 * D_RWKV:3 * D_RWKV]
    wa = fs[:, 3 * D_RWKV:D_SHIFT]
    lane = lax.broadcasted_iota(jnp.int32, (1, LANES), 1)
    lora = _dot(jnp.where(lane < D_LORA, jnp.tanh(wa), wa), lora_ref[...])
    lw = (-math.exp(-0.5)) * jax.nn.sigmoid(w0_ref[...] + lora[:, 0:D_RWKV])
    a_sig = jax.nn.sigmoid(a0_ref[...] + lora[:, D_RWKV:2 * D_RWKV])
    ones = _head_ones()
    kk = k * kk_ref[...]
    kk = kk * lax.rsqrt(jnp.maximum(_head_sum(kk * kk, ones), 1e-24))
    k2 = k * (1.0 + (a_sig - 1.0) * ka_ref[...])
    a_vec = -kk
    b_vec = kk * a_sig

    c2 = 2 * chunk
    ri = lax.broadcasted_iota(jnp.int32, (c2, c2), 0)
    ci = lax.broadcasted_iota(jnp.int32, (c2, c2), 1)
    same = (ri // chunk) == (ci // chunk)
    half0 = lane < HEAD_DIM
    sr = lax.broadcasted_iota(jnp.int32, (LANES, LANES), 0) // HEAD_DIM
    sc = lax.broadcasted_iota(jnp.int32, (LANES, LANES), 1) // HEAD_DIM
    masks = (half0, same & (ci < ri), same & (ci <= ri), (ri == ci).astype(F32), sr == sc)
    ti = lax.broadcasted_iota(jnp.int32, (chunk, chunk), 0)
    tj = lax.broadcasted_iota(jnp.int32, (chunk, chunk), 1)
    tri = (tj <= ti).astype(BF16)
    n_double = max(int(math.log2(chunk)) - 1, 0)

    ys = []
    for c0 in range(0, tb, chunk):
        rows = slice(c0, c0 + chunk)
        lw_c = lw[rows]
        cum = _split3_dot_left(tri, lw_c)
        e_cum = jnp.exp(cum)
        e_prev = jnp.exp(cum - lw_c)
        e_inv = jnp.exp(-cum)
        ys.append(_rwkv_chunk(a_vec[rows] * e_prev, r[rows] * e_cum, b_vec[rows] * e_inv, k2[rows] * e_inv,
                              v[rows], e_cum[chunk - 1:chunk, :], s_scr, masks, n_double))
    y = ys[0] if len(ys) == 1 else jnp.concatenate(ys, axis=0)

    inv_n = 1.0 / HEAD_DIM
    yc = y - _head_sum(y, ones) * inv_n
    var = _head_sum(yc * yc, ones) * inv_n
    yn = yc * lax.rsqrt(var + LNX_EPS) * lnw_ref[...] + lnb_ref[...]
    bonus = _head_sum(r * k2 * rk_ref[...], ones) * v
    o_ref[...] = yn + bonus
    s_out_ref[0] = s_scr[...]


def _rwkv(f, prev0, s0_bd, vecs, lora_w, batch, tb, chunk):
    m = f.shape[0]
    nblk = m // batch // tb
    has_state = s0_bd is not None
    vec_spec = lambda n: pl.BlockSpec((1, n), lambda b, j: (0, 0))
    mu, w0, a0, k_k, k_a, r_k, lnx_w, lnx_b = vecs
    in_specs = [pl.BlockSpec((tb, D_SHIFT), lambda b, j: (b * nblk + j, 0)),
                pl.BlockSpec((1, 1, D_SHIFT), lambda b, j: (b, 0, 0)),
                vec_spec(D_SHIFT), vec_spec(D_RWKV), vec_spec(D_RWKV),
                pl.BlockSpec((LANES, 2 * D_RWKV), lambda b, j: (0, 0)),
                vec_spec(D_RWKV), vec_spec(D_RWKV), vec_spec(D_RWKV), vec_spec(D_RWKV), vec_spec(D_RWKV)]
    args = [f, prev0, mu, w0, a0, lora_w, k_k, k_a, r_k, lnx_w, lnx_b]
    state_spec = pl.BlockSpec((1, N_PAIRS, LANES, LANES), lambda b, j: (b, 0, 0, 0))
    if has_state:
        in_specs.append(state_spec)
        args.append(s0_bd)
    return pl.pallas_call(
        functools.partial(_rwkv_kernel, chunk, has_state),
        grid=(batch, nblk),
        in_specs=in_specs,
        out_specs=[pl.BlockSpec((tb, D_RWKV), lambda b, j: (b * nblk + j, 0)), state_spec],
        out_shape=[jax.ShapeDtypeStruct((m, D_RWKV), F32),
                   jax.ShapeDtypeStruct((batch, N_PAIRS, LANES, LANES), F32)],
        scratch_shapes=[pltpu.VMEM((N_PAIRS, LANES, LANES), F32), pltpu.VMEM((1, D_SHIFT), F32)],
        compiler_params=pltpu.CompilerParams(dimension_semantics=("arbitrary", "arbitrary"),
                                             vmem_limit_bytes=VMEM_LIMIT),
        name="rwkv",
    )(*args)


def _qk_norm(q, k, qw, kw, ones):
    inv_n = 1.0 / HEAD_DIM
    qn = q * lax.rsqrt(_head_sum(q * q, ones) * inv_n + NORM_EPS) * (qw * (HEAD_DIM ** -0.5))
    kn = k * lax.rsqrt(_head_sum(k * k, ones) * inv_n + NORM_EPS) * kw
    return qn, kn


def _attend(qn, score_blocks, sinks_ref, half0):
    tiles = []
    for t in range(D_ATTN // LANES):
        q_t = qn[:, t * LANES:(t + 1) * LANES]
        outs = []
        for e in range(2):
            h = 2 * t + e
            g = h // Q_PER_KV
            qm = jnp.where(half0 if e == 0 else jnp.logical_not(half0), q_t, 0.0).astype(BF16)
            sink = sinks_ref[h:h + 1, 0:1]
            scores = []
            for keys, keys_rot, _, _, valid in score_blocks:
                s = _dot_nt(qm, keys if g == e else keys_rot)
                scores.append(jnp.where(valid, s, NEG_INF))
            m = sink
            for s in scores:
                m = jnp.maximum(m, jnp.max(s, axis=-1, keepdims=True))
            denom = jnp.exp(sink - m)
            acc = None
            for s, (_, _, vals, vals_rot, _) in zip(scores, score_blocks):
                pr = jnp.exp(s - m)
                denom = denom + jnp.sum(pr, axis=-1, keepdims=True)
                pv = _dot(pr, vals if g == e else vals_rot)
                acc = pv if acc is None else acc + pv
            outs.append(acc * (1.0 / denom))
        tiles.append(jnp.where(half0, outs[0], outs[1]))
    return jnp.concatenate(tiles, axis=1)


def _attn_prompt_kernel(q_ref, k_ref, v_ref, qw_ref, kw_ref, sinks_ref, o_ref, kc_ref, vc_ref, kprev, vprev):
    j = pl.program_id(1)

    @pl.when(j == 0)
    def _():
        kprev[...] = jnp.zeros(kprev.shape, F32)
        vprev[...] = jnp.zeros(vprev.shape, F32)

    ones = _head_ones()
    half0 = lax.broadcasted_iota(jnp.int32, (1, LANES), 1) < HEAD_DIM
    v = v_ref[...]
    qn, kn = _qk_norm(q_ref[...], k_ref[...], qw_ref[...], kw_ref[...], ones)
    keys = jnp.concatenate([kprev[...], kn], axis=0)
    vals = jnp.concatenate([vprev[...], v], axis=0)
    qi = lax.broadcasted_iota(jnp.int32, (WINDOW, 2 * WINDOW), 0)
    kj = lax.broadcasted_iota(jnp.int32, (WINDOW, 2 * WINDOW), 1)
    valid = (kj > qi) & (kj <= qi + WINDOW) & ((kj >= WINDOW) | (j > 0))
    blocks = [(keys.astype(BF16), pltpu.roll(keys, HEAD_DIM, 1).astype(BF16),
               vals.astype(BF16), pltpu.roll(vals, HEAD_DIM, 1).astype(BF16), valid)]
    o_ref[...] = _attend(qn, blocks, sinks_ref, half0)
    kprev[...] = kn
    vprev[...] = v
    kc_ref[0] = kn
    vc_ref[0] = v


def _attn_prompt(q, k, v, qw, kw, sinks_b, batch):
    m = q.shape[0]
    nblk = m // batch // WINDOW
    tok = lambda w: pl.BlockSpec((WINDOW, w), lambda b, j: (b * nblk + j, 0))
    const = lambda r, w: pl.BlockSpec((r, w), lambda b, j: (0, 0))
    cache = pl.BlockSpec((1, WINDOW, D_KV), lambda b, j: (b, 0, 0))
    return pl.pallas_call(
        _attn_prompt_kernel,
        grid=(batch, nblk),
        in_specs=[tok(D_ATTN), tok(D_KV), tok(D_KV), const(1, D_ATTN), const(1, D_KV), const(N_Q_HEADS, LANES)],
        out_specs=[tok(D_ATTN), cache, cache],
        out_shape=[jax.ShapeDtypeStruct((m, D_ATTN), F32),
                   jax.ShapeDtypeStruct((batch, WINDOW, D_KV), F32),
                   jax.ShapeDtypeStruct((batch, WINDOW, D_KV), F32)],
        scratch_shapes=[pltpu.VMEM((WINDOW, D_KV), F32), pltpu.VMEM((WINDOW, D_KV), F32)],
        compiler_params=pltpu.CompilerParams(dimension_semantics=("arbitrary", "arbitrary")),
        name="attn_prompt",
    )(q, k, v, qw, kw, sinks_b)


SAMPLE_GROUP = 8


def _attn_sample_kernel(t_new, q_ref, k_ref, v_ref, ck_ref, cv_ref, qw_ref, kw_ref, sinks_ref,
                        o_ref, ko_ref, vo_ref):
    ones = _head_ones()
    half0 = lax.broadcasted_iota(jnp.int32, (1, LANES), 1) < HEAD_DIM
    wb = ck_ref.shape[1]
    qi_c = lax.broadcasted_iota(jnp.int32, (t_new, wb), 0)
    kj_c = lax.broadcasted_iota(jnp.int32, (t_new, wb), 1)
    valid_c = (qi_c + wb - kj_c) < WINDOW
    qi_n = lax.broadcasted_iota(jnp.int32, (t_new, t_new), 0)
    kj_n = lax.broadcasted_iota(jnp.int32, (t_new, t_new), 1)
    valid_n = kj_n <= qi_n
    rot = lambda x: pltpu.roll(x, HEAD_DIM, 1).astype(BF16)
    for i in range(q_ref.shape[0]):
        v = v_ref[i]
        qn, kn = _qk_norm(q_ref[i], k_ref[i], qw_ref[...], kw_ref[...], ones)
        ck = ck_ref[i]
        cv = cv_ref[i]
        blocks = [(ck.astype(BF16), rot(ck), cv.astype(BF16), rot(cv), valid_c),
                  (kn.astype(BF16), rot(kn), v.astype(BF16), rot(v), valid_n)]
        o_ref[i] = _attend(qn, blocks, sinks_ref, half0)
        ko_ref[i] = jnp.concatenate([ck[t_new:], kn], axis=0)
        vo_ref[i] = jnp.concatenate([cv[t_new:], v], axis=0)


def _attn_sample(q, k, v, ck, cv, qw, kw, sinks_b):
    b, t_new, _ = q.shape
    wb = ck.shape[1]
    gb = SAMPLE_GROUP
    spec = lambda r, w: pl.BlockSpec((gb, r, w), lambda i: (i, 0, 0))
    const = lambda r, w: pl.BlockSpec((r, w), lambda i: (0, 0))
    return pl.pallas_call(
        functools.partial(_attn_sample_kernel, t_new),
        grid=(b // gb,),
        in_specs=[spec(t_new, D_ATTN), spec(t_new, D_KV), spec(t_new, D_KV), spec(wb, D_KV), spec(wb, D_KV),
                  const(1, D_ATTN), const(1, D_KV), const(N_Q_HEADS, LANES)],
        out_specs=[spec(t_new, D_ATTN), spec(wb, D_KV), spec(wb, D_KV)],
        out_shape=[jax.ShapeDtypeStruct((b, t_new, D_ATTN), F32),
                   jax.ShapeDtypeStruct((b, wb, D_KV), F32),
                   jax.ShapeDtypeStruct((b, wb, D_KV), F32)],
        compiler_params=pltpu.CompilerParams(dimension_semantics=("arbitrary",)),
        name="attn_sample",
    )(q, k, v, ck, cv, qw, kw, sinks_b)


def _merge_kernel(x_ref, or_ref, zr_ref, oa_ref, za_ref, p_ref, wout_ref, g_ref, wgate_ref, wproj_ref, y_ref):
    gr = or_ref[...] * jax.nn.silu(zr_ref[...])
    ga = oa_ref[...] * jax.nn.silu(za_ref[...])
    h = x_ref[...] + _dot(gr, wout_ref[0:D_RWKV, :]) + _dot(ga, wout_ref[D_RWKV:D_MODEL, :])
    gate = jax.nn.sigmoid(_dot(_rms_rows(h, g_ref[...]), wgate_ref[...]))
    y_ref[...] = h + gate * _dot(p_ref[...], wproj_ref[...])


def _merge(x, o_r, z_r, o_a, z_a, p, w_out, g_ple, w_gate, w_proj, tm):
    m = x.shape[0]
    tok = lambda w: pl.BlockSpec((tm, w), lambda i: (i, 0))
    const = lambda r, w: pl.BlockSpec((r, w), lambda i: (0, 0))
    return pl.pallas_call(
        _merge_kernel,
        grid=(m // tm,),
        in_specs=[tok(D_MODEL), tok(D_RWKV), tok(D_RWKV), tok(D_ATTN), tok(D_ATTN), tok(D_PLE),
                  const(D_MODEL, D_MODEL), const(1, D_MODEL), const(D_MODEL, D_MODEL), const(D_PLE, D_MODEL)],
        out_specs=tok(D_MODEL),
        out_shape=jax.ShapeDtypeStruct((m, D_MODEL), F32),
        compiler_params=pltpu.CompilerParams(dimension_semantics=("arbitrary",), vmem_limit_bytes=VMEM_LIMIT),
        name="merge",
    )(x, o_r, z_r, o_a, z_a, p, w_out, g_ple, w_gate, w_proj)


def _state_to_pairs(s):
    b = s.shape[0]
    s = s.reshape(b, N_PAIRS, 2, HEAD_DIM, HEAD_DIM)
    z = jnp.zeros_like(s[:, :, 0])
    top = jnp.concatenate([s[:, :, 0], z], axis=-1)
    bot = jnp.concatenate([z, s[:, :, 1]], axis=-1)
    return jnp.concatenate([top, bot], axis=-2)


def _pairs_to_state(s_bd):
    b = s_bd.shape[0]
    s0 = s_bd[:, :, :HEAD_DIM, :HEAD_DIM]
    s1 = s_bd[:, :, HEAD_DIM:, HEAD_DIM:]
    return jnp.stack([s0, s1], axis=2).reshape(b, 2 * N_PAIRS, HEAD_DIM, HEAD_DIM)


def kernel(x_prompt, x_sample, state_rwkv, state_shift, cache_k, cache_v, p_prompt, p_sample, g_norm, w_in, mu_shift, w0, w_dec2, a0, w_a2, k_k, k_a, r_k, lnx_w, lnx_b, q_norm_w, k_norm_w, sinks, w_out, g_ple, w_ple_gate, w_ple_proj):
    depth = w_in.shape[0]
    bp, seq, _ = x_prompt.shape
    bs, dec, _ = x_sample.shape
    wb = cache_k.shape[2]
    xp = x_prompt.reshape(bp * seq, D_MODEL)
    xs = x_sample.reshape(bs * dec, D_MODEL)
    outs = [[] for _ in range(8)]
    for i in range(depth):
        w_in_b = w_in[i].astype(BF16)
        w_out_b = w_out[i].astype(BF16)
        w_gate_b = w_ple_gate[i].astype(BF16)
        w_proj_b = w_ple_proj[i].astype(BF16)
        zl = jnp.zeros((D_LORA, D_RWKV), F32)
        lora_w = jnp.concatenate([jnp.concatenate([w_dec2[i], zl], axis=1),
                                  jnp.concatenate([zl, w_a2[i]], axis=1)], axis=0).astype(BF16)
        row = lambda t: t.reshape(1, -1)
        vecs = (row(mu_shift[i]), row(w0[i]), row(a0[i]), row(k_k[i]), row(k_a[i]), row(r_k[i]),
                row(lnx_w[i]), row(lnx_b[i]))
        qw = row(jnp.tile(q_norm_w[i], N_Q_HEADS))
        kw = row(jnp.tile(k_norm_w[i], N_KV_HEADS))
        sinks_b = jnp.broadcast_to(sinks[i][:, None], (N_Q_HEADS, LANES))
        gn, gp = row(g_norm[i]), row(g_ple[i])

        f, z_r, q, k, v, z_a = _in_proj(xp, gn, w_in_b, 512)
        o_r, s_bd = _rwkv(f, jnp.zeros((bp, 1, D_SHIFT), F32), None, vecs, lora_w, bp, WINDOW, PROMPT_CHUNK)
        o_a, kc, vc = _attn_prompt(q, k, v, qw, kw, sinks_b, bp)
        outs[0].append(_pairs_to_state(s_bd))
        outs[2].append(f.reshape(bp, seq, D_SHIFT)[:, -1:])
        outs[4].append(kc.reshape(bp, WINDOW, N_KV_HEADS, HEAD_DIM))
        outs[6].append(vc.reshape(bp, WINDOW, N_KV_HEADS, HEAD_DIM))
        xp = _merge(xp, o_r, z_r, o_a, z_a, p_prompt[i].reshape(bp * seq, D_PLE), w_out_b, gp, w_gate_b, w_proj_b, 512)

        f, z_r, q, k, v, z_a = _in_proj(xs, gn, w_in_b, 512)
        o_r, s_bd = _rwkv(f, state_shift[i], _state_to_pairs(state_rwkv[i]), vecs, lora_w, bs, dec, dec)
        o_a, k_buf, v_buf = _attn_sample(q.reshape(bs, dec, D_ATTN), k.reshape(bs, dec, D_KV), v.reshape(bs, dec, D_KV),
                                         cache_k[i].reshape(bs, wb, D_KV), cache_v[i].reshape(bs, wb, D_KV),
                                         qw, kw, sinks_b)
        outs[1].append(_pairs_to_state(s_bd))
        outs[3].append(f.reshape(bs, dec, D_SHIFT)[:, -1:])
        outs[5].append(k_buf.reshape(bs, wb, N_KV_HEADS, HEAD_DIM))
        outs[7].append(v_buf.reshape(bs, wb, N_KV_HEADS, HEAD_DIM))
        xs = _merge(xs, o_r, z_r, o_a.reshape(bs * dec, D_ATTN), z_a, p_sample[i].reshape(bs * dec, D_PLE),
                    w_out_b, gp, w_gate_b, w_proj_b, 512)
    st = lambda l: jnp.stack(l)
    return (xp.reshape(bp, seq, D_MODEL), xs.reshape(bs, dec, D_MODEL),
            st(outs[0]), st(outs[1]), st(outs[2]), st(outs[3]), st(outs[4]), st(outs[5]), st(outs[6]), st(outs[7]))
```

```python
import functools
import math

import jax
import jax.numpy as jnp
from jax import lax
from jax.experimental import pallas as pl
from jax.experimental.pallas import tpu as pltpu

F32 = jnp.float32
BF16 = jnp.bfloat16

D_MODEL = 1024
HEAD_DIM = 64
D_RWKV = 512
D_ATTN = 512
N_KV_HEADS = 2
N_Q_HEADS = 8
Q_PER_KV = N_Q_HEADS // N_KV_HEADS
D_KV = N_KV_HEADS * HEAD_DIM
WINDOW = 128
D_LORA = 64
D_SHIFT = 3 * D_RWKV + 2 * D_LORA
D_PLE = 256
D_IN = D_SHIFT + D_RWKV + D_ATTN + 2 * D_KV + D_ATTN
NORM_EPS = 1e-6
LNX_EPS = 64e-5
NEG_INF = -1e30

LANES = 128
N_PAIRS = D_RWKV // LANES
PROMPT_CHUNK = 64
SAMPLE_GROUP = 8
VMEM_LIMIT = 56 * 1024 * 1024


def _dot(a, b):
    return jnp.dot(a.astype(BF16), b.astype(BF16), preferred_element_type=F32)


def _dot_nt(a, b):
    return lax.dot_general(a.astype(BF16), b.astype(BF16), (((1,), (1,)), ((), ())), preferred_element_type=F32)


def _dot_tn(a, b):
    return lax.dot_general(a.astype(BF16), b.astype(BF16), (((0,), (0,)), ((), ())), preferred_element_type=F32)


def _split_dot(x, w):
    hi = x.astype(BF16)
    lo = (x - hi.astype(F32)).astype(BF16)
    return jnp.dot(hi, w, preferred_element_type=F32) + jnp.dot(lo, w, preferred_element_type=F32)


def _split3_dot_left(w, x):
    hi = x.astype(BF16)
    r1 = x - hi.astype(F32)
    mid = r1.astype(BF16)
    lo = (r1 - mid.astype(F32)).astype(BF16)
    d = lambda t: jnp.dot(w, t, preferred_element_type=F32)
    return d(hi) + d(mid) + d(lo)


def _head_ones():
    r = lax.broadcasted_iota(jnp.int32, (LANES, LANES), 0) // HEAD_DIM
    c = lax.broadcasted_iota(jnp.int32, (LANES, LANES), 1) // HEAD_DIM
    return (r == c).astype(BF16)


def _head_sum(x, ones):
    tiles = [_split_dot(x[:, i:i + LANES], ones) for i in range(0, x.shape[1], LANES)]
    return tiles[0] if len(tiles) == 1 else jnp.concatenate(tiles, axis=1)


def _rms_rows(x, g):
    return x * lax.rsqrt(jnp.mean(x * x, axis=-1, keepdims=True) + NORM_EPS) * g


_IN_SPLITS = (D_SHIFT, D_RWKV, D_ATTN, D_KV, D_KV, D_ATTN)


def _in_proj_kernel(x_ref, g_ref, w_ref, *out_refs):
    h = _dot(_rms_rows(x_ref[...], g_ref[...]), w_ref[...])
    off = 0
    for o_ref, width in zip(out_refs, _IN_SPLITS):
        o_ref[...] = h[:, off:off + width]
        off += width


def _in_proj(x, g_norm, w_in_bf16, tm):
    m = x.shape[0]
    return pl.pallas_call(
        _in_proj_kernel,
        grid=(m // tm,),
        in_specs=[pl.BlockSpec((tm, D_MODEL), lambda i: (i, 0)),
                  pl.BlockSpec((1, D_MODEL), lambda i: (0, 0)),
                  pl.BlockSpec((D_MODEL, D_IN), lambda i: (0, 0))],
        out_specs=[pl.BlockSpec((tm, w), lambda i: (i, 0)) for w in _IN_SPLITS],
        out_shape=[jax.ShapeDtypeStruct((m, w), F32) for w in _IN_SPLITS],
        compiler_params=pltpu.CompilerParams(dimension_semantics=("arbitrary",), vmem_limit_bytes=VMEM_LIMIT),
        name="in_proj",
    )(x, g_norm, w_in_bf16)


def _seg_rows(x, off, seg, cm):
    return x[off:off + seg], x[cm + off:cm + off + seg]


def _rwkv_recurrence(at, rt, bt, kt, v, e_cum, s_scr, cm, seg, segs_per_state):
    n_rows = at.shape[0]
    n_blk = n_rows // cm
    c2 = 2 * cm
    lane = lax.broadcasted_iota(jnp.int32, (1, LANES), 1)
    half0 = lane < HEAD_DIM
    ri = lax.broadcasted_iota(jnp.int32, (c2, c2), 0)
    ci = lax.broadcasted_iota(jnp.int32, (c2, c2), 1)
    same = (ri // seg) == (ci // seg)
    bd_strict = same & (ci < ri)
    bd_incl = same & (ci <= ri)
    eye2 = (ri == ci).astype(F32)
    sr = lax.broadcasted_iota(jnp.int32, (LANES, LANES), 0) // HEAD_DIM
    sc = lax.broadcasted_iota(jnp.int32, (LANES, LANES), 1) // HEAD_DIM
    state_mask = sr == sc
    n_levels = max(int(math.log2(seg)) - 1, 0)
    merged = c2 % LANES == 0
    blocks = [(rb, p) for rb in range(n_blk) for p in range(N_PAIRS)]

    def tile(x, rb, p):
        return x[rb * cm:(rb + 1) * cm, p * LANES:(p + 1) * LANES]

    ops = {}
    for rb, p in blocks:
        at_p, rt_p, bt_p, kt_p, v_p = (tile(x, rb, p) for x in (at, rt, bt, kt, v))
        at2 = jnp.concatenate([at_p, at_p], axis=0)
        rt2 = jnp.concatenate([rt_p, rt_p], axis=0)
        v2 = jnp.concatenate([v_p, v_p], axis=0).astype(BF16)
        bt_m = jnp.concatenate([jnp.where(half0, bt_p, 0.0), jnp.where(half0, 0.0, bt_p)], axis=0).astype(BF16)
        kt_m = jnp.concatenate([jnp.where(half0, kt_p, 0.0), jnp.where(half0, 0.0, kt_p)], axis=0).astype(BF16)
        ops[rb, p] = dict(at2=at2, rt2=rt2, v2=v2, bt_m=bt_m, kt_m=kt_m, bt=bt_p, kt=kt_p, v=v_p)
    for blk in blocks:
        o = ops[blk]
        at2b = o["at2"].astype(BF16)
        if merged:
            g = _dot_nt(jnp.concatenate([at2b, o["rt2"].astype(BF16)], axis=0),
                        jnp.concatenate([o["bt_m"], o["kt_m"]], axis=0))
            g_ab, g_ak, g_rb, g_rk = g[:c2, :c2], g[:c2, c2:], g[c2:, :c2], g[c2:, c2:]
        else:
            g_ab, g_ak = _dot_nt(at2b, o["bt_m"]), _dot_nt(at2b, o["kt_m"])
            rt2b = o["rt2"].astype(BF16)
            g_rb, g_rk = _dot_nt(rt2b, o["bt_m"]), _dot_nt(rt2b, o["kt_m"])
        o["g_ab"] = jnp.where(bd_strict, g_ab, 0.0)
        o["g_kk"] = jnp.concatenate([jnp.where(bd_strict, g_ak, 0.0), jnp.where(bd_incl, g_rk, 0.0)], axis=0)
        o["g_rb"] = jnp.where(bd_incl, g_rb, 0.0)

    for blk in blocks:
        ops[blk]["t_inv"] = eye2 + ops[blk]["g_ab"]
    if n_levels > 0:
        for blk in blocks:
            ops[blk]["a_pow"] = _dot(ops[blk]["g_ab"], ops[blk]["g_ab"])
        for lvl in range(n_levels):
            last = lvl == n_levels - 1
            for blk in blocks:
                o = ops[blk]
                if last:
                    o["t_inv"] = o["t_inv"] + _dot(o["a_pow"], o["t_inv"])
                elif merged:
                    m = _dot(o["a_pow"], jnp.concatenate([o["a_pow"], o["t_inv"]], axis=1))
                    o["t_inv"] = o["t_inv"] + m[:, c2:]
                    o["a_pow"] = m[:, :c2]
                else:
                    o["t_inv"] = o["t_inv"] + _dot(o["a_pow"], o["t_inv"])
                    o["a_pow"] = _dot(o["a_pow"], o["a_pow"])

    for blk in blocks:
        o = ops[blk]
        gv = _dot(o["g_kk"], o["v2"])
        o["y0"] = gv[c2:]
        z = _dot(o["t_inv"], jnp.concatenate([o["at2"], gv[:c2]], axis=1))
        o["a_hat"], o["p0"] = z[:, :LANES], z[:, LANES:]

    n_seg = n_rows // seg
    n_states = n_seg // segs_per_state
    n_half = cm // seg
    p_parts = {blk: [None] * (2 * n_half) for blk in blocks}
    y_parts = {blk: [None] * (2 * n_half) for blk in blocks}
    for step in range(segs_per_state):
        segs = [st * segs_per_state + step for st in range(n_states)]
        proj = {}
        for g_i in segs:
            rb, off = (g_i * seg) // cm, (g_i * seg) % cm
            for p in range(N_PAIRS):
                o = ops[rb, p]
                lhs = jnp.concatenate(_seg_rows(o["a_hat"], off, seg, cm) + _seg_rows(o["rt2"], off, seg, cm), axis=0)
                proj[g_i, p] = _dot_nt(lhs, s_scr[(g_i // segs_per_state) * N_PAIRS + p])
        for g_i in segs:
            rb, off = (g_i * seg) // cm, (g_i * seg) % cm
            k_seg = off // seg
            for p in range(N_PAIRS):
                o = ops[rb, p]
                pr = proj[g_i, p]
                p0_top, p0_bot = _seg_rows(o["p0"], off, seg, cm)
                ps_top, ps_bot = pr[0:seg] + p0_top, pr[seg:2 * seg] + p0_bot
                p_parts[rb, p][k_seg], p_parts[rb, p][n_half + k_seg] = ps_top, ps_bot
                y_parts[rb, p][k_seg], y_parts[rb, p][n_half + k_seg] = pr[2 * seg:3 * seg], pr[3 * seg:]
                p_p = jnp.where(half0, ps_top, ps_bot)
                upd = _dot_tn(jnp.concatenate([p_p, o["v"][off:off + seg]], axis=0),
                              jnp.concatenate([o["bt"][off:off + seg], o["kt"][off:off + seg]], axis=0))
                si = (g_i // segs_per_state) * N_PAIRS + p
                row_end = g_i * seg + seg - 1
                w_end = e_cum[row_end:row_end + 1, p * LANES:(p + 1) * LANES]
                s_scr[si] = w_end * (s_scr[si] + jnp.where(state_mask, upd, 0.0))

    cat = lambda parts: parts[0] if len(parts) == 1 else jnp.concatenate(parts, axis=0)
    rows = []
    for rb in range(n_blk):
        tiles = []
        for p in range(N_PAIRS):
            o = ops[rb, p]
            y_s = cat(y_parts[rb, p]) + _dot(o["g_rb"], cat(p_parts[rb, p])) + o["y0"]
            tiles.append(jnp.where(half0, y_s[:cm], y_s[cm:]))
        rows.append(jnp.concatenate(tiles, axis=1))
    return cat(rows)


def _rwkv_kernel(bb, tb, cm, seg, has_state, carry_prev, f_ref, prev0_ref, mu_ref, w0_ref, a0_ref, lora_ref,
                 kk_ref, ka_ref, rk_ref, lnw_ref, lnb_ref, *rest):
    if has_state:
        s0_ref, o_ref, s_out_ref, s_scr, prev_scr = rest
    else:
        o_ref, s_out_ref, s_scr, prev_scr = rest
    j = pl.program_id(1)
    n_rows = bb * tb

    @pl.when(j == 0)
    def _():
        prev_scr[...] = prev0_ref[...]
        if has_state:
            s_scr[...] = s0_ref[...].reshape(s_scr.shape)
        else:
            s_scr[...] = jnp.zeros(s_scr.shape, F32)

    f = f_ref[...]
    row = lax.broadcasted_iota(jnp.int32, (n_rows, 1), 0)
    if bb == 1:
        prev_rows = prev_scr[0]
    else:
        prev_rows = jnp.broadcast_to(prev_scr[...], (bb, tb, D_SHIFT)).reshape(n_rows, D_SHIFT)
    f_prev = jnp.where(row % tb == 0, prev_rows, pltpu.roll(f, 1, 0))
    if carry_prev:
        prev_scr[0] = f[n_rows - 1:n_rows, :]
    fs = f + (f_prev - f) * mu_ref[...]
    r = fs[:, 0:D_RWKV]
    k = fs[:, D_RWKV:2 * D_RWKV]
    v = fs[:, 2 * D_RWKV:3 * D_RWKV]
    wa = fs[:, 3 * D_RWKV:D_SHIFT]
    lane = lax.broadcasted_iota(jnp.int32, (1, LANES), 1)
    lora = _dot(jnp.where(lane < D_LORA, jnp.tanh(wa), wa), lora_ref[...])
    lw = (-math.exp(-0.5)) * jax.nn.sigmoid(w0_ref[...] + lora[:, 0:D_RWKV])
    a_sig = jax.nn.sigmoid(a0_ref[...] + lora[:, D_RWKV:2 * D_RWKV])
    ones = _head_ones()
    kk = k * kk_ref[...]
    kk = kk * lax.rsqrt(jnp.maximum(_head_sum(kk * kk, ones), 1e-24))
    k2 = k * (1.0 + (a_sig - 1.0) * ka_ref[...])

    ti = lax.broadcasted_iota(jnp.int32, (n_rows, n_rows), 0)
    tj = lax.broadcasted_iota(jnp.int32, (n_rows, n_rows), 1)
    tri = ((ti // seg == tj // seg) & (tj <= ti)).astype(BF16)
    cum = _split3_dot_left(tri, lw)
    e_cum = jnp.exp(cum)
    e_inv = jnp.exp(-cum)
    y = _rwkv_recurrence(-kk * jnp.exp(cum - lw), r * e_cum, kk * a_sig * e_inv, k2 * e_inv, v, e_cum,
                         s_scr, cm, seg, tb // seg)

    inv_n = 1.0 / HEAD_DIM
    yc = y - _head_sum(y, ones) * inv_n
    var = _head_sum(yc * yc, ones) * inv_n
    yn = yc * lax.rsqrt(var + LNX_EPS) * lnw_ref[...] + lnb_ref[...]
    bonus = _head_sum(r * k2 * rk_ref[...], ones) * v
    o_ref[...] = yn + bonus
    s_out_ref[...] = s_scr[...].reshape(s_out_ref.shape)


def _rwkv(f, prev0, s0_bd, vecs, lora_w, batch, bb, tb, seg):
    m = f.shape[0]
    n_rows = bb * tb
    cm = min(n_rows, PROMPT_CHUNK)
    n_outer = batch // bb
    nblk = m // batch // tb
    has_state = s0_bd is not None
    vec_spec = lambda n: pl.BlockSpec((1, n), lambda b, j: (0, 0))
    mu, w0, a0, k_k, k_a, r_k, lnx_w, lnx_b = vecs
    in_specs = [pl.BlockSpec((n_rows, D_SHIFT), lambda b, j: (b * nblk + j, 0)),
                pl.BlockSpec((bb, 1, D_SHIFT), lambda b, j: (b, 0, 0)),
                vec_spec(D_SHIFT), vec_spec(D_RWKV), vec_spec(D_RWKV),
                pl.BlockSpec((LANES, 2 * D_RWKV), lambda b, j: (0, 0)),
                vec_spec(D_RWKV), vec_spec(D_RWKV), vec_spec(D_RWKV), vec_spec(D_RWKV), vec_spec(D_RWKV)]
    args = [f, prev0, mu, w0, a0, lora_w, k_k, k_a, r_k, lnx_w, lnx_b]
    state_spec = pl.BlockSpec((bb, N_PAIRS, LANES, LANES), lambda b, j: (b, 0, 0, 0))
    if has_state:
        in_specs.append(state_spec)
        args.append(s0_bd)
    return pl.pallas_call(
        functools.partial(_rwkv_kernel, bb, tb, cm, seg, has_state, nblk > 1),
        grid=(n_outer, nblk),
        in_specs=in_specs,
        out_specs=[pl.BlockSpec((n_rows, D_RWKV), lambda b, j: (b * nblk + j, 0)), state_spec],
        out_shape=[jax.ShapeDtypeStruct((m, D_RWKV), F32),
                   jax.ShapeDtypeStruct((batch, N_PAIRS, LANES, LANES), F32)],
        scratch_shapes=[pltpu.VMEM((bb * N_PAIRS, LANES, LANES), F32), pltpu.VMEM((bb, 1, D_SHIFT), F32)],
        compiler_params=pltpu.CompilerParams(dimension_semantics=("arbitrary", "arbitrary"),
                                             vmem_limit_bytes=VMEM_LIMIT),
        name="rwkv",
    )(*args)


def _qk_norm(q, k, qw, kw, ones):
    inv_n = 1.0 / HEAD_DIM
    qn = q * lax.rsqrt(_head_sum(q * q, ones) * inv_n + NORM_EPS) * (qw * (HEAD_DIM ** -0.5))
    kn = k * lax.rsqrt(_head_sum(k * k, ones) * inv_n + NORM_EPS) * kw
    return qn, kn


def _attend(qn, score_blocks, sinks_ref, half0):
    heads = range(N_Q_HEADS)
    aligned = [(h // Q_PER_KV) == (h % 2) for h in heads]
    qm = [jnp.where(half0 if h % 2 == 0 else jnp.logical_not(half0),
                    qn[:, (h // 2) * LANES:(h // 2 + 1) * LANES], 0.0).astype(BF16) for h in heads]
    sink = [sinks_ref[h:h + 1, 0:1] for h in heads]
    scores = [[jnp.where(valid, _dot_nt(qm[h], keys if aligned[h] else keys_rot), NEG_INF)
               for keys, keys_rot, _, _, valid in score_blocks] for h in heads]
    m = []
    for h in heads:
        mh = sink[h]
        for s in scores[h]:
            mh = jnp.maximum(mh, jnp.max(s, axis=-1, keepdims=True))
        m.append(mh)
    probs = [[jnp.exp(s - m[h]) for s in scores[h]] for h in heads]
    denom = []
    for h in heads:
        d = jnp.exp(sink[h] - m[h])
        for pr in probs[h]:
            d = d + jnp.sum(pr, axis=-1, keepdims=True)
        denom.append(d)
    outs = []
    for h in heads:
        acc = None
        for pr, (_, _, vals, vals_rot, _) in zip(probs[h], score_blocks):
            pv = _dot(pr, vals if aligned[h] else vals_rot)
            acc = pv if acc is None else acc + pv
        outs.append(acc * (1.0 / denom[h]))
    return jnp.concatenate([jnp.where(half0, outs[2 * t], outs[2 * t + 1]) for t in range(D_ATTN // LANES)], axis=1)


def _attn_prompt_kernel(q_ref, k_ref, v_ref, qw_ref, kw_ref, sinks_ref, o_ref, kc_ref, vc_ref, kprev, vprev):
    j = pl.program_id(1)

    @pl.when(j == 0)
    def _():
        kprev[...] = jnp.zeros(kprev.shape, F32)
        vprev[...] = jnp.zeros(vprev.shape, F32)

    ones = _head_ones()
    half0 = lax.broadcasted_iota(jnp.int32, (1, LANES), 1) < HEAD_DIM
    v = v_ref[...]
    qn, kn = _qk_norm(q_ref[...], k_ref[...], qw_ref[...], kw_ref[...], ones)
    keys = jnp.concatenate([kprev[...], kn], axis=0)
    vals = jnp.concatenate([vprev[...], v], axis=0)
    qi = lax.broadcasted_iota(jnp.int32, (WINDOW, 2 * WINDOW), 0)
    kj = lax.broadcasted_iota(jnp.int32, (WINDOW, 2 * WINDOW), 1)
    valid = (kj > qi) & (kj <= qi + WINDOW) & ((kj >= WINDOW) | (j > 0))
    blocks = [(keys.astype(BF16), pltpu.roll(keys, HEAD_DIM, 1).astype(BF16),
               vals.astype(BF16), pltpu.roll(vals, HEAD_DIM, 1).astype(BF16), valid)]
    o_ref[...] = _attend(qn, blocks, sinks_ref, half0)
    kprev[...] = kn
    vprev[...] = v
    kc_ref[0] = kn
    vc_ref[0] = v


def _attn_prompt(q, k, v, qw, kw, sinks_b, batch):
    m = q.shape[0]
    nblk = m // batch // WINDOW
    tok = lambda w: pl.BlockSpec((WINDOW, w), lambda b, j: (b * nblk + j, 0))
    const = lambda r, w: pl.BlockSpec((r, w), lambda b, j: (0, 0))
    cache = pl.BlockSpec((1, WINDOW, D_KV), lambda b, j: (b, 0, 0))
    return pl.pallas_call(
        _attn_prompt_kernel,
        grid=(batch, nblk),
        in_specs=[tok(D_ATTN), tok(D_KV), tok(D_KV), const(1, D_ATTN), const(1, D_KV), const(N_Q_HEADS, LANES)],
        out_specs=[tok(D_ATTN), cache, cache],
        out_shape=[jax.ShapeDtypeStruct((m, D_ATTN), F32),
                   jax.ShapeDtypeStruct((batch, WINDOW, D_KV), F32),
                   jax.ShapeDtypeStruct((batch, WINDOW, D_KV), F32)],
        scratch_shapes=[pltpu.VMEM((WINDOW, D_KV), F32), pltpu.VMEM((WINDOW, D_KV), F32)],
        compiler_params=pltpu.CompilerParams(dimension_semantics=("arbitrary", "arbitrary")),
        name="attn_prompt",
    )(q, k, v, qw, kw, sinks_b)


def _attn_sample_kernel(t_new, q_ref, k_ref, v_ref, ck_ref, cv_ref, qw_ref, kw_ref, sinks_ref,
                        o_ref, ko_ref, vo_ref):
    ones = _head_ones()
    half0 = lax.broadcasted_iota(jnp.int32, (1, LANES), 1) < HEAD_DIM
    wb = ck_ref.shape[1]
    qi_c = lax.broadcasted_iota(jnp.int32, (t_new, wb), 0)
    kj_c = lax.broadcasted_iota(jnp.int32, (t_new, wb), 1)
    valid_c = (qi_c + wb - kj_c) < WINDOW
    qi_n = lax.broadcasted_iota(jnp.int32, (t_new, t_new), 0)
    kj_n = lax.broadcasted_iota(jnp.int32, (t_new, t_new), 1)
    valid_n = kj_n <= qi_n
    rot = lambda x: pltpu.roll(x, HEAD_DIM, 1).astype(BF16)
    for i in range(q_ref.shape[0]):
        v = v_ref[i]
        qn, kn = _qk_norm(q_ref[i], k_ref[i], qw_ref[...], kw_ref[...], ones)
        ck = ck_ref[i]
        cv = cv_ref[i]
        blocks = [(ck.astype(BF16), rot(ck), cv.astype(BF16), rot(cv), valid_c),
                  (kn.astype(BF16), rot(kn), v.astype(BF16), rot(v), valid_n)]
        o_ref[i] = _attend(qn, blocks, sinks_ref, half0)
        ko_ref[i] = jnp.concatenate([ck[t_new:], kn], axis=0)
        vo_ref[i] = jnp.concatenate([cv[t_new:], v], axis=0)


def _attn_sample(q, k, v, ck, cv, qw, kw, sinks_b):
    b, t_new, _ = q.shape
    wb = ck.shape[1]
    gb = SAMPLE_GROUP
    spec = lambda r, w: pl.BlockSpec((gb, r, w), lambda i: (i, 0, 0))
    const = lambda r, w: pl.BlockSpec((r, w), lambda i: (0, 0))
    return pl.pallas_call(
        functools.partial(_attn_sample_kernel, t_new),
        grid=(b // gb,),
        in_specs=[spec(t_new, D_ATTN), spec(t_new, D_KV), spec(t_new, D_KV), spec(wb, D_KV), spec(wb, D_KV),
                  const(1, D_ATTN), const(1, D_KV), const(N_Q_HEADS, LANES)],
        out_specs=[spec(t_new, D_ATTN), spec(wb, D_KV), spec(wb, D_KV)],
        out_shape=[jax.ShapeDtypeStruct((b, t_new, D_ATTN), F32),
                   jax.ShapeDtypeStruct((b, wb, D_KV), F32),
                   jax.ShapeDtypeStruct((b, wb, D_KV), F32)],
        compiler_params=pltpu.CompilerParams(dimension_semantics=("arbitrary",)),
        name="attn_sample",
    )(q, k, v, ck, cv, qw, kw, sinks_b)


def _merge_kernel(x_ref, or_ref, zr_ref, oa_ref, za_ref, p_ref, wout_ref, g_ref, wgate_ref, wproj_ref, y_ref):
    gr = or_ref[...] * jax.nn.silu(zr_ref[...])
    ga = oa_ref[...] * jax.nn.silu(za_ref[...])
    h = x_ref[...] + _dot(gr, wout_ref[0:D_RWKV, :]) + _dot(ga, wout_ref[D_RWKV:D_MODEL, :])
    gate = jax.nn.sigmoid(_dot(_rms_rows(h, g_ref[...]), wgate_ref[...]))
    y_ref[...] = h + gate * _dot(p_ref[...], wproj_ref[...])


def _merge(x, o_r, z_r, o_a, z_a, p, w_out, g_ple, w_gate, w_proj, tm):
    m = x.shape[0]
    tok = lambda w: pl.BlockSpec((tm, w), lambda i: (i, 0))
    const = lambda r, w: pl.BlockSpec((r, w), lambda i: (0, 0))
    return pl.pallas_call(
        _merge_kernel,
        grid=(m // tm,),
        in_specs=[tok(D_MODEL), tok(D_RWKV), tok(D_RWKV), tok(D_ATTN), tok(D_ATTN), tok(D_PLE),
                  const(D_MODEL, D_MODEL), const(1, D_MODEL), const(D_MODEL, D_MODEL), const(D_PLE, D_MODEL)],
        out_specs=tok(D_MODEL),
        out_shape=jax.ShapeDtypeStruct((m, D_MODEL), F32),
        compiler_params=pltpu.CompilerParams(dimension_semantics=("arbitrary",), vmem_limit_bytes=VMEM_LIMIT),
        name="merge",
    )(x, o_r, z_r, o_a, z_a, p, w_out, g_ple, w_gate, w_proj)


def _state_to_pairs(s):
    b = s.shape[0]
    s = s.reshape(b, N_PAIRS, 2, HEAD_DIM, HEAD_DIM)
    z = jnp.zeros_like(s[:, :, 0])
    top = jnp.concatenate([s[:, :, 0], z], axis=-1)
    bot = jnp.concatenate([z, s[:, :, 1]], axis=-1)
    return jnp.concatenate([top, bot], axis=-2)


def _pairs_to_state(s_bd):
    b = s_bd.shape[0]
    s0 = s_bd[:, :, :HEAD_DIM, :HEAD_DIM]
    s1 = s_bd[:, :, HEAD_DIM:, HEAD_DIM:]
    return jnp.stack([s0, s1], axis=2).reshape(b, 2 * N_PAIRS, HEAD_DIM, HEAD_DIM)


def kernel(x_prompt, x_sample, state_rwkv, state_shift, cache_k, cache_v, p_prompt, p_sample, g_norm, w_in, mu_shift, w0, w_dec2, a0, w_a2, k_k, k_a, r_k, lnx_w, lnx_b, q_norm_w, k_norm_w, sinks, w_out, g_ple, w_ple_gate, w_ple_proj):
    depth = w_in.shape[0]
    bp, seq, _ = x_prompt.shape
    bs, dec, _ = x_sample.shape
    wb = cache_k.shape[2]
    xp = x_prompt.reshape(bp * seq, D_MODEL)
    xs = x_sample.reshape(bs * dec, D_MODEL)
    outs = [[] for _ in range(8)]
    for i in range(depth):
        w_in_b = w_in[i].astype(BF16)
        w_out_b = w_out[i].astype(BF16)
        w_gate_b = w_ple_gate[i].astype(BF16)
        w_proj_b = w_ple_proj[i].astype(BF16)
        zl = jnp.zeros((D_LORA, D_RWKV), F32)
        lora_w = jnp.concatenate([jnp.concatenate([w_dec2[i], zl], axis=1),
                                  jnp.concatenate([zl, w_a2[i]], axis=1)], axis=0).astype(BF16)
        row = lambda t: t.reshape(1, -1)
        vecs = (row(mu_shift[i]), row(w0[i]), row(a0[i]), row(k_k[i]), row(k_a[i]), row(r_k[i]),
                row(lnx_w[i]), row(lnx_b[i]))
        qw = row(jnp.tile(q_norm_w[i], N_Q_HEADS))
        kw = row(jnp.tile(k_norm_w[i], N_KV_HEADS))
        sinks_b = jnp.broadcast_to(sinks[i][:, None], (N_Q_HEADS, LANES))
        gn, gp = row(g_norm[i]), row(g_ple[i])

        f, z_r, q, k, v, z_a = _in_proj(xp, gn, w_in_b, 512)
        o_r, s_bd = _rwkv(f, jnp.zeros((bp, 1, D_SHIFT), F32), None, vecs, lora_w, bp, 1, WINDOW, PROMPT_CHUNK)
        o_a, kc, vc = _attn_prompt(q, k, v, qw, kw, sinks_b, bp)
        outs[0].append(_pairs_to_state(s_bd))
        outs[2].append(f.reshape(bp, seq, D_SHIFT)[:, -1:])
        outs[4].append(kc.reshape(bp, WINDOW, N_KV_HEADS, HEAD_DIM))
        outs[6].append(vc.reshape(bp, WINDOW, N_KV_HEADS, HEAD_DIM))
        xp = _merge(xp, o_r, z_r, o_a, z_a, p_prompt[i].reshape(bp * seq, D_PLE), w_out_b, gp, w_gate_b, w_proj_b, 512)

        f, z_r, q, k, v, z_a = _in_proj(xs, gn, w_in_b, 512)
        o_r, s_bd = _rwkv(f, state_shift[i], _state_to_pairs(state_rwkv[i]), vecs, lora_w, bs, SAMPLE_GROUP, dec, dec)
        o_a, k_buf, v_buf = _attn_sample(q.reshape(bs, dec, D_ATTN), k.reshape(bs, dec, D_KV), v.reshape(bs, dec, D_KV),
                                         cache_k[i].reshape(bs, wb, D_KV), cache_v[i].reshape(bs, wb, D_KV),
                                         qw, kw, sinks_b)
        outs[1].append(_pairs_to_state(s_bd))
        outs[3].append(f.reshape(bs, dec, D_SHIFT)[:, -1:])
        outs[5].append(k_buf.reshape(bs, wb, N_KV_HEADS, HEAD_DIM))
        outs[7].append(v_buf.reshape(bs, wb, N_KV_HEADS, HEAD_DIM))
        xs = _merge(xs, o_r, z_r, o_a.reshape(bs * dec, D_ATTN), z_a, p_sample[i].reshape(bs * dec, D_PLE),
                    w_out_b, gp, w_gate_b, w_proj_b, 512)
    st = lambda l: jnp.stack(l)
    return (xp.reshape(bp, seq, D_MODEL), xs.reshape(bs, dec, D_MODEL),
            st(outs[0]), st(outs[1]), st(outs[2]), st(outs[3]), st(outs[4]), st(outs[5]), st(outs[6]), st(outs[7]))
```

```python
import functools
import math

import jax
import jax.numpy as jnp
from jax import lax
from jax.experimental import pallas as pl
from jax.experimental.pallas import tpu as pltpu

F32 = jnp.float32
BF16 = jnp.bfloat16

D_MODEL = 1024
HEAD_DIM = 64
D_RWKV = 512
D_ATTN = 512
N_KV_HEADS = 2
N_Q_HEADS = 8
Q_PER_KV = N_Q_HEADS // N_KV_HEADS
D_KV = N_KV_HEADS * HEAD_DIM
WINDOW = 128
D_LORA = 64
D_SHIFT = 3 * D_RWKV + 2 * D_LORA
D_PLE = 256
D_IN = D_SHIFT + D_RWKV + D_ATTN + 2 * D_KV + D_ATTN
NORM_EPS = 1e-6
LNX_EPS = 64e-5
NEG_INF = -1e30

LANES = 128
N_PAIRS = D_RWKV // LANES
PROMPT_CHUNK = 64
SAMPLE_GROUP = 8
VMEM_LIMIT = 56 * 1024 * 1024


def _dot(a, b):
    return jnp.dot(a.astype(BF16), b.astype(BF16), preferred_element_type=F32)


def _dot_nt(a, b):
    return lax.dot_general(a.astype(BF16), b.astype(BF16), (((1,), (1,)), ((), ())), preferred_element_type=F32)


def _dot_tn(a, b):
    return lax.dot_general(a.astype(BF16), b.astype(BF16), (((0,), (0,)), ((), ())), preferred_element_type=F32)


def _split_dot(x, w):
    hi = x.astype(BF16)
    lo = (x - hi.astype(F32)).astype(BF16)
    return jnp.dot(hi, w, preferred_element_type=F32) + jnp.dot(lo, w, preferred_element_type=F32)


def _split3_dot_left(w, x):
    hi = x.astype(BF16)
    r1 = x - hi.astype(F32)
    mid = r1.astype(BF16)
    lo = (r1 - mid.astype(F32)).astype(BF16)
    d = lambda t: jnp.dot(w, t, preferred_element_type=F32)
    return d(hi) + d(mid) + d(lo)


def _head_ones():
    r = lax.broadcasted_iota(jnp.int32, (LANES, LANES), 0) // HEAD_DIM
    c = lax.broadcasted_iota(jnp.int32, (LANES, LANES), 1) // HEAD_DIM
    return (r == c).astype(BF16)


def _head_sum(x, ones):
    tiles = [_split_dot(x[:, i:i + LANES], ones) for i in range(0, x.shape[1], LANES)]
    return tiles[0] if len(tiles) == 1 else jnp.concatenate(tiles, axis=1)


def _rms_rows(x, g):
    return x * lax.rsqrt(jnp.mean(x * x, axis=-1, keepdims=True) + NORM_EPS) * g


_IN_SPLITS = (D_SHIFT, D_RWKV, D_ATTN, D_KV, D_KV, D_ATTN)


def _in_proj_kernel(x_ref, g_ref, w_ref, *out_refs):
    h = _dot(_rms_rows(x_ref[...], g_ref[...]), w_ref[...])
    off = 0
    for o_ref, width in zip(out_refs, _IN_SPLITS):
        o_ref[...] = h[:, off:off + width]
        off += width


def _in_proj(x, g_norm, w_in_bf16, tm):
    m = x.shape[0]
    return pl.pallas_call(
        _in_proj_kernel,
        grid=(m // tm,),
        in_specs=[pl.BlockSpec((tm, D_MODEL), lambda i: (i, 0)),
                  pl.BlockSpec((1, D_MODEL), lambda i: (0, 0)),
                  pl.BlockSpec((D_MODEL, D_IN), lambda i: (0, 0))],
        out_specs=[pl.BlockSpec((tm, w), lambda i: (i, 0)) for w in _IN_SPLITS],
        out_shape=[jax.ShapeDtypeStruct((m, w), F32) for w in _IN_SPLITS],
        compiler_params=pltpu.CompilerParams(dimension_semantics=("arbitrary",), vmem_limit_bytes=VMEM_LIMIT),
        name="in_proj",
    )(x, g_norm, w_in_bf16)


def _seg_rows(x, off, seg, cm):
    return x[off:off + seg], x[cm + off:cm + off + seg]


def _rwkv_recurrence(at, rt, bt, kt, v, e_cum, s_scr, cm, seg, segs_per_state):
    n_rows = at.shape[0]
    n_blk = n_rows // cm
    c2 = 2 * cm
    lane = lax.broadcasted_iota(jnp.int32, (1, LANES), 1)
    half0 = lane < HEAD_DIM
    ri = lax.broadcasted_iota(jnp.int32, (c2, c2), 0)
    ci = lax.broadcasted_iota(jnp.int32, (c2, c2), 1)
    same = (ri // seg) == (ci // seg)
    bd_strict = same & (ci < ri)
    bd_incl = same & (ci <= ri)
    eye2 = (ri == ci).astype(F32)
    sr = lax.broadcasted_iota(jnp.int32, (LANES, LANES), 0) // HEAD_DIM
    sc = lax.broadcasted_iota(jnp.int32, (LANES, LANES), 1) // HEAD_DIM
    state_mask = sr == sc
    n_levels = max(int(math.log2(seg)) - 1, 0)
    merged = c2 % LANES == 0
    blocks = [(rb, p) for rb in range(n_blk) for p in range(N_PAIRS)]

    def tile(x, rb, p):
        return x[rb * cm:(rb + 1) * cm, p * LANES:(p + 1) * LANES]

    ops = {}
    for rb, p in blocks:
        at_p, rt_p, bt_p, kt_p, v_p = (tile(x, rb, p) for x in (at, rt, bt, kt, v))
        at2 = jnp.concatenate([at_p, at_p], axis=0)
        rt2 = jnp.concatenate([rt_p, rt_p], axis=0)
        v2 = jnp.concatenate([v_p, v_p], axis=0).astype(BF16)
        bt_m = jnp.concatenate([jnp.where(half0, bt_p, 0.0), jnp.where(half0, 0.0, bt_p)], axis=0).astype(BF16)
        kt_m = jnp.concatenate([jnp.where(half0, kt_p, 0.0), jnp.where(half0, 0.0, kt_p)], axis=0).astype(BF16)
        ops[rb, p] = dict(at2=at2, rt2=rt2, v2=v2, bt_m=bt_m, kt_m=kt_m, bt=bt_p, kt=kt_p, v=v_p)
    for blk in blocks:
        o = ops[blk]
        at2b = o["at2"].astype(BF16)
        if merged:
            g = _dot_nt(jnp.concatenate([at2b, o["rt2"].astype(BF16)], axis=0),
                        jnp.concatenate([o["bt_m"], o["kt_m"]], axis=0))
            g_ab, g_ak, g_rb, g_rk = g[:c2, :c2], g[:c2, c2:], g[c2:, :c2], g[c2:, c2:]
        else:
            g_ab, g_ak = _dot_nt(at2b, o["bt_m"]), _dot_nt(at2b, o["kt_m"])
            rt2b = o["rt2"].astype(BF16)
            g_rb, g_rk = _dot_nt(rt2b, o["bt_m"]), _dot_nt(rt2b, o["kt_m"])
        o["g_ab"] = jnp.where(bd_strict, g_ab, 0.0)
        o["g_kk"] = jnp.concatenate([jnp.where(bd_strict, g_ak, 0.0), jnp.where(bd_incl, g_rk, 0.0)], axis=0)
        o["g_rb"] = jnp.where(bd_incl, g_rb, 0.0)

    for blk in blocks:
        ops[blk]["t_inv"] = eye2 + ops[blk]["g_ab"]
    if n_levels > 0:
        for blk in blocks:
            ops[blk]["a_pow"] = _dot(ops[blk]["g_ab"], ops[blk]["g_ab"])
        for lvl in range(n_levels):
            last = lvl == n_levels - 1
            for blk in blocks:
                o = ops[blk]
                if last:
                    o["t_inv"] = o["t_inv"] + _dot(o["a_pow"], o["t_inv"])
                elif merged:
                    m = _dot(o["a_pow"], jnp.concatenate([o["a_pow"], o["t_inv"]], axis=1))
                    o["t_inv"] = o["t_inv"] + m[:, c2:]
                    o["a_pow"] = m[:, :c2]
                else:
                    o["t_inv"] = o["t_inv"] + _dot(o["a_pow"], o["t_inv"])
                    o["a_pow"] = _dot(o["a_pow"], o["a_pow"])

    for blk in blocks:
        o = ops[blk]
        gv = _dot(o["g_kk"], o["v2"])
        o["y0"] = gv[c2:]
        z = _dot(o["t_inv"], jnp.concatenate([o["at2"], gv[:c2]], axis=1))
        o["a_hat"], o["p0"] = z[:, :LANES], z[:, LANES:]

    n_seg = n_rows // seg
    n_states = n_seg // segs_per_state
    n_half = cm // seg
    p_parts = {blk: [None] * (2 * n_half) for blk in blocks}
    y_parts = {blk: [None] * (2 * n_half) for blk in blocks}
    for step in range(segs_per_state):
        segs = [st * segs_per_state + step for st in range(n_states)]
        proj = {}
        for g_i in segs:
            rb, off = (g_i * seg) // cm, (g_i * seg) % cm
            for p in range(N_PAIRS):
                o = ops[rb, p]
                lhs = jnp.concatenate(_seg_rows(o["a_hat"], off, seg, cm) + _seg_rows(o["rt2"], off, seg, cm), axis=0)
                proj[g_i, p] = _dot_nt(lhs, s_scr[(g_i // segs_per_state) * N_PAIRS + p])
        for g_i in segs:
            rb, off = (g_i * seg) // cm, (g_i * seg) % cm
            k_seg = off // seg
            for p in range(N_PAIRS):
                o = ops[rb, p]
                pr = proj[g_i, p]
                p0_top, p0_bot = _seg_rows(o["p0"], off, seg, cm)
                ps_top, ps_bot = pr[0:seg] + p0_top, pr[seg:2 * seg] + p0_bot
                p_parts[rb, p][k_seg], p_parts[rb, p][n_half + k_seg] = ps_top, ps_bot
                y_parts[rb, p][k_seg], y_parts[rb, p][n_half + k_seg] = pr[2 * seg:3 * seg], pr[3 * seg:]
                p_p = jnp.where(half0, ps_top, ps_bot)
                upd = _dot_tn(jnp.concatenate([p_p, o["v"][off:off + seg]], axis=0),
                              jnp.concatenate([o["bt"][off:off + seg], o["kt"][off:off + seg]], axis=0))
                si = (g_i // segs_per_state) * N_PAIRS + p
                row_end = g_i * seg + seg - 1
                w_end = e_cum[row_end:row_end + 1, p * LANES:(p + 1) * LANES]
                s_scr[si] = w_end * (s_scr[si] + jnp.where(state_mask, upd, 0.0))

    cat = lambda parts: parts[0] if len(parts) == 1 else jnp.concatenate(parts, axis=0)
    rows = []
    for rb in range(n_blk):
        tiles = []
        for p in range(N_PAIRS):
            o = ops[rb, p]
            y_s = cat(y_parts[rb, p]) + _dot(o["g_rb"], cat(p_parts[rb, p])) + o["y0"]
            tiles.append(jnp.where(half0, y_s[:cm], y_s[cm:]))
        rows.append(jnp.concatenate(tiles, axis=1))
    return cat(rows)


def _rwkv_kernel(bb, tb, cm, seg, has_state, carry_prev, f_ref, prev0_ref, mu_ref, w0_ref, a0_ref, lora_ref,
                 kk_ref, ka_ref, rk_ref, lnw_ref, lnb_ref, *rest):
    if has_state:
        s0_ref, o_ref, s_out_ref, s_scr, prev_scr = rest
    else:
        o_ref, s_out_ref, s_scr, prev_scr = rest
    j = pl.program_id(1)
    n_rows = bb * tb

    @pl.when(j == 0)
    def _():
        prev_scr[...] = prev0_ref[...]
        if has_state:
            s_scr[...] = s0_ref[...].reshape(s_scr.shape)
        else:
            s_scr[...] = jnp.zeros(s_scr.shape, F32)

    f = f_ref[...]
    row = lax.broadcasted_iota(jnp.int32, (n_rows, 1), 0)
    if bb == 1:
        prev_rows = prev_scr[0]
    else:
        prev_rows = jnp.broadcast_to(prev_scr[...], (bb, tb, D_SHIFT)).reshape(n_rows, D_SHIFT)
    f_prev = jnp.where(row % tb == 0, prev_rows, pltpu.roll(f, 1, 0))
    if carry_prev:
        prev_scr[0] = f[n_rows - 1:n_rows, :]
    fs = f + (f_prev - f) * mu_ref[...]
    r = fs[:, 0:D_RWKV]
    k = fs[:, D_RWKV:2 * D_RWKV]
    v = fs[:, 2 * D_RWKV:3 * D_RWKV]
    wa = fs[:, 3 * D_RWKV:D_SHIFT]
    lane = lax.broadcasted_iota(jnp.int32, (1, LANES), 1)
    lora = _dot(jnp.where(lane < D_LORA, jnp.tanh(wa), wa), lora_ref[...])
    lw = (-math.exp(-0.5)) * jax.nn.sigmoid(w0_ref[...] + lora[:, 0:D_RWKV])
    a_sig = jax.nn.sigmoid(a0_ref[...] + lora[:, D_RWKV:2 * D_RWKV])
    ones = _head_ones()
    kk = k * kk_ref[...]
    kk = kk * lax.rsqrt(jnp.maximum(_head_sum(kk * kk, ones), 1e-24))
    k2 = k * (1.0 + (a_sig - 1.0) * ka_ref[...])

    ti = lax.broadcasted_iota(jnp.int32, (n_rows, n_rows), 0)
    tj = lax.broadcasted_iota(jnp.int32, (n_rows, n_rows), 1)
    tri = ((ti // seg == tj // seg) & (tj <= ti)).astype(BF16)
    cum = _split3_dot_left(tri, lw)
    e_cum = jnp.exp(cum)
    e_inv = jnp.exp(-cum)
    y = _rwkv_recurrence(-kk * jnp.exp(cum - lw), r * e_cum, kk * a_sig * e_inv, k2 * e_inv, v, e_cum,
                         s_scr, cm, seg, tb // seg)

    inv_n = 1.0 / HEAD_DIM
    yc = y - _head_sum(y, ones) * inv_n
    var = _head_sum(yc * yc, ones) * inv_n
    yn = yc * lax.rsqrt(var + LNX_EPS) * lnw_ref[...] + lnb_ref[...]
    bonus = _head_sum(r * k2 * rk_ref[...], ones) * v
    o_ref[...] = yn + bonus
    s_out_ref[...] = s_scr[...].reshape(s_out_ref.shape)


def _rwkv(f, prev0, s0_bd, vecs, lora_w, batch, bb, tb, seg):
    m = f.shape[0]
    n_rows = bb * tb
    cm = min(n_rows, PROMPT_CHUNK)
    n_outer = batch // bb
    nblk = m // batch // tb
    has_state = s0_bd is not None
    vec_spec = lambda n: pl.BlockSpec((1, n), lambda b, j: (0, 0))
    mu, w0, a0, k_k, k_a, r_k, lnx_w, lnx_b = vecs
    in_specs = [pl.BlockSpec((n_rows, D_SHIFT), lambda b, j: (b * nblk + j, 0)),
                pl.BlockSpec((bb, 1, D_SHIFT), lambda b, j: (b, 0, 0)),
                vec_spec(D_SHIFT), vec_spec(D_RWKV), vec_spec(D_RWKV),
                pl.BlockSpec((LANES, 2 * D_RWKV), lambda b, j: (0, 0)),
                vec_spec(D_RWKV), vec_spec(D_RWKV), vec_spec(D_RWKV), vec_spec(D_RWKV), vec_spec(D_RWKV)]
    args = [f, prev0, mu, w0, a0, lora_w, k_k, k_a, r_k, lnx_w, lnx_b]
    state_spec = pl.BlockSpec((bb, N_PAIRS, LANES, LANES), lambda b, j: (b, 0, 0, 0))
    if has_state:
        in_specs.append(state_spec)
        args.append(s0_bd)
    return pl.pallas_call(
        functools.partial(_rwkv_kernel, bb, tb, cm, seg, has_state, nblk > 1),
        grid=(n_outer, nblk),
        in_specs=in_specs,
        out_specs=[pl.BlockSpec((n_rows, D_RWKV), lambda b, j: (b * nblk + j, 0)), state_spec],
        out_shape=[jax.ShapeDtypeStruct((m, D_RWKV), F32),
                   jax.ShapeDtypeStruct((batch, N_PAIRS, LANES, LANES), F32)],
        scratch_shapes=[pltpu.VMEM((bb * N_PAIRS, LANES, LANES), F32), pltpu.VMEM((bb, 1, D_SHIFT), F32)],
        compiler_params=pltpu.CompilerParams(dimension_semantics=("arbitrary", "arbitrary"),
                                             vmem_limit_bytes=VMEM_LIMIT),
        name="rwkv",
    )(*args)


def _qk_norm(q, k, qw, kw, ones):
    inv_n = 1.0 / HEAD_DIM
    qn = q * lax.rsqrt(_head_sum(q * q, ones) * inv_n + NORM_EPS) * (qw * (HEAD_DIM ** -0.5))
    kn = k * lax.rsqrt(_head_sum(k * k, ones) * inv_n + NORM_EPS) * kw
    return qn, kn


def _attend(qn, score_blocks, sinks_ref, half0):
    heads = range(N_Q_HEADS)
    aligned = [(h // Q_PER_KV) == (h % 2) for h in heads]
    qm = [jnp.where(half0 if h % 2 == 0 else jnp.logical_not(half0),
                    qn[:, (h // 2) * LANES:(h // 2 + 1) * LANES], 0.0).astype(BF16) for h in heads]
    sink = [sinks_ref[h:h + 1, 0:1] for h in heads]
    scores = [[jnp.where(valid, _dot_nt(qm[h], keys if aligned[h] else keys_rot), NEG_INF)
               for keys, keys_rot, _, _, valid in score_blocks] for h in heads]
    m = []
    for h in heads:
        mh = sink[h]
        for s in scores[h]:
            mh = jnp.maximum(mh, jnp.max(s, axis=-1, keepdims=True))
        m.append(mh)
    probs = [[jnp.exp(s - m[h]) for s in scores[h]] for h in heads]
    denom = []
    for h in heads:
        d = jnp.exp(sink[h] - m[h])
        for pr in probs[h]:
            d = d + jnp.sum(pr, axis=-1, keepdims=True)
        denom.append(d)
    outs = []
    for h in heads:
        acc = None
        for pr, (_, _, vals, vals_rot, _) in zip(probs[h], score_blocks):
            pv = _dot(pr, vals if aligned[h] else vals_rot)
            acc = pv if acc is None else acc + pv
        outs.append(acc * (1.0 / denom[h]))
    return jnp.concatenate([jnp.where(half0, outs[2 * t], outs[2 * t + 1]) for t in range(D_ATTN // LANES)], axis=1)


def _attend_keys_major(qn, keys, keys_rot, vals_t, valid_t, sinks_ref, half0):
    heads = range(N_Q_HEADS)
    scores = []
    for h in heads:
        e, g = h % 2, h // Q_PER_KV
        qm = jnp.where(half0 if e == 0 else jnp.logical_not(half0), qn[:, (h // 2) * LANES:(h // 2 + 1) * LANES], 0.0)
        scores.append(jnp.where(valid_t, _dot_nt(keys if g == e else keys_rot, qm.astype(BF16)), NEG_INF))
    probs = []
    for h in heads:
        sink = sinks_ref[h:h + 1, 0:1]
        m = jnp.maximum(jnp.max(scores[h], axis=0, keepdims=True), sink)
        pr = jnp.exp(scores[h] - m)
        denom = jnp.sum(pr, axis=0, keepdims=True) + jnp.exp(sink - m)
        probs.append((pr * (1.0 / denom)).astype(BF16))
    outs = [jnp.dot(vals_t[(h // Q_PER_KV) * HEAD_DIM:(h // Q_PER_KV + 1) * HEAD_DIM, :], probs[h],
                    preferred_element_type=F32) for h in heads]
    return jnp.concatenate([jnp.concatenate(outs[2 * t:2 * t + 2], axis=0).T for t in range(D_ATTN // LANES)], axis=1)


def _attn_prompt_kernel(q_ref, k_ref, v_ref, qw_ref, kw_ref, sinks_ref, o_ref, kc_ref, vc_ref,
                        kprev, kprev_rot, vprev_t):
    j = pl.program_id(1)

    @pl.when(j == 0)
    def _():
        kprev[...] = jnp.zeros(kprev.shape, BF16)
        kprev_rot[...] = jnp.zeros(kprev_rot.shape, BF16)
        vprev_t[...] = jnp.zeros(vprev_t.shape, BF16)

    ones = _head_ones()
    half0 = lax.broadcasted_iota(jnp.int32, (1, LANES), 1) < HEAD_DIM
    v = v_ref[...]
    qn, kn = _qk_norm(q_ref[...], k_ref[...], qw_ref[...], kw_ref[...], ones)
    kn_b = kn.astype(BF16)
    kn_rot = pltpu.roll(kn, HEAD_DIM, 1).astype(BF16)
    v_t = v.T.astype(BF16)
    keys = jnp.concatenate([kprev[...], kn_b], axis=0)
    keys_rot = jnp.concatenate([kprev_rot[...], kn_rot], axis=0)
    vals_t = jnp.concatenate([vprev_t[...], v_t], axis=1)
    kj = lax.broadcasted_iota(jnp.int32, (2 * WINDOW, WINDOW), 0)
    qi = lax.broadcasted_iota(jnp.int32, (2 * WINDOW, WINDOW), 1)
    valid_t = (kj > qi) & (kj <= qi + WINDOW) & ((kj >= WINDOW) | (j > 0))
    o_ref[...] = _attend_keys_major(qn, keys, keys_rot, vals_t, valid_t, sinks_ref, half0)
    kprev[...] = kn_b
    kprev_rot[...] = kn_rot
    vprev_t[...] = v_t
    kc_ref[0] = kn
    vc_ref[0] = v


def _attn_prompt(q, k, v, qw, kw, sinks_b, batch):
    m = q.shape[0]
    nblk = m // batch // WINDOW
    tok = lambda w: pl.BlockSpec((WINDOW, w), lambda b, j: (b * nblk + j, 0))
    const = lambda r, w: pl.BlockSpec((r, w), lambda b, j: (0, 0))
    cache = pl.BlockSpec((1, WINDOW, D_KV), lambda b, j: (b, 0, 0))
    return pl.pallas_call(
        _attn_prompt_kernel,
        grid=(batch, nblk),
        in_specs=[tok(D_ATTN), tok(D_KV), tok(D_KV), const(1, D_ATTN), const(1, D_KV), const(N_Q_HEADS, LANES)],
        out_specs=[tok(D_ATTN), cache, cache],
        out_shape=[jax.ShapeDtypeStruct((m, D_ATTN), F32),
                   jax.ShapeDtypeStruct((batch, WINDOW, D_KV), F32),
                   jax.ShapeDtypeStruct((batch, WINDOW, D_KV), F32)],
        scratch_shapes=[pltpu.VMEM((WINDOW, D_KV), BF16), pltpu.VMEM((WINDOW, D_KV), BF16),
                        pltpu.VMEM((D_KV, WINDOW), BF16)],
        compiler_params=pltpu.CompilerParams(dimension_semantics=("arbitrary", "arbitrary")),
        name="attn_prompt",
    )(q, k, v, qw, kw, sinks_b)


def _attn_sample_kernel(t_new, q_ref, k_ref, v_ref, ck_ref, cv_ref, qw_ref, kw_ref, sinks_ref,
                        o_ref, ko_ref, vo_ref):
    ones = _head_ones()
    half0 = lax.broadcasted_iota(jnp.int32, (1, LANES), 1) < HEAD_DIM
    wb = ck_ref.shape[1]
    qi_c = lax.broadcasted_iota(jnp.int32, (t_new, wb), 0)
    kj_c = lax.broadcasted_iota(jnp.int32, (t_new, wb), 1)
    valid_c = (qi_c + wb - kj_c) < WINDOW
    qi_n = lax.broadcasted_iota(jnp.int32, (t_new, t_new), 0)
    kj_n = lax.broadcasted_iota(jnp.int32, (t_new, t_new), 1)
    valid_n = kj_n <= qi_n
    rot = lambda x: pltpu.roll(x, HEAD_DIM, 1).astype(BF16)
    for i in range(q_ref.shape[0]):
        v = v_ref[i]
        qn, kn = _qk_norm(q_ref[i], k_ref[i], qw_ref[...], kw_ref[...], ones)
        ck = ck_ref[i]
        cv = cv_ref[i]
        blocks = [(ck.astype(BF16), rot(ck), cv.astype(BF16), rot(cv), valid_c),
                  (kn.astype(BF16), rot(kn), v.astype(BF16), rot(v), valid_n)]
        o_ref[i] = _attend(qn, blocks, sinks_ref, half0)
        ko_ref[i] = jnp.concatenate([ck[t_new:], kn], axis=0)
        vo_ref[i] = jnp.concatenate([cv[t_new:], v], axis=0)


def _attn_sample(q, k, v, ck, cv, qw, kw, sinks_b):
    b, t_new, _ = q.shape
    wb = ck.shape[1]
    gb = SAMPLE_GROUP
    spec = lambda r, w: pl.BlockSpec((gb, r, w), lambda i: (i, 0, 0))
    const = lambda r, w: pl.BlockSpec((r, w), lambda i: (0, 0))
    return pl.pallas_call(
        functools.partial(_attn_sample_kernel, t_new),
        grid=(b // gb,),
        in_specs=[spec(t_new, D_ATTN), spec(t_new, D_KV), spec(t_new, D_KV), spec(wb, D_KV), spec(wb, D_KV),
                  const(1, D_ATTN), const(1, D_KV), const(N_Q_HEADS, LANES)],
        out_specs=[spec(t_new, D_ATTN), spec(wb, D_KV), spec(wb, D_KV)],
        out_shape=[jax.ShapeDtypeStruct((b, t_new, D_ATTN), F32),
                   jax.ShapeDtypeStruct((b, wb, D_KV), F32),
                   jax.ShapeDtypeStruct((b, wb, D_KV), F32)],
        compiler_params=pltpu.CompilerParams(dimension_semantics=("arbitrary",)),
        name="attn_sample",
    )(q, k, v, ck, cv, qw, kw, sinks_b)


def _merge_kernel(x_ref, or_ref, zr_ref, oa_ref, za_ref, p_ref, wout_ref, g_ref, wgate_ref, wproj_ref, y_ref):
    gr = or_ref[...] * jax.nn.silu(zr_ref[...])
    ga = oa_ref[...] * jax.nn.silu(za_ref[...])
    h = x_ref[...] + _dot(gr, wout_ref[0:D_RWKV, :]) + _dot(ga, wout_ref[D_RWKV:D_MODEL, :])
    gate = jax.nn.sigmoid(_dot(_rms_rows(h, g_ref[...]), wgate_ref[...]))
    y_ref[...] = h + gate * _dot(p_ref[...], wproj_ref[...])


def _merge(x, o_r, z_r, o_a, z_a, p, w_out, g_ple, w_gate, w_proj, tm):
    m = x.shape[0]
    tok = lambda w: pl.BlockSpec((tm, w), lambda i: (i, 0))
    const = lambda r, w: pl.BlockSpec((r, w), lambda i: (0, 0))
    return pl.pallas_call(
        _merge_kernel,
        grid=(m // tm,),
        in_specs=[tok(D_MODEL), tok(D_RWKV), tok(D_RWKV), tok(D_ATTN), tok(D_ATTN), tok(D_PLE),
                  const(D_MODEL, D_MODEL), const(1, D_MODEL), const(D_MODEL, D_MODEL), const(D_PLE, D_MODEL)],
        out_specs=tok(D_MODEL),
        out_shape=jax.ShapeDtypeStruct((m, D_MODEL), F32),
        compiler_params=pltpu.CompilerParams(dimension_semantics=("arbitrary",), vmem_limit_bytes=VMEM_LIMIT),
        name="merge",
    )(x, o_r, z_r, o_a, z_a, p, w_out, g_ple, w_gate, w_proj)


def _state_to_pairs(s):
    b = s.shape[0]
    s = s.reshape(b, N_PAIRS, 2, HEAD_DIM, HEAD_DIM)
    z = jnp.zeros_like(s[:, :, 0])
    top = jnp.concatenate([s[:, :, 0], z], axis=-1)
    bot = jnp.concatenate([z, s[:, :, 1]], axis=-1)
    return jnp.concatenate([top, bot], axis=-2)


def _pairs_to_state(s_bd):
    b = s_bd.shape[0]
    s0 = s_bd[:, :, :HEAD_DIM, :HEAD_DIM]
    s1 = s_bd[:, :, HEAD_DIM:, HEAD_DIM:]
    return jnp.stack([s0, s1], axis=2).reshape(b, 2 * N_PAIRS, HEAD_DIM, HEAD_DIM)


def kernel(x_prompt, x_sample, state_rwkv, state_shift, cache_k, cache_v, p_prompt, p_sample, g_norm, w_in, mu_shift, w0, w_dec2, a0, w_a2, k_k, k_a, r_k, lnx_w, lnx_b, q_norm_w, k_norm_w, sinks, w_out, g_ple, w_ple_gate, w_ple_proj):
    depth = w_in.shape[0]
    bp, seq, _ = x_prompt.shape
    bs, dec, _ = x_sample.shape
    wb = cache_k.shape[2]
    xp = x_prompt.reshape(bp * seq, D_MODEL)
    xs = x_sample.reshape(bs * dec, D_MODEL)
    outs = [[] for _ in range(8)]
    for i in range(depth):
        w_in_b = w_in[i].astype(BF16)
        w_out_b = w_out[i].astype(BF16)
        w_gate_b = w_ple_gate[i].astype(BF16)
        w_proj_b = w_ple_proj[i].astype(BF16)
        zl = jnp.zeros((D_LORA, D_RWKV), F32)
        lora_w = jnp.concatenate([jnp.concatenate([w_dec2[i], zl], axis=1),
                                  jnp.concatenate([zl, w_a2[i]], axis=1)], axis=0).astype(BF16)
        row = lambda t: t.reshape(1, -1)
        vecs = (row(mu_shift[i]), row(w0[i]), row(a0[i]), row(k_k[i]), row(k_a[i]), row(r_k[i]),
                row(lnx_w[i]), row(lnx_b[i]))
        qw = row(jnp.tile(q_norm_w[i], N_Q_HEADS))
        kw = row(jnp.tile(k_norm_w[i], N_KV_HEADS))
        sinks_b = jnp.broadcast_to(sinks[i][:, None], (N_Q_HEADS, LANES))
        gn, gp = row(g_norm[i]), row(g_ple[i])

        f, z_r, q, k, v, z_a = _in_proj(xp, gn, w_in_b, 512)
        o_r, s_bd = _rwkv(f, jnp.zeros((bp, 1, D_SHIFT), F32), None, vecs, lora_w, bp, 1, WINDOW, PROMPT_CHUNK)
        o_a, kc, vc = _attn_prompt(q, k, v, qw, kw, sinks_b, bp)
        outs[0].append(_pairs_to_state(s_bd))
        outs[2].append(f.reshape(bp, seq, D_SHIFT)[:, -1:])
        outs[4].append(kc.reshape(bp, WINDOW, N_KV_HEADS, HEAD_DIM))
        outs[6].append(vc.reshape(bp, WINDOW, N_KV_HEADS, HEAD_DIM))
        xp = _merge(xp, o_r, z_r, o_a, z_a, p_prompt[i].reshape(bp * seq, D_PLE), w_out_b, gp, w_gate_b, w_proj_b, 512)

        f, z_r, q, k, v, z_a = _in_proj(xs, gn, w_in_b, 512)
        o_r, s_bd = _rwkv(f, state_shift[i], _state_to_pairs(state_rwkv[i]), vecs, lora_w, bs, SAMPLE_GROUP, dec, dec)
        o_a, k_buf, v_buf = _attn_sample(q.reshape(bs, dec, D_ATTN), k.reshape(bs, dec, D_KV), v.reshape(bs, dec, D_KV),
                                         cache_k[i].reshape(bs, wb, D_KV), cache_v[i].reshape(bs, wb, D_KV),
                                         qw, kw, sinks_b)
        outs[1].append(_pairs_to_state(s_bd))
        outs[3].append(f.reshape(bs, dec, D_SHIFT)[:, -1:])
        outs[5].append(k_buf.reshape(bs, wb, N_KV_HEADS, HEAD_DIM))
        outs[7].append(v_buf.reshape(bs, wb, N_KV_HEADS, HEAD_DIM))
        xs = _merge(xs, o_r, z_r, o_a.reshape(bs * dec, D_ATTN), z_a, p_sample[i].reshape(bs * dec, D_PLE),
                    w_out_b, gp, w_gate_b, w_proj_b, 512)
    st = lambda l: jnp.stack(l)
    return (xp.reshape(bp, seq, D_MODEL), xs.reshape(bs, dec, D_MODEL),
            st(outs[0]), st(outs[1]), st(outs[2]), st(outs[3]), st(outs[4]), st(outs[5]), st(outs[6]), st(outs[7]))
```

```python
import functools
import math

import jax
import jax.numpy as jnp
from jax import lax
from jax.experimental import pallas as pl
from jax.experimental.pallas import tpu as pltpu

F32 = jnp.float32
BF16 = jnp.bfloat16

D_MODEL = 1024
HEAD_DIM = 64
D_RWKV = 512
D_ATTN = 512
N_KV_HEADS = 2
N_Q_HEADS = 8
Q_PER_KV = N_Q_HEADS // N_KV_HEADS
D_KV = N_KV_HEADS * HEAD_DIM
WINDOW = 128
D_LORA = 64
D_SHIFT = 3 * D_RWKV + 2 * D_LORA
D_PLE = 256
D_IN = D_SHIFT + D_RWKV + D_ATTN + 2 * D_KV + D_ATTN
NORM_EPS = 1e-6
LNX_EPS = 64e-5
NEG_INF = -1e30

LANES = 128
N_PAIRS = D_RWKV // LANES
PROMPT_CHUNK = 64
SAMPLE_GROUP = 8
VMEM_LIMIT = 56 * 1024 * 1024


def _dot(a, b):
    return jnp.dot(a.astype(BF16), b.astype(BF16), preferred_element_type=F32)


def _dot_nt(a, b):
    return lax.dot_general(a.astype(BF16), b.astype(BF16), (((1,), (1,)), ((), ())), preferred_element_type=F32)


def _dot_tn(a, b):
    return lax.dot_general(a.astype(BF16), b.astype(BF16), (((0,), (0,)), ((), ())), preferred_element_type=F32)


def _split3_dot_left(w, x):
    hi = x.astype(BF16)
    r1 = x - hi.astype(F32)
    mid = r1.astype(BF16)
    lo = (r1 - mid.astype(F32)).astype(BF16)
    return (jnp.dot(jnp.concatenate([w, w], axis=1), jnp.concatenate([hi, mid], axis=0), preferred_element_type=F32)
            + jnp.dot(w, lo, preferred_element_type=F32))


def _head_ones():
    r = (lax.broadcasted_iota(jnp.int32, (2 * LANES, LANES), 0) % LANES) // HEAD_DIM
    c = lax.broadcasted_iota(jnp.int32, (2 * LANES, LANES), 1) // HEAD_DIM
    return (r == c).astype(BF16)


def _head_sum(x, ones):
    tiles = []
    for i in range(0, x.shape[1], LANES):
        xt = x[:, i:i + LANES]
        hi = xt.astype(BF16)
        lo = (xt - hi.astype(F32)).astype(BF16)
        tiles.append(jnp.dot(jnp.concatenate([hi, lo], axis=1), ones, preferred_element_type=F32))
    return tiles[0] if len(tiles) == 1 else jnp.concatenate(tiles, axis=1)


def _rms_rows(x, g):
    return x * lax.rsqrt(jnp.mean(x * x, axis=-1, keepdims=True) + NORM_EPS) * g


_IN_SPLITS = (D_SHIFT, D_RWKV, D_ATTN, D_KV, D_KV, D_ATTN)


def _in_proj_kernel(x_ref, g_ref, w_ref, *out_refs):
    h = _dot(_rms_rows(x_ref[...], g_ref[...]), w_ref[...])
    off = 0
    for o_ref, width in zip(out_refs, _IN_SPLITS):
        o_ref[...] = h[:, off:off + width]
        off += width


def _in_proj(x, g_norm, w_in_bf16, tm):
    m = x.shape[0]
    return pl.pallas_call(
        _in_proj_kernel,
        grid=(m // tm,),
        in_specs=[pl.BlockSpec((tm, D_MODEL), lambda i: (i, 0)),
                  pl.BlockSpec((1, D_MODEL), lambda i: (0, 0)),
                  pl.BlockSpec((D_MODEL, D_IN), lambda i: (0, 0))],
        out_specs=[pl.BlockSpec((tm, w), lambda i: (i, 0)) for w in _IN_SPLITS],
        out_shape=[jax.ShapeDtypeStruct((m, w), F32) for w in _IN_SPLITS],
        compiler_params=pltpu.CompilerParams(dimension_semantics=("arbitrary",), vmem_limit_bytes=VMEM_LIMIT),
        name="in_proj",
    )(x, g_norm, w_in_bf16)


def _stack(z, half0):
    return jnp.concatenate([jnp.where(half0, z, 0.0), jnp.where(half0, 0.0, z)], axis=0).astype(BF16)


def _rwkv_recurrence(at, rt, bt, kt, v, e_cum, s_scr, cm, seg, segs_per_state):
    n_rows = at.shape[0]
    n_blk = n_rows // cm
    lane = lax.broadcasted_iota(jnp.int32, (1, LANES), 1)
    half0 = lane < HEAD_DIM
    ri = lax.broadcasted_iota(jnp.int32, (cm, LANES), 0)
    ci = lax.broadcasted_iota(jnp.int32, (cm, LANES), 1) % cm
    same = (ri // seg) == (ci // seg)
    tri_strict = same & (ci < ri)
    tri_incl = same & (ci <= ri)
    eye_c = (ri == ci).astype(F32)
    sr = lax.broadcasted_iota(jnp.int32, (LANES, LANES), 0) // HEAD_DIM
    sc = lax.broadcasted_iota(jnp.int32, (LANES, LANES), 1) // HEAD_DIM
    state_mask = sr == sc
    n_levels = max(int(math.log2(seg)) - 1, 0)
    blocks = [(rb, p) for rb in range(n_blk) for p in range(N_PAIRS)]

    def tile(x, rb, p):
        return x[rb * cm:(rb + 1) * cm, p * LANES:(p + 1) * LANES]

    ops = {}
    for rb, p in blocks:
        at_p, rt_p, bt_p, kt_p, v_p = (tile(x, rb, p) for x in (at, rt, bt, kt, v))
        ops[rb, p] = dict(at=at_p, rt=rt_p, bt=bt_p, kt=kt_p, v=v_p, v_s=_stack(v_p, half0))
    for blk in blocks:
        o = ops[blk]
        g = _dot_nt(jnp.concatenate([o["at"], o["rt"]], axis=0),
                    jnp.concatenate([_stack(o["bt"], half0), _stack(o["kt"], half0)], axis=0))
        o["g_ab"] = jnp.where(tri_strict, g[:cm, :LANES], 0.0)
        o["g_ak"] = jnp.where(tri_strict, g[:cm, LANES:], 0.0)
        o["g_r"] = jnp.concatenate([jnp.where(tri_incl, g[cm:, :LANES], 0.0),
                                    jnp.where(tri_incl, g[cm:, LANES:], 0.0)], axis=1).astype(BF16)

    for blk in blocks:
        ops[blk]["t_inv"] = eye_c + ops[blk]["g_ab"]
    if n_levels > 0:
        for blk in blocks:
            ops[blk]["a_pow"] = _dot(ops[blk]["g_ab"], _stack(ops[blk]["g_ab"], half0))
        for lvl in range(n_levels):
            last = lvl == n_levels - 1
            for blk in blocks:
                o = ops[blk]
                if last:
                    o["t_inv"] = o["t_inv"] + _dot(o["a_pow"], _stack(o["t_inv"], half0))
                else:
                    m = _dot(o["a_pow"], jnp.concatenate([_stack(o["a_pow"], half0), _stack(o["t_inv"], half0)], axis=1))
                    o["t_inv"] = o["t_inv"] + m[:, LANES:]
                    o["a_pow"] = m[:, :LANES]

    for blk in blocks:
        o = ops[blk]
        gakv = _dot(o["g_ak"], o["v_s"])
        z = _dot(o["t_inv"], jnp.concatenate([_stack(o["at"], half0), _stack(gakv, half0)], axis=1))
        o["a_hat"], o["p0"] = z[:, :LANES], z[:, LANES:]

    n_seg = n_rows // seg
    n_states = n_seg // segs_per_state
    per_blk = cm // seg
    p_parts = {blk: [None] * per_blk for blk in blocks}
    y_parts = {blk: [None] * per_blk for blk in blocks}
    for step in range(segs_per_state):
        segs = [st * segs_per_state + step for st in range(n_states)]
        proj = {}
        for g_i in segs:
            rb, off = (g_i * seg) // cm, (g_i * seg) % cm
            for p in range(N_PAIRS):
                o = ops[rb, p]
                lhs = jnp.concatenate([o["a_hat"][off:off + seg], o["rt"][off:off + seg]], axis=0)
                proj[g_i, p] = _dot_nt(lhs, s_scr[(g_i // segs_per_state) * N_PAIRS + p])
        for g_i in segs:
            rb, off = (g_i * seg) // cm, (g_i * seg) % cm
            for p in range(N_PAIRS):
                o = ops[rb, p]
                p_seg = proj[g_i, p][:seg] + o["p0"][off:off + seg]
                p_parts[rb, p][off // seg] = p_seg
                y_parts[rb, p][off // seg] = proj[g_i, p][seg:]
                upd = _dot_tn(jnp.concatenate([p_seg, o["v"][off:off + seg]], axis=0),
                              jnp.concatenate([o["bt"][off:off + seg], o["kt"][off:off + seg]], axis=0))
                si = (g_i // segs_per_state) * N_PAIRS + p
                row_end = g_i * seg + seg - 1
                w_end = e_cum[row_end:row_end + 1, p * LANES:(p + 1) * LANES]
                s_scr[si] = w_end * (s_scr[si] + jnp.where(state_mask, upd, 0.0))

    cat = lambda parts: parts[0] if len(parts) == 1 else jnp.concatenate(parts, axis=0)
    rows = []
    for rb in range(n_blk):
        tiles = []
        for p in range(N_PAIRS):
            o = ops[rb, p]
            pv_s = jnp.concatenate([_stack(cat(p_parts[rb, p]), half0), o["v_s"]], axis=0)
            tiles.append(cat(y_parts[rb, p]) + jnp.dot(o["g_r"], pv_s, preferred_element_type=F32))
        rows.append(jnp.concatenate(tiles, axis=1))
    return cat(rows)


def _rwkv_kernel(bb, tb, cm, seg, has_state, carry_prev, f_ref, prev0_ref, mu_ref, w0_ref, a0_ref, lora_ref,
                 kk_ref, ka_ref, rk_ref, lnw_ref, lnb_ref, *rest):
    if has_state:
        s0_ref, o_ref, s_out_ref, s_scr, prev_scr = rest
    else:
        o_ref, s_out_ref, s_scr, prev_scr = rest
    j = pl.program_id(1)
    n_rows = bb * tb

    @pl.when(j == 0)
    def _():
        prev_scr[...] = prev0_ref[...]
        if has_state:
            zero = jnp.zeros((HEAD_DIM, HEAD_DIM), F32)
            for b in range(bb):
                for p in range(N_PAIRS):
                    top = jnp.concatenate([s0_ref[b, 2 * p], zero], axis=1)
                    bot = jnp.concatenate([zero, s0_ref[b, 2 * p + 1]], axis=1)
                    s_scr[b * N_PAIRS + p] = jnp.concatenate([top, bot], axis=0)
        else:
            s_scr[...] = jnp.zeros(s_scr.shape, F32)

    f = f_ref[...]
    row = lax.broadcasted_iota(jnp.int32, (n_rows, 1), 0)
    if bb == 1:
        prev_rows = prev_scr[0]
    else:
        prev_rows = jnp.broadcast_to(prev_scr[...], (bb, tb, D_SHIFT)).reshape(n_rows, D_SHIFT)
    f_prev = jnp.where(row % tb == 0, prev_rows, pltpu.roll(f, 1, 0))
    if carry_prev:
        prev_scr[0] = f[n_rows - 1:n_rows, :]
    fs = f + (f_prev - f) * mu_ref[...]
    r = fs[:, 0:D_RWKV]
    k = fs[:, D_RWKV:2 * D_RWKV]
    v = fs[:, 2 * D_RWKV:3 * D_RWKV]
    wa = fs[:, 3 * D_RWKV:D_SHIFT]
    lane = lax.broadcasted_iota(jnp.int32, (1, LANES), 1)
    lora = _dot(jnp.where(lane < D_LORA, jnp.tanh(wa), wa), lora_ref[...])
    lw = (-math.exp(-0.5)) * jax.nn.sigmoid(w0_ref[...] + lora[:, 0:D_RWKV])
    a_sig = jax.nn.sigmoid(a0_ref[...] + lora[:, D_RWKV:2 * D_RWKV])
    ones = _head_ones()
    kk = k * kk_ref[...]
    kk = kk * lax.rsqrt(jnp.maximum(_head_sum(kk * kk, ones), 1e-24))
    k2 = k * (1.0 + (a_sig - 1.0) * ka_ref[...])

    ti = lax.broadcasted_iota(jnp.int32, (n_rows, n_rows), 0)
    tj = lax.broadcasted_iota(jnp.int32, (n_rows, n_rows), 1)
    tri = ((ti // seg == tj // seg) & (tj <= ti)).astype(BF16)
    cum = _split3_dot_left(tri, lw)
    e_cum = jnp.exp(cum)
    e_inv = jnp.exp(-cum)
    y = _rwkv_recurrence(-kk * jnp.exp(cum - lw), r * e_cum, kk * a_sig * e_inv, k2 * e_inv, v, e_cum,
                         s_scr, cm, seg, tb // seg)

    inv_n = 1.0 / HEAD_DIM
    yc = y - _head_sum(y, ones) * inv_n
    var = _head_sum(yc * yc, ones) * inv_n
    yn = yc * lax.rsqrt(var + LNX_EPS) * lnw_ref[...] + lnb_ref[...]
    bonus = _head_sum(r * k2 * rk_ref[...], ones) * v
    o_ref[...] = yn + bonus

    @pl.when(j == pl.num_programs(1) - 1)
    def _():
        for b in range(bb):
            for p in range(N_PAIRS):
                s = s_scr[b * N_PAIRS + p]
                s_out_ref[b, 2 * p] = s[:HEAD_DIM, :HEAD_DIM]
                s_out_ref[b, 2 * p + 1] = s[HEAD_DIM:, HEAD_DIM:]


def _rwkv(f, prev0, s0, vecs, lora_w, batch, bb, tb, seg):
    m = f.shape[0]
    n_rows = bb * tb
    cm = HEAD_DIM
    assert n_rows % cm == 0 and cm % seg == 0 and tb % seg == 0
    n_outer = batch // bb
    nblk = m // batch // tb
    has_state = s0 is not None
    vec_spec = lambda n: pl.BlockSpec((1, n), lambda b, j: (0, 0))
    mu, w0, a0, k_k, k_a, r_k, lnx_w, lnx_b = vecs
    in_specs = [pl.BlockSpec((n_rows, D_SHIFT), lambda b, j: (b * nblk + j, 0)),
                pl.BlockSpec((bb, 1, D_SHIFT), lambda b, j: (b, 0, 0)),
                vec_spec(D_SHIFT), vec_spec(D_RWKV), vec_spec(D_RWKV),
                pl.BlockSpec((LANES, 2 * D_RWKV), lambda b, j: (0, 0)),
                vec_spec(D_RWKV), vec_spec(D_RWKV), vec_spec(D_RWKV), vec_spec(D_RWKV), vec_spec(D_RWKV)]
    args = [f, prev0, mu, w0, a0, lora_w, k_k, k_a, r_k, lnx_w, lnx_b]
    state_spec = pl.BlockSpec((bb, 2 * N_PAIRS, HEAD_DIM, HEAD_DIM), lambda b, j: (b, 0, 0, 0))
    if has_state:
        in_specs.append(state_spec)
        args.append(s0)
    return pl.pallas_call(
        functools.partial(_rwkv_kernel, bb, tb, cm, seg, has_state, nblk > 1),
        grid=(n_outer, nblk),
        in_specs=in_specs,
        out_specs=[pl.BlockSpec((n_rows, D_RWKV), lambda b, j: (b * nblk + j, 0)), state_spec],
        out_shape=[jax.ShapeDtypeStruct((m, D_RWKV), F32),
                   jax.ShapeDtypeStruct((batch, 2 * N_PAIRS, HEAD_DIM, HEAD_DIM), F32)],
        scratch_shapes=[pltpu.VMEM((bb * N_PAIRS, LANES, LANES), F32), pltpu.VMEM((bb, 1, D_SHIFT), F32)],
        compiler_params=pltpu.CompilerParams(dimension_semantics=("arbitrary", "arbitrary"),
                                             vmem_limit_bytes=VMEM_LIMIT),
        name="rwkv",
    )(*args)


def _qk_norm(q, k, qw, kw, ones):
    inv_n = 1.0 / HEAD_DIM
    qn = q * lax.rsqrt(_head_sum(q * q, ones) * inv_n + NORM_EPS) * (qw * (HEAD_DIM ** -0.5))
    kn = k * lax.rsqrt(_head_sum(k * k, ones) * inv_n + NORM_EPS) * kw
    return qn, kn


def _attend(qn, score_blocks, sinks_ref, half0):
    heads = range(N_Q_HEADS)
    aligned = [(h // Q_PER_KV) == (h % 2) for h in heads]
    qm = [jnp.where(half0 if h % 2 == 0 else jnp.logical_not(half0),
                    qn[:, (h // 2) * LANES:(h // 2 + 1) * LANES], 0.0).astype(BF16) for h in heads]
    sink = [sinks_ref[h:h + 1, 0:1] for h in heads]
    scores = [[jnp.where(valid, _dot_nt(qm[h], keys if aligned[h] else keys_rot), NEG_INF)
               for keys, keys_rot, _, _, valid in score_blocks] for h in heads]
    m = []
    for h in heads:
        mh = sink[h]
        for s in scores[h]:
            mh = jnp.maximum(mh, jnp.max(s, axis=-1, keepdims=True))
        m.append(mh)
    probs = [[jnp.exp(s - m[h]) for s in scores[h]] for h in heads]
    denom = []
    for h in heads:
        d = jnp.exp(sink[h] - m[h])
        for pr in probs[h]:
            d = d + jnp.sum(pr, axis=-1, keepdims=True)
        denom.append(d)
    outs = []
    for h in heads:
        acc = None
        for pr, (_, _, vals, vals_rot, _) in zip(probs[h], score_blocks):
            pv = _dot(pr, vals if aligned[h] else vals_rot)
            acc = pv if acc is None else acc + pv
        outs.append(acc * (1.0 / denom[h]))
    return jnp.concatenate([jnp.where(half0, outs[2 * t], outs[2 * t + 1]) for t in range(D_ATTN // LANES)], axis=1)


def _attend_keys_major(qn, keys, keys_rot, vals_t, valid_t, sinks_ref, half0):
    heads = range(N_Q_HEADS)
    scores = []
    for h in heads:
        e, g = h % 2, h // Q_PER_KV
        qm = jnp.where(half0 if e == 0 else jnp.logical_not(half0), qn[:, (h // 2) * LANES:(h // 2 + 1) * LANES], 0.0)
        scores.append(jnp.where(valid_t, _dot_nt(keys if g == e else keys_rot, qm.astype(BF16)), NEG_INF))
    probs = []
    for h in heads:
        sink = sinks_ref[h:h + 1, 0:1]
        m = jnp.maximum(jnp.max(scores[h], axis=0, keepdims=True), sink)
        pr = jnp.exp(scores[h] - m)
        denom = jnp.sum(pr, axis=0, keepdims=True) + jnp.exp(sink - m)
        probs.append((pr * (1.0 / denom)).astype(BF16))
    outs = [jnp.dot(vals_t[(h // Q_PER_KV) * HEAD_DIM:(h // Q_PER_KV + 1) * HEAD_DIM, :], probs[h],
                    preferred_element_type=F32) for h in heads]
    return jnp.concatenate([jnp.concatenate(outs[2 * t:2 * t + 2], axis=0).T for t in range(D_ATTN // LANES)], axis=1)


def _attn_prompt_kernel(q_ref, k_ref, v_ref, qw_ref, kw_ref, sinks_ref, o_ref, kc_ref, vc_ref,
                        kprev, kprev_rot, vprev_t):
    j = pl.program_id(1)

    @pl.when(j == 0)
    def _():
        kprev[...] = jnp.zeros(kprev.shape, BF16)
        kprev_rot[...] = jnp.zeros(kprev_rot.shape, BF16)
        vprev_t[...] = jnp.zeros(vprev_t.shape, BF16)

    ones = _head_ones()
    half0 = lax.broadcasted_iota(jnp.int32, (1, LANES), 1) < HEAD_DIM
    v = v_ref[...]
    qn, kn = _qk_norm(q_ref[...], k_ref[...], qw_ref[...], kw_ref[...], ones)
    kn_b = kn.astype(BF16)
    kn_rot = pltpu.roll(kn, HEAD_DIM, 1).astype(BF16)
    v_t = v.T.astype(BF16)
    keys = jnp.concatenate([kprev[...], kn_b], axis=0)
    keys_rot = jnp.concatenate([kprev_rot[...], kn_rot], axis=0)
    vals_t = jnp.concatenate([vprev_t[...], v_t], axis=1)
    kj = lax.broadcasted_iota(jnp.int32, (2 * WINDOW, WINDOW), 0)
    qi = lax.broadcasted_iota(jnp.int32, (2 * WINDOW, WINDOW), 1)
    valid_t = (kj > qi) & (kj <= qi + WINDOW) & ((kj >= WINDOW) | (j > 0))
    o_ref[...] = _attend_keys_major(qn, keys, keys_rot, vals_t, valid_t, sinks_ref, half0)
    kprev[...] = kn_b
    kprev_rot[...] = kn_rot
    vprev_t[...] = v_t
    kc_ref[0] = kn
    vc_ref[0] = v


def _attn_prompt(q, k, v, qw, kw, sinks_b, batch):
    m = q.shape[0]
    nblk = m // batch // WINDOW
    tok = lambda w: pl.BlockSpec((WINDOW, w), lambda b, j: (b * nblk + j, 0))
    const = lambda r, w: pl.BlockSpec((r, w), lambda b, j: (0, 0))
    cache = pl.BlockSpec((1, WINDOW, D_KV), lambda b, j: (b, 0, 0))
    return pl.pallas_call(
        _attn_prompt_kernel,
        grid=(batch, nblk),
        in_specs=[tok(D_ATTN), tok(D_KV), tok(D_KV), const(1, D_ATTN), const(1, D_KV), const(N_Q_HEADS, LANES)],
        out_specs=[tok(D_ATTN), cache, cache],
        out_shape=[jax.ShapeDtypeStruct((m, D_ATTN), F32),
                   jax.ShapeDtypeStruct((batch, WINDOW, D_KV), F32),
                   jax.ShapeDtypeStruct((batch, WINDOW, D_KV), F32)],
        scratch_shapes=[pltpu.VMEM((WINDOW, D_KV), BF16), pltpu.VMEM((WINDOW, D_KV), BF16),
                        pltpu.VMEM((D_KV, WINDOW), BF16)],
        compiler_params=pltpu.CompilerParams(dimension_semantics=("arbitrary", "arbitrary")),
        name="attn_prompt",
    )(q, k, v, qw, kw, sinks_b)


def _attn_sample_kernel(t_new, q_ref, k_ref, v_ref, ck_ref, cv_ref, qw_ref, kw_ref, sinks_ref,
                        o_ref, ko_ref, vo_ref):
    ones = _head_ones()
    half0 = lax.broadcasted_iota(jnp.int32, (1, LANES), 1) < HEAD_DIM
    wb = ck_ref.shape[1]
    qi_c = lax.broadcasted_iota(jnp.int32, (t_new, wb), 0)
    kj_c = lax.broadcasted_iota(jnp.int32, (t_new, wb), 1)
    valid_c = (qi_c + wb - kj_c) < WINDOW
    qi_n = lax.broadcasted_iota(jnp.int32, (t_new, t_new), 0)
    kj_n = lax.broadcasted_iota(jnp.int32, (t_new, t_new), 1)
    valid_n = kj_n <= qi_n
    rot = lambda x: pltpu.roll(x, HEAD_DIM, 1).astype(BF16)
    for i in range(q_ref.shape[0]):
        v = v_ref[i]
        qn, kn = _qk_norm(q_ref[i], k_ref[i], qw_ref[...], kw_ref[...], ones)
        ck = ck_ref[i]
        cv = cv_ref[i]
        blocks = [(ck.astype(BF16), rot(ck), cv.astype(BF16), rot(cv), valid_c),
                  (kn.astype(BF16), rot(kn), v.astype(BF16), rot(v), valid_n)]
        o_ref[i] = _attend(qn, blocks, sinks_ref, half0)
        ko_ref[i] = jnp.concatenate([ck[t_new:], kn], axis=0)
        vo_ref[i] = jnp.concatenate([cv[t_new:], v], axis=0)


def _attn_sample(q, k, v, ck, cv, qw, kw, sinks_b):
    b, t_new, _ = q.shape
    wb = ck.shape[1]
    gb = SAMPLE_GROUP
    spec = lambda r, w: pl.BlockSpec((gb, r, w), lambda i: (i, 0, 0))
    const = lambda r, w: pl.BlockSpec((r, w), lambda i: (0, 0))
    return pl.pallas_call(
        functools.partial(_attn_sample_kernel, t_new),
        grid=(b // gb,),
        in_specs=[spec(t_new, D_ATTN), spec(t_new, D_KV), spec(t_new, D_KV), spec(wb, D_KV), spec(wb, D_KV),
                  const(1, D_ATTN), const(1, D_KV), const(N_Q_HEADS, LANES)],
        out_specs=[spec(t_new, D_ATTN), spec(wb, D_KV), spec(wb, D_KV)],
        out_shape=[jax.ShapeDtypeStruct((b, t_new, D_ATTN), F32),
                   jax.ShapeDtypeStruct((b, wb, D_KV), F32),
                   jax.ShapeDtypeStruct((b, wb, D_KV), F32)],
        compiler_params=pltpu.CompilerParams(dimension_semantics=("arbitrary",)),
        name="attn_sample",
    )(q, k, v, ck, cv, qw, kw, sinks_b)


def _merge_kernel(x_ref, or_ref, zr_ref, oa_ref, za_ref, p_ref, wout_ref, g_ref, wgate_ref, wproj_ref, y_ref):
    gr = or_ref[...] * jax.nn.silu(zr_ref[...])
    ga = oa_ref[...] * jax.nn.silu(za_ref[...])
    h = x_ref[...] + _dot(gr, wout_ref[0:D_RWKV, :]) + _dot(ga, wout_ref[D_RWKV:D_MODEL, :])
    gate = jax.nn.sigmoid(_dot(_rms_rows(h, g_ref[...]), wgate_ref[...]))
    y_ref[...] = h + gate * _dot(p_ref[...], wproj_ref[...])


def _merge(x, o_r, z_r, o_a, z_a, p, w_out, g_ple, w_gate, w_proj, tm):
    m = x.shape[0]
    tok = lambda w: pl.BlockSpec((tm, w), lambda i: (i, 0))
    const = lambda r, w: pl.BlockSpec((r, w), lambda i: (0, 0))
    return pl.pallas_call(
        _merge_kernel,
        grid=(m // tm,),
        in_specs=[tok(D_MODEL), tok(D_RWKV), tok(D_RWKV), tok(D_ATTN), tok(D_ATTN), tok(D_PLE),
                  const(D_MODEL, D_MODEL), const(1, D_MODEL), const(D_MODEL, D_MODEL), const(D_PLE, D_MODEL)],
        out_specs=tok(D_MODEL),
        out_shape=jax.ShapeDtypeStruct((m, D_MODEL), F32),
        compiler_params=pltpu.CompilerParams(dimension_semantics=("arbitrary",), vmem_limit_bytes=VMEM_LIMIT),
        name="merge",
    )(x, o_r, z_r, o_a, z_a, p, w_out, g_ple, w_gate, w_proj)


def kernel(x_prompt, x_sample, state_rwkv, state_shift, cache_k, cache_v, p_prompt, p_sample, g_norm, w_in, mu_shift, w0, w_dec2, a0, w_a2, k_k, k_a, r_k, lnx_w, lnx_b, q_norm_w, k_norm_w, sinks, w_out, g_ple, w_ple_gate, w_ple_proj):
    depth = w_in.shape[0]
    bp, seq, _ = x_prompt.shape
    bs, dec, _ = x_sample.shape
    wb = cache_k.shape[2]
    xp = x_prompt.reshape(bp * seq, D_MODEL)
    xs = x_sample.reshape(bs * dec, D_MODEL)
    outs = [[] for _ in range(8)]
    for i in range(depth):
        w_in_b = w_in[i].astype(BF16)
        w_out_b = w_out[i].astype(BF16)
        w_gate_b = w_ple_gate[i].astype(BF16)
        w_proj_b = w_ple_proj[i].astype(BF16)
        zl = jnp.zeros((D_LORA, D_RWKV), F32)
        lora_w = jnp.concatenate([jnp.concatenate([w_dec2[i], zl], axis=1),
                                  jnp.concatenate([zl, w_a2[i]], axis=1)], axis=0).astype(BF16)
        row = lambda t: t.reshape(1, -1)
        vecs = (row(mu_shift[i]), row(w0[i]), row(a0[i]), row(k_k[i]), row(k_a[i]), row(r_k[i]),
                row(lnx_w[i]), row(lnx_b[i]))
        qw = row(jnp.tile(q_norm_w[i], N_Q_HEADS))
        kw = row(jnp.tile(k_norm_w[i], N_KV_HEADS))
        sinks_b = jnp.broadcast_to(sinks[i][:, None], (N_Q_HEADS, LANES))
        gn, gp = row(g_norm[i]), row(g_ple[i])

        f, z_r, q, k, v, z_a = _in_proj(xp, gn, w_in_b, 512)
        o_r, s_p = _rwkv(f, jnp.zeros((bp, 1, D_SHIFT), F32), None, vecs, lora_w, bp, 1, WINDOW, PROMPT_CHUNK)
        o_a, kc, vc = _attn_prompt(q, k, v, qw, kw, sinks_b, bp)
        outs[0].append(s_p)
        outs[2].append(f.reshape(bp, seq, D_SHIFT)[:, -1:])
        outs[4].append(kc.reshape(bp, WINDOW, N_KV_HEADS, HEAD_DIM))
        outs[6].append(vc.reshape(bp, WINDOW, N_KV_HEADS, HEAD_DIM))
        xp = _merge(xp, o_r, z_r, o_a, z_a, p_prompt[i].reshape(bp * seq, D_PLE), w_out_b, gp, w_gate_b, w_proj_b, 512)

        f, z_r, q, k, v, z_a = _in_proj(xs, gn, w_in_b, 512)
        o_r, s_s = _rwkv(f, state_shift[i], state_rwkv[i], vecs, lora_w, bs, SAMPLE_GROUP, dec, dec)
        o_a, k_buf, v_buf = _attn_sample(q.reshape(bs, dec, D_ATTN), k.reshape(bs, dec, D_KV), v.reshape(bs, dec, D_KV),
                                         cache_k[i].reshape(bs, wb, D_KV), cache_v[i].reshape(bs, wb, D_KV),
                                         qw, kw, sinks_b)
        outs[1].append(s_s)
        outs[3].append(f.reshape(bs, dec, D_SHIFT)[:, -1:])
        outs[5].append(k_buf.reshape(bs, wb, N_KV_HEADS, HEAD_DIM))
        outs[7].append(v_buf.reshape(bs, wb, N_KV_HEADS, HEAD_DIM))
        xs = _merge(xs, o_r, z_r, o_a.reshape(bs * dec, D_ATTN), z_a, p_sample[i].reshape(bs * dec, D_PLE),
                    w_out_b, gp, w_gate_b, w_proj_b, 512)
    st = lambda l: jnp.stack(l)
    return (xp.reshape(bp, seq, D_MODEL), xs.reshape(bs, dec, D_MODEL),
            st(outs[0]), st(outs[1]), st(outs[2]), st(outs[3]), st(outs[4]), st(outs[5]), st(outs[6]), st(outs[7]))
```

```python
import functools
import math

import jax
import jax.numpy as jnp
from jax import lax
from jax.experimental import pallas as pl
from jax.experimental.pallas import tpu as pltpu

F32 = jnp.float32
BF16 = jnp.bfloat16

D_MODEL = 1024
HEAD_DIM = 64
D_RWKV = 512
D_ATTN = 512
N_KV_HEADS = 2
N_Q_HEADS = 8
Q_PER_KV = N_Q_HEADS // N_KV_HEADS
D_KV = N_KV_HEADS * HEAD_DIM
WINDOW = 128
D_LORA = 64
D_SHIFT = 3 * D_RWKV + 2 * D_LORA
D_PLE = 256
D_IN = D_SHIFT + D_RWKV + D_ATTN + 2 * D_KV + D_ATTN
NORM_EPS = 1e-6
LNX_EPS = 64e-5
NEG_INF = -1e30

LANES = 128
N_PAIRS = D_RWKV // LANES
PROMPT_CHUNK = 64
SAMPLE_GROUP = 8
VMEM_LIMIT = 56 * 1024 * 1024


def _dot(a, b):
    return jnp.dot(a.astype(BF16), b.astype(BF16), preferred_element_type=F32)


def _dot_nt(a, b):
    return lax.dot_general(a.astype(BF16), b.astype(BF16), (((1,), (1,)), ((), ())), preferred_element_type=F32)


def _dot_tn(a, b):
    return lax.dot_general(a.astype(BF16), b.astype(BF16), (((0,), (0,)), ((), ())), preferred_element_type=F32)


def _split3_dot_left(w, x):
    hi = x.astype(BF16)
    r1 = x - hi.astype(F32)
    mid = r1.astype(BF16)
    lo = (r1 - mid.astype(F32)).astype(BF16)
    return (jnp.dot(jnp.concatenate([w, w], axis=1), jnp.concatenate([hi, mid], axis=0), preferred_element_type=F32)
            + jnp.dot(w, lo, preferred_element_type=F32))


def _head_ones():
    r = (lax.broadcasted_iota(jnp.int32, (2 * LANES, LANES), 0) % LANES) // HEAD_DIM
    c = lax.broadcasted_iota(jnp.int32, (2 * LANES, LANES), 1) // HEAD_DIM
    return (r == c).astype(BF16)


def _head_sum(x, ones):
    tiles = []
    for i in range(0, x.shape[1], LANES):
        xt = x[:, i:i + LANES]
        hi = xt.astype(BF16)
        lo = (xt - hi.astype(F32)).astype(BF16)
        tiles.append(jnp.dot(jnp.concatenate([hi, lo], axis=1), ones, preferred_element_type=F32))
    return tiles[0] if len(tiles) == 1 else jnp.concatenate(tiles, axis=1)


def _rms_rows(x, g):
    return x * lax.rsqrt(jnp.mean(x * x, axis=-1, keepdims=True) + NORM_EPS) * g


_IN_SPLITS = (D_SHIFT, D_RWKV, D_ATTN, D_KV, D_KV, D_ATTN)


def _in_proj_kernel(x_ref, g_ref, w_ref, *out_refs):
    h = _dot(_rms_rows(x_ref[...], g_ref[...]), w_ref[...])
    off = 0
    for o_ref, width in zip(out_refs, _IN_SPLITS):
        o_ref[...] = h[:, off:off + width]
        off += width


def _in_proj(x, g_norm, w_in_bf16, tm):
    m = x.shape[0]
    return pl.pallas_call(
        _in_proj_kernel,
        grid=(m // tm,),
        in_specs=[pl.BlockSpec((tm, D_MODEL), lambda i: (i, 0)),
                  pl.BlockSpec((1, D_MODEL), lambda i: (0, 0)),
                  pl.BlockSpec((D_MODEL, D_IN), lambda i: (0, 0))],
        out_specs=[pl.BlockSpec((tm, w), lambda i: (i, 0)) for w in _IN_SPLITS],
        out_shape=[jax.ShapeDtypeStruct((m, w), F32) for w in _IN_SPLITS],
        compiler_params=pltpu.CompilerParams(dimension_semantics=("arbitrary",), vmem_limit_bytes=VMEM_LIMIT),
        name="in_proj",
    )(x, g_norm, w_in_bf16)


def _stack(z, half0):
    return jnp.concatenate([jnp.where(half0, z, 0.0), jnp.where(half0, 0.0, z)], axis=0).astype(BF16)


def _rwkv_recurrence(at, rt, bt, kt, v, e_cum, s_scr, cm, seg, segs_per_state, fill):
    n_rows = at.shape[0]
    n_blk = n_rows // cm
    lane = lax.broadcasted_iota(jnp.int32, (1, LANES), 1)
    half0 = lane < HEAD_DIM
    ri = lax.broadcasted_iota(jnp.int32, (cm, LANES), 0)
    ci = lax.broadcasted_iota(jnp.int32, (cm, LANES), 1) % cm
    same = (ri // seg) == (ci // seg)
    tri_strict = same & (ci < ri)
    tri_incl = same & (ci <= ri)
    eye_c = (ri == ci).astype(F32)
    sr = lax.broadcasted_iota(jnp.int32, (LANES, LANES), 0) // HEAD_DIM
    sc = lax.broadcasted_iota(jnp.int32, (LANES, LANES), 1) // HEAD_DIM
    state_mask = sr == sc
    n_levels = max(int(math.log2(seg)) - 1, 0)
    blocks = [(rb, p) for rb in range(n_blk) for p in range(N_PAIRS)]

    def tile(x, rb, p):
        return x[rb * cm:(rb + 1) * cm, p * LANES:(p + 1) * LANES]

    ops = {}
    for rb, p in blocks:
        at_p, rt_p, bt_p, kt_p, v_p = (tile(x, rb, p) for x in (at, rt, bt, kt, v))
        ops[rb, p] = dict(at=at_p, rt=rt_p, bt=bt_p, kt=kt_p, v=v_p, v_s=_stack(v_p, half0))
    for blk in blocks:
        o = ops[blk]
        g = _dot_nt(jnp.concatenate([o["at"], o["rt"]], axis=0),
                    jnp.concatenate([_stack(o["bt"], half0), _stack(o["kt"], half0)], axis=0))
        o["g_ab"] = jnp.where(tri_strict, g[:cm, :LANES], 0.0)
        o["g_ak"] = jnp.where(tri_strict, g[:cm, LANES:], 0.0)
        o["g_r"] = jnp.concatenate([jnp.where(tri_incl, g[cm:, :LANES], 0.0),
                                    jnp.where(tri_incl, g[cm:, LANES:], 0.0)], axis=1).astype(BF16)
    fill()

    for blk in blocks:
        ops[blk]["t_inv"] = eye_c + ops[blk]["g_ab"]
    if n_levels > 0:
        for blk in blocks:
            ops[blk]["a_pow"] = _dot(ops[blk]["g_ab"], _stack(ops[blk]["g_ab"], half0))
        fill()
        for lvl in range(n_levels):
            last = lvl == n_levels - 1
            for blk in blocks:
                o = ops[blk]
                if last:
                    o["t_inv"] = o["t_inv"] + _dot(o["a_pow"], _stack(o["t_inv"], half0))
                else:
                    m = _dot(o["a_pow"], jnp.concatenate([_stack(o["a_pow"], half0), _stack(o["t_inv"], half0)], axis=1))
                    o["t_inv"] = o["t_inv"] + m[:, LANES:]
                    o["a_pow"] = m[:, :LANES]
            fill()

    for blk in blocks:
        o = ops[blk]
        gakv = _dot(o["g_ak"], o["v_s"])
        z = _dot(o["t_inv"], jnp.concatenate([_stack(o["at"], half0), _stack(gakv, half0)], axis=1))
        o["a_hat"], o["p0"] = z[:, :LANES], z[:, LANES:]
    fill()

    n_seg = n_rows // seg
    n_states = n_seg // segs_per_state
    per_blk = cm // seg
    p_parts = {blk: [None] * per_blk for blk in blocks}
    y_parts = {blk: [None] * per_blk for blk in blocks}
    for step in range(segs_per_state):
        segs = [st * segs_per_state + step for st in range(n_states)]
        proj = {}
        for g_i in segs:
            rb, off = (g_i * seg) // cm, (g_i * seg) % cm
            for p in range(N_PAIRS):
                o = ops[rb, p]
                lhs = jnp.concatenate([o["a_hat"][off:off + seg], o["rt"][off:off + seg]], axis=0)
                proj[g_i, p] = _dot_nt(lhs, s_scr[(g_i // segs_per_state) * N_PAIRS + p])
        for g_i in segs:
            rb, off = (g_i * seg) // cm, (g_i * seg) % cm
            for p in range(N_PAIRS):
                o = ops[rb, p]
                p_seg = proj[g_i, p][:seg] + o["p0"][off:off + seg]
                p_parts[rb, p][off // seg] = p_seg
                y_parts[rb, p][off // seg] = proj[g_i, p][seg:]
                upd = _dot_tn(jnp.concatenate([p_seg, o["v"][off:off + seg]], axis=0),
                              jnp.concatenate([o["bt"][off:off + seg], o["kt"][off:off + seg]], axis=0))
                si = (g_i // segs_per_state) * N_PAIRS + p
                row_end = g_i * seg + seg - 1
                w_end = e_cum[row_end:row_end + 1, p * LANES:(p + 1) * LANES]
                s_scr[si] = w_end * (s_scr[si] + jnp.where(state_mask, upd, 0.0))

    cat = lambda parts: parts[0] if len(parts) == 1 else jnp.concatenate(parts, axis=0)
    rows = []
    for rb in range(n_blk):
        tiles = []
        for p in range(N_PAIRS):
            o = ops[rb, p]
            pv_s = jnp.concatenate([_stack(cat(p_parts[rb, p]), half0), o["v_s"]], axis=0)
            tiles.append(cat(y_parts[rb, p]) + jnp.dot(o["g_r"], pv_s, preferred_element_type=F32))
        rows.append(jnp.concatenate(tiles, axis=1))
    return cat(rows)


def _time_mix(f, prev_rows, tb, seg, cm, vec_refs, s_scr, fill=lambda: None):
    mu_ref, w0_ref, a0_ref, lora_ref, kk_ref, ka_ref, rk_ref, lnw_ref, lnb_ref = vec_refs
    n_rows = f.shape[0]
    row = lax.broadcasted_iota(jnp.int32, (n_rows, 1), 0)
    f_prev = pltpu.roll(f, 1, 0)
    for b, prev in enumerate(prev_rows):
        f_prev = jnp.where(row == b * tb, prev, f_prev)
    fs = f + (f_prev - f) * mu_ref[...]
    r = fs[:, 0:D_RWKV]
    k = fs[:, D_RWKV:2 * D_RWKV]
    v = fs[:, 2 * D_RWKV:3 * D_RWKV]
    wa = fs[:, 3 * D_RWKV:D_SHIFT]
    lane = lax.broadcasted_iota(jnp.int32, (1, LANES), 1)
    lora = _dot(jnp.where(lane < D_LORA, jnp.tanh(wa), wa), lora_ref[...])
    lw = (-math.exp(-0.5)) * jax.nn.sigmoid(w0_ref[...] + lora[:, 0:D_RWKV])
    a_sig = jax.nn.sigmoid(a0_ref[...] + lora[:, D_RWKV:2 * D_RWKV])
    ones = _head_ones()
    kk = k * kk_ref[...]
    kk = kk * lax.rsqrt(jnp.maximum(_head_sum(kk * kk, ones), 1e-24))
    k2 = k * (1.0 + (a_sig - 1.0) * ka_ref[...])

    ti = lax.broadcasted_iota(jnp.int32, (n_rows, n_rows), 0)
    tj = lax.broadcasted_iota(jnp.int32, (n_rows, n_rows), 1)
    tri = ((ti // seg == tj // seg) & (tj <= ti)).astype(BF16)
    cum = _split3_dot_left(tri, lw)
    e_cum = jnp.exp(cum)
    e_inv = jnp.exp(-cum)
    y = _rwkv_recurrence(-kk * jnp.exp(cum - lw), r * e_cum, kk * a_sig * e_inv, k2 * e_inv, v, e_cum,
                         s_scr, cm, seg, tb // seg, fill)

    inv_n = 1.0 / HEAD_DIM
    yc = y - _head_sum(y, ones) * inv_n
    var = _head_sum(yc * yc, ones) * inv_n
    yn = yc * lax.rsqrt(var + LNX_EPS) * lnw_ref[...] + lnb_ref[...]
    bonus = _head_sum(r * k2 * rk_ref[...], ones) * v
    return yn + bonus


def _load_states(s0_ref, s_scr, bb):
    zero = jnp.zeros((HEAD_DIM, HEAD_DIM), F32)
    for b in range(bb):
        for p in range(N_PAIRS):
            top = jnp.concatenate([s0_ref[b, 2 * p], zero], axis=1)
            bot = jnp.concatenate([zero, s0_ref[b, 2 * p + 1]], axis=1)
            s_scr[b * N_PAIRS + p] = jnp.concatenate([top, bot], axis=0)


def _store_states(s_scr, s_out_ref, bb):
    for b in range(bb):
        for p in range(N_PAIRS):
            s = s_scr[b * N_PAIRS + p]
            s_out_ref[b, 2 * p] = s[:HEAD_DIM, :HEAD_DIM]
            s_out_ref[b, 2 * p + 1] = s[HEAD_DIM:, HEAD_DIM:]


def _rwkv_sample_kernel(bb, tb, f_ref, prev0_ref, *rest):
    vec_refs, (s0_ref, o_ref, s_out_ref, s_scr) = rest[:9], rest[9:]
    _load_states(s0_ref, s_scr, bb)
    o_ref[...] = _time_mix(f_ref[...], [prev0_ref[b] for b in range(bb)], tb, tb, HEAD_DIM, vec_refs, s_scr)
    _store_states(s_scr, s_out_ref, bb)


def _vec_specs(index_map):
    vec = lambda n: pl.BlockSpec((1, n), index_map)
    return [vec(D_SHIFT), vec(D_RWKV), vec(D_RWKV), pl.BlockSpec((LANES, 2 * D_RWKV), index_map),
            vec(D_RWKV), vec(D_RWKV), vec(D_RWKV), vec(D_RWKV), vec(D_RWKV)]


def _rwkv_sample(f, prev0, s0, vecs, batch, bb, tb):
    n_rows = bb * tb
    assert n_rows % HEAD_DIM == 0 and HEAD_DIM % tb == 0
    state_spec = pl.BlockSpec((bb, 2 * N_PAIRS, HEAD_DIM, HEAD_DIM), lambda b: (b, 0, 0, 0))
    return pl.pallas_call(
        functools.partial(_rwkv_sample_kernel, bb, tb),
        grid=(batch // bb,),
        in_specs=[pl.BlockSpec((n_rows, D_SHIFT), lambda b: (b, 0)),
                  pl.BlockSpec((bb, 1, D_SHIFT), lambda b: (b, 0, 0))] + _vec_specs(lambda b: (0, 0)) + [state_spec],
        out_specs=[pl.BlockSpec((n_rows, D_RWKV), lambda b: (b, 0)), state_spec],
        out_shape=[jax.ShapeDtypeStruct((batch * tb, D_RWKV), F32),
                   jax.ShapeDtypeStruct((batch, 2 * N_PAIRS, HEAD_DIM, HEAD_DIM), F32)],
        scratch_shapes=[pltpu.VMEM((bb * N_PAIRS, LANES, LANES), F32)],
        compiler_params=pltpu.CompilerParams(dimension_semantics=("arbitrary",), vmem_limit_bytes=VMEM_LIMIT),
        name="rwkv_sample",
    )(f, prev0, *vecs, s0)


def _qk_norm(q, k, qw, kw, ones):
    inv_n = 1.0 / HEAD_DIM
    qn = q * lax.rsqrt(_head_sum(q * q, ones) * inv_n + NORM_EPS) * (qw * (HEAD_DIM ** -0.5))
    kn = k * lax.rsqrt(_head_sum(k * k, ones) * inv_n + NORM_EPS) * kw
    return qn, kn


def _attend(qn, score_blocks, sinks_ref, half0):
    heads = range(N_Q_HEADS)
    aligned = [(h // Q_PER_KV) == (h % 2) for h in heads]
    qm = [jnp.where(half0 if h % 2 == 0 else jnp.logical_not(half0),
                    qn[:, (h // 2) * LANES:(h // 2 + 1) * LANES], 0.0).astype(BF16) for h in heads]
    sink = [sinks_ref[h:h + 1, 0:1] for h in heads]
    scores = [[jnp.where(valid, _dot_nt(qm[h], keys if aligned[h] else keys_rot), NEG_INF)
               for keys, keys_rot, _, _, valid in score_blocks] for h in heads]
    m = []
    for h in heads:
        mh = sink[h]
        for s in scores[h]:
            mh = jnp.maximum(mh, jnp.max(s, axis=-1, keepdims=True))
        m.append(mh)
    probs = [[jnp.exp(s - m[h]) for s in scores[h]] for h in heads]
    denom = []
    for h in heads:
        d = jnp.exp(sink[h] - m[h])
        for pr in probs[h]:
            d = d + jnp.sum(pr, axis=-1, keepdims=True)
        denom.append(d)
    outs = []
    for h in heads:
        acc = None
        for pr, (_, _, vals, vals_rot, _) in zip(probs[h], score_blocks):
            pv = _dot(pr, vals if aligned[h] else vals_rot)
            acc = pv if acc is None else acc + pv
        outs.append(acc * (1.0 / denom[h]))
    return jnp.concatenate([jnp.where(half0, outs[2 * t], outs[2 * t + 1]) for t in range(D_ATTN // LANES)], axis=1)


def _swa_prompt_steps(q, k, v, qw, kw, sinks_ref, kprev, kprev_rot, vprev_t, has_prev, out):
    ones = _head_ones()
    half0 = lax.broadcasted_iota(jnp.int32, (1, LANES), 1) < HEAD_DIM
    qn, kn = _qk_norm(q, k, qw, kw, ones)
    kn_b = kn.astype(BF16)
    kn_rot = pltpu.roll(kn, HEAD_DIM, 1).astype(BF16)
    v_t = v.T.astype(BF16)
    keys = jnp.concatenate([kprev[...], kn_b], axis=0)
    keys_rot = jnp.concatenate([kprev_rot[...], kn_rot], axis=0)
    vals_t = jnp.concatenate([vprev_t[...], v_t], axis=1)
    kprev[...] = kn_b
    kprev_rot[...] = kn_rot
    vprev_t[...] = v_t
    kj = lax.broadcasted_iota(jnp.int32, (2 * WINDOW, WINDOW), 0)
    qi = lax.broadcasted_iota(jnp.int32, (2 * WINDOW, WINDOW), 1)
    valid_t = (kj > qi) & (kj <= qi + WINDOW) & ((kj >= WINDOW) | has_prev)
    heads = range(N_Q_HEADS)
    yield
    scores = []
    for h in heads:
        e, g = h % 2, h // Q_PER_KV
        qm = jnp.where(half0 if e == 0 else jnp.logical_not(half0), qn[:, (h // 2) * LANES:(h // 2 + 1) * LANES], 0.0)
        scores.append(jnp.where(valid_t, _dot_nt(keys if g == e else keys_rot, qm.astype(BF16)), NEG_INF))
    yield
    probs = []
    for h in heads:
        sink = sinks_ref[h:h + 1, 0:1]
        m = jnp.maximum(jnp.max(scores[h], axis=0, keepdims=True), sink)
        pr = jnp.exp(scores[h] - m)
        denom = jnp.sum(pr, axis=0, keepdims=True) + jnp.exp(sink - m)
        probs.append((pr * (1.0 / denom)).astype(BF16))
    yield
    outs = [jnp.dot(vals_t[(h // Q_PER_KV) * HEAD_DIM:(h // Q_PER_KV + 1) * HEAD_DIM, :], probs[h],
                    preferred_element_type=F32) for h in heads]
    out["o_a"] = jnp.concatenate([jnp.concatenate(outs[2 * t:2 * t + 2], axis=0).T for t in range(D_ATTN // LANES)],
                                 axis=1)
    out["kn"] = kn


def _attn_sample_kernel(t_new, q_ref, k_ref, v_ref, ck_ref, cv_ref, qw_ref, kw_ref, sinks_ref,
                        o_ref, ko_ref, vo_ref):
    ones = _head_ones()
    half0 = lax.broadcasted_iota(jnp.int32, (1, LANES), 1) < HEAD_DIM
    wb = ck_ref.shape[1]
    qi_c = lax.broadcasted_iota(jnp.int32, (t_new, wb), 0)
    kj_c = lax.broadcasted_iota(jnp.int32, (t_new, wb), 1)
    valid_c = (qi_c + wb - kj_c) < WINDOW
    qi_n = lax.broadcasted_iota(jnp.int32, (t_new, t_new), 0)
    kj_n = lax.broadcasted_iota(jnp.int32, (t_new, t_new), 1)
    valid_n = kj_n <= qi_n
    rot = lambda x: pltpu.roll(x, HEAD_DIM, 1).astype(BF16)
    for i in range(q_ref.shape[0]):
        v = v_ref[i]
        qn, kn = _qk_norm(q_ref[i], k_ref[i], qw_ref[...], kw_ref[...], ones)
        ck = ck_ref[i]
        cv = cv_ref[i]
        blocks = [(ck.astype(BF16), rot(ck), cv.astype(BF16), rot(cv), valid_c),
                  (kn.astype(BF16), rot(kn), v.astype(BF16), rot(v), valid_n)]
        o_ref[i] = _attend(qn, blocks, sinks_ref, half0)
        ko_ref[i] = jnp.concatenate([ck[t_new:], kn], axis=0)
        vo_ref[i] = jnp.concatenate([cv[t_new:], v], axis=0)


def _attn_sample(q, k, v, ck, cv, qw, kw, sinks_b):
    b, t_new, _ = q.shape
    wb = ck.shape[1]
    gb = SAMPLE_GROUP
    spec = lambda r, w: pl.BlockSpec((gb, r, w), lambda i: (i, 0, 0))
    const = lambda r, w: pl.BlockSpec((r, w), lambda i: (0, 0))
    return pl.pallas_call(
        functools.partial(_attn_sample_kernel, t_new),
        grid=(b // gb,),
        in_specs=[spec(t_new, D_ATTN), spec(t_new, D_KV), spec(t_new, D_KV), spec(wb, D_KV), spec(wb, D_KV),
                  const(1, D_ATTN), const(1, D_KV), const(N_Q_HEADS, LANES)],
        out_specs=[spec(t_new, D_ATTN), spec(wb, D_KV), spec(wb, D_KV)],
        out_shape=[jax.ShapeDtypeStruct((b, t_new, D_ATTN), F32),
                   jax.ShapeDtypeStruct((b, wb, D_KV), F32),
                   jax.ShapeDtypeStruct((b, wb, D_KV), F32)],
        compiler_params=pltpu.CompilerParams(dimension_semantics=("arbitrary",)),
        name="attn_sample",
    )(q, k, v, ck, cv, qw, kw, sinks_b)


def _merge_rows(x, o_r, z_r, o_a, z_a, p, wout_ref, g_ref, wgate_ref, wproj_ref, fill=lambda: None):
    gr = o_r * jax.nn.silu(z_r)
    ga = o_a * jax.nn.silu(z_a)
    ple = _dot(p, wproj_ref[...])
    h = x + _dot(gr, wout_ref[0:D_RWKV, :]) + _dot(ga, wout_ref[D_RWKV:D_MODEL, :])
    fill()
    gate = jax.nn.sigmoid(_dot(_rms_rows(h, g_ref[...]), wgate_ref[...]))
    fill()
    return h + gate * ple


def _merge_kernel(x_ref, or_ref, zr_ref, oa_ref, za_ref, p_ref, wout_ref, g_ref, wgate_ref, wproj_ref, y_ref):
    y_ref[...] = _merge_rows(x_ref[...], or_ref[...], zr_ref[...], oa_ref[...], za_ref[...], p_ref[...],
                             wout_ref, g_ref, wgate_ref, wproj_ref)


def _merge(x, o_r, z_r, o_a, z_a, p, w_out, g_ple, w_gate, w_proj, tm):
    m = x.shape[0]
    tok = lambda w: pl.BlockSpec((tm, w), lambda i: (i, 0))
    const = lambda r, w: pl.BlockSpec((r, w), lambda i: (0, 0))
    return pl.pallas_call(
        _merge_kernel,
        grid=(m // tm,),
        in_specs=[tok(D_MODEL), tok(D_RWKV), tok(D_RWKV), tok(D_ATTN), tok(D_ATTN), tok(D_PLE),
                  const(D_MODEL, D_MODEL), const(1, D_MODEL), const(D_MODEL, D_MODEL), const(D_PLE, D_MODEL)],
        out_specs=tok(D_MODEL),
        out_shape=jax.ShapeDtypeStruct((m, D_MODEL), F32),
        compiler_params=pltpu.CompilerParams(dimension_semantics=("arbitrary",), vmem_limit_bytes=VMEM_LIMIT),
        name="merge",
    )(x, o_r, z_r, o_a, z_a, p, w_out, g_ple, w_gate, w_proj)


_IN_CHUNK = 512
PROMPT_ROWS = 2
_FILL_HEAD = 1
_FILL_TAIL = 2


def _fill_order(n_proj, n_attn):
    mid = n_proj - _FILL_HEAD - _FILL_TAIL
    order = ["p"] * _FILL_HEAD
    done = 0
    for i in range(n_attn):
        while done < mid and done * n_attn <= i * mid:
            order.append("p")
            done += 1
        order.append("a")
    return order + ["p"] * (mid - done + _FILL_TAIL)


def _prompt_layer_kernel(x_ref, p_ref, gn_ref, win_ref, *rest):
    vec_refs = rest[:9]
    (qw_ref, kw_ref, sinks_ref, wout_ref, gp_ref, wgate_ref, wproj_ref,
     y_ref, s_out_ref, shift_ref, kc_ref, vc_ref,
     hbuf, xbuf, s_scr, prev_scr, kprev, kprev_rot, vprev_t) = rest[9:]
    bb = x_ref.shape[0]
    n_rows = bb * WINDOW
    j = pl.program_id(1)
    slot = j % 2

    @pl.when(j == 0)
    def _():
        hbuf[1] = jnp.zeros(hbuf.shape[1:], F32)
        xbuf[1] = jnp.zeros(xbuf.shape[1:], F32)
        s_scr[...] = jnp.zeros(s_scr.shape, F32)
        prev_scr[...] = jnp.zeros(prev_scr.shape, F32)
        kprev[...] = jnp.zeros(kprev.shape, BF16)
        kprev_rot[...] = jnp.zeros(kprev_rot.shape, BF16)
        vprev_t[...] = jnp.zeros(vprev_t.shape, BF16)

    h_cur = hbuf.at[1 - slot]
    x_new = x_ref[...].reshape(n_rows, D_MODEL)
    xbuf[slot] = x_new
    xn = _rms_rows(x_new, gn_ref[...]).astype(BF16)
    h_new = hbuf.at[slot]

    def project(c0):
        c1 = min(c0 + _IN_CHUNK, D_IN)
        h_new[:, c0:c1] = jnp.dot(xn, win_ref[:, c0:c1], preferred_element_type=F32)

    offs = [0]
    for w in _IN_SPLITS:
        offs.append(offs[-1] + w)
    part = lambda i: h_cur[:, offs[i]:offs[i + 1]]
    q, k, v = part(2), part(3), part(4)
    attn = [{} for _ in range(bb)]
    attn_steps = []
    for b in range(bb):
        rows = slice(b * WINDOW, (b + 1) * WINDOW)
        attn_steps.append(_swa_prompt_steps(q[rows], k[rows], v[rows], qw_ref[...], kw_ref[...], sinks_ref,
                                            kprev.at[b], kprev_rot.at[b], vprev_t.at[b], j > 1, attn[b]))
    proj_items = [functools.partial(project, c0) for c0 in range(0, D_IN, _IN_CHUNK)]
    attn_items = [functools.partial(next, attn_steps[b], None) for _ in range(4) for b in range(bb)]
    queue = [proj_items.pop(0) if kind == "p" else attn_items.pop(0)
             for kind in _fill_order(len(proj_items), len(attn_items))]
    assert not proj_items and not attn_items

    def fill():
        if queue:
            queue.pop(0)()

    for _ in range(_FILL_HEAD):
        fill()
    f = part(0)
    o_r = _time_mix(f, [prev_scr[b] for b in range(bb)], WINDOW, PROMPT_CHUNK, HEAD_DIM, vec_refs, s_scr, fill)
    for b in range(bb):
        prev_scr[b] = f[(b + 1) * WINDOW - 1:(b + 1) * WINDOW, :]
    while len(queue) > _FILL_TAIL:
        fill()
    o_a = jnp.concatenate([attn[b]["o_a"] for b in range(bb)], axis=0)
    y = _merge_rows(xbuf[1 - slot], o_r, part(1), o_a, part(5), p_ref[...].reshape(n_rows, D_PLE), wout_ref, gp_ref,
                    wgate_ref, wproj_ref, fill)
    y_ref[...] = y.reshape(y_ref.shape)
    while queue:
        fill()

    @pl.when(j == pl.num_programs(1) - 1)
    def _():
        _store_states(s_scr, s_out_ref, bb)
        shift_ref[...] = prev_scr[...]
        for b in range(bb):
            kc_ref[b] = attn[b]["kn"]
            vc_ref[b] = v[b * WINDOW:(b + 1) * WINDOW]


def _prompt_layer(x, p, gn, w_in_b, vecs, qw, kw, sinks_b, w_out_b, gp, w_gate_b, w_proj_b):
    batch, seq, _ = x.shape
    bb = PROMPT_ROWS
    nblk = seq // WINDOW
    const = lambda r, w: pl.BlockSpec((r, w), lambda g, j: (0, 0))
    per_row = lambda *shape: pl.BlockSpec((bb,) + shape, lambda g, j: (g,) + (0,) * len(shape))
    in_specs = ([pl.BlockSpec((bb, WINDOW, D_MODEL), lambda g, j: (g, jnp.minimum(j, nblk - 1), 0)),
                 pl.BlockSpec((bb, WINDOW, D_PLE), lambda g, j: (g, jnp.maximum(j - 1, 0), 0)),
                 const(1, D_MODEL), const(D_MODEL, D_IN)]
                + _vec_specs(lambda g, j: (0, 0))
                + [const(1, D_ATTN), const(1, D_KV), const(N_Q_HEADS, LANES), const(D_MODEL, D_MODEL),
                   const(1, D_MODEL), const(D_MODEL, D_MODEL), const(D_PLE, D_MODEL)])
    return pl.pallas_call(
        _prompt_layer_kernel,
        grid=(batch // bb, nblk + 1),
        in_specs=in_specs,
        out_specs=[pl.BlockSpec((bb, WINDOW, D_MODEL), lambda g, j: (g, jnp.maximum(j - 1, 0), 0)),
                   per_row(2 * N_PAIRS, HEAD_DIM, HEAD_DIM), per_row(1, D_SHIFT),
                   per_row(WINDOW, D_KV), per_row(WINDOW, D_KV)],
        out_shape=[jax.ShapeDtypeStruct((batch, seq, D_MODEL), F32),
                   jax.ShapeDtypeStruct((batch, 2 * N_PAIRS, HEAD_DIM, HEAD_DIM), F32),
                   jax.ShapeDtypeStruct((batch, 1, D_SHIFT), F32),
                   jax.ShapeDtypeStruct((batch, WINDOW, D_KV), F32),
                   jax.ShapeDtypeStruct((batch, WINDOW, D_KV), F32)],
        scratch_shapes=[pltpu.VMEM((2, bb * WINDOW, D_IN), F32), pltpu.VMEM((2, bb * WINDOW, D_MODEL), F32),
                        pltpu.VMEM((bb * N_PAIRS, LANES, LANES), F32), pltpu.VMEM((bb, 1, D_SHIFT), F32),
                        pltpu.VMEM((bb, WINDOW, D_KV), BF16), pltpu.VMEM((bb, WINDOW, D_KV), BF16),
                        pltpu.VMEM((bb, D_KV, WINDOW), BF16)],
        compiler_params=pltpu.CompilerParams(dimension_semantics=("arbitrary", "arbitrary"),
                                             vmem_limit_bytes=VMEM_LIMIT),
        name="prompt_layer",
    )(x, p, gn, w_in_b, *vecs, qw, kw, sinks_b, w_out_b, gp, w_gate_b, w_proj_b)


def kernel(x_prompt, x_sample, state_rwkv, state_shift, cache_k, cache_v, p_prompt, p_sample, g_norm, w_in, mu_shift, w0, w_dec2, a0, w_a2, k_k, k_a, r_k, lnx_w, lnx_b, q_norm_w, k_norm_w, sinks, w_out, g_ple, w_ple_gate, w_ple_proj):
    depth = w_in.shape[0]
    bp, seq, _ = x_prompt.shape
    bs, dec, _ = x_sample.shape
    wb = cache_k.shape[2]
    xp = x_prompt
    xs = x_sample.reshape(bs * dec, D_MODEL)
    outs = [[] for _ in range(8)]
    for i in range(depth):
        w_in_b = w_in[i].astype(BF16)
        w_out_b = w_out[i].astype(BF16)
        w_gate_b = w_ple_gate[i].astype(BF16)
        w_proj_b = w_ple_proj[i].astype(BF16)
        zl = jnp.zeros((D_LORA, D_RWKV), F32)
        lora_w = jnp.concatenate([jnp.concatenate([w_dec2[i], zl], axis=1),
                                  jnp.concatenate([zl, w_a2[i]], axis=1)], axis=0).astype(BF16)
        row = lambda t: t.reshape(1, -1)
        vecs = (row(mu_shift[i]), row(w0[i]), row(a0[i]), lora_w, row(k_k[i]), row(k_a[i]), row(r_k[i]),
                row(lnx_w[i]), row(lnx_b[i]))
        qw = row(jnp.tile(q_norm_w[i], N_Q_HEADS))
        kw = row(jnp.tile(k_norm_w[i], N_KV_HEADS))
        sinks_b = jnp.broadcast_to(sinks[i][:, None], (N_Q_HEADS, LANES))
        gn, gp = row(g_norm[i]), row(g_ple[i])

        xp, s_p, sh_p, kc, vc = _prompt_layer(xp, p_prompt[i], gn, w_in_b, vecs, qw, kw, sinks_b, w_out_b, gp,
                                              w_gate_b, w_proj_b)
        outs[0].append(s_p)
        outs[2].append(sh_p)
        outs[4].append(kc.reshape(bp, WINDOW, N_KV_HEADS, HEAD_DIM))
        outs[6].append(vc.reshape(bp, WINDOW, N_KV_HEADS, HEAD_DIM))

        f, z_r, q, k, v, z_a = _in_proj(xs, gn, w_in_b, 512)
        o_r, s_s = _rwkv_sample(f, state_shift[i], state_rwkv[i], vecs, bs, SAMPLE_GROUP, dec)
        o_a, k_buf, v_buf = _attn_sample(q.reshape(bs, dec, D_ATTN), k.reshape(bs, dec, D_KV), v.reshape(bs, dec, D_KV),
                                         cache_k[i].reshape(bs, wb, D_KV), cache_v[i].reshape(bs, wb, D_KV),
                                         qw, kw, sinks_b)
        outs[1].append(s_s)
        outs[3].append(f.reshape(bs, dec, D_SHIFT)[:, -1:])
        outs[5].append(k_buf.reshape(bs, wb, N_KV_HEADS, HEAD_DIM))
        outs[7].append(v_buf.reshape(bs, wb, N_KV_HEADS, HEAD_DIM))
        xs = _merge(xs, o_r, z_r, o_a.reshape(bs * dec, D_ATTN), z_a, p_sample[i].reshape(bs * dec, D_PLE),
                    w_out_b, gp, w_gate_b, w_proj_b, 512)
    st = lambda l: jnp.stack(l)
    return (xp, xs.reshape(bs, dec, D_MODEL),
            st(outs[0]), st(outs[1]), st(outs[2]), st(outs[3]), st(outs[4]), st(outs[5]), st(outs[6]), st(outs[7]))
```

```python
import functools
import math

import jax
import jax.numpy as jnp
from jax import lax
from jax.experimental import pallas as pl
from jax.experimental.pallas import tpu as pltpu

F32 = jnp.float32
BF16 = jnp.bfloat16

D_MODEL = 1024
HEAD_DIM = 64
D_RWKV = 512
D_ATTN = 512
N_KV_HEADS = 2
N_Q_HEADS = 8
Q_PER_KV = N_Q_HEADS // N_KV_HEADS
D_KV = N_KV_HEADS * HEAD_DIM
WINDOW = 128
D_LORA = 64
D_SHIFT = 3 * D_RWKV + 2 * D_LORA
D_PLE = 256
D_IN = D_SHIFT + D_RWKV + D_ATTN + 2 * D_KV + D_ATTN
NORM_EPS = 1e-6
LNX_EPS = 64e-5
NEG_INF = -1e30

LANES = 128
N_PAIRS = D_RWKV // LANES
PROMPT_CHUNK = 64
SAMPLE_GROUP = 16
VMEM_LIMIT = 56 * 1024 * 1024


def _dot(a, b):
    return jnp.dot(a.astype(BF16), b.astype(BF16), preferred_element_type=F32)


def _dot_nt(a, b):
    return lax.dot_general(a.astype(BF16), b.astype(BF16), (((1,), (1,)), ((), ())), preferred_element_type=F32)


def _dot_tn(a, b):
    return lax.dot_general(a.astype(BF16), b.astype(BF16), (((0,), (0,)), ((), ())), preferred_element_type=F32)


def _segment_cumsum(x, seg):
    blk = min(max(seg, HEAD_DIM), x.shape[0])
    ti = lax.broadcasted_iota(jnp.int32, (blk, 3 * blk), 0)
    tj = lax.broadcasted_iota(jnp.int32, (blk, 3 * blk), 1) % blk
    tri3 = ((ti // seg == tj // seg) & (tj <= ti)).astype(BF16)
    hi = x.astype(BF16)
    r1 = x - hi.astype(F32)
    mid = r1.astype(BF16)
    lo = (r1 - mid.astype(F32)).astype(BF16)
    parts = []
    for r0 in range(0, x.shape[0], blk):
        rows = slice(r0, r0 + blk)
        parts.append(jnp.dot(tri3, jnp.concatenate([hi[rows], mid[rows], lo[rows]], axis=0),
                             preferred_element_type=F32))
    return parts[0] if len(parts) == 1 else jnp.concatenate(parts, axis=0)


def _head_ones():
    r = (lax.broadcasted_iota(jnp.int32, (2 * LANES, LANES), 0) % LANES) // HEAD_DIM
    c = lax.broadcasted_iota(jnp.int32, (2 * LANES, LANES), 1) // HEAD_DIM
    return (r == c).astype(BF16)


def _head_sum(x, ones):
    tiles = []
    for i in range(0, x.shape[1], LANES):
        xt = x[:, i:i + LANES]
        hi = xt.astype(BF16)
        lo = (xt - hi.astype(F32)).astype(BF16)
        tiles.append(jnp.dot(jnp.concatenate([hi, lo], axis=1), ones, preferred_element_type=F32))
    return tiles[0] if len(tiles) == 1 else jnp.concatenate(tiles, axis=1)


def _rms_rows(x, g):
    return x * lax.rsqrt(jnp.mean(x * x, axis=-1, keepdims=True) + NORM_EPS) * g


_IN_SPLITS = (D_SHIFT, D_RWKV, D_ATTN, D_KV, D_KV, D_ATTN)


def _in_proj_kernel(x_ref, g_ref, w_ref, *out_refs):
    h = _dot(_rms_rows(x_ref[...], g_ref[...]), w_ref[...])
    off = 0
    for o_ref, width in zip(out_refs, _IN_SPLITS):
        o_ref[...] = h[:, off:off + width]
        off += width


def _in_proj(x, g_norm, w_in_bf16, tm):
    m = x.shape[0]
    return pl.pallas_call(
        _in_proj_kernel,
        grid=(m // tm,),
        in_specs=[pl.BlockSpec((tm, D_MODEL), lambda i: (i, 0)),
                  pl.BlockSpec((1, D_MODEL), lambda i: (0, 0)),
                  pl.BlockSpec((D_MODEL, D_IN), lambda i: (0, 0))],
        out_specs=[pl.BlockSpec((tm, w), lambda i: (i, 0)) for w in _IN_SPLITS],
        out_shape=[jax.ShapeDtypeStruct((m, w), F32) for w in _IN_SPLITS],
        compiler_params=pltpu.CompilerParams(dimension_semantics=("arbitrary",), vmem_limit_bytes=VMEM_LIMIT),
        name="in_proj",
    )(x, g_norm, w_in_bf16)


def _stack(z, half0):
    return jnp.concatenate([jnp.where(half0, z, 0.0), jnp.where(half0, 0.0, z)], axis=0).astype(BF16)


def _rwkv_recurrence(at, rt, bt, kt, v, e_cum, s_scr, cm, seg, segs_per_state, fill):
    n_rows = at.shape[0]
    n_blk = n_rows // cm
    lane = lax.broadcasted_iota(jnp.int32, (1, LANES), 1)
    half0 = lane < HEAD_DIM
    ri = lax.broadcasted_iota(jnp.int32, (cm, LANES), 0)
    ci = lax.broadcasted_iota(jnp.int32, (cm, LANES), 1) % cm
    same = (ri // seg) == (ci // seg)
    tri_strict = same & (ci < ri)
    tri_incl = same & (ci <= ri)
    eye_c = (ri == ci).astype(F32)
    sr = lax.broadcasted_iota(jnp.int32, (LANES, LANES), 0) // HEAD_DIM
    sc = lax.broadcasted_iota(jnp.int32, (LANES, LANES), 1) // HEAD_DIM
    state_mask = sr == sc
    n_levels = max(int(math.log2(seg)) - 1, 0)
    blocks = [(rb, p) for rb in range(n_blk) for p in range(N_PAIRS)]

    def tile(x, rb, p):
        return x[rb * cm:(rb + 1) * cm, p * LANES:(p + 1) * LANES]

    ops = {}
    for rb, p in blocks:
        at_p, rt_p, bt_p, kt_p, v_p = (tile(x, rb, p) for x in (at, rt, bt, kt, v))
        ops[rb, p] = dict(at=at_p, rt=rt_p, bt=bt_p, kt=kt_p, v=v_p, v_s=_stack(v_p, half0))
    for blk in blocks:
        o = ops[blk]
        g = _dot_nt(jnp.concatenate([o["at"], o["rt"]], axis=0),
                    jnp.concatenate([_stack(o["bt"], half0), _stack(o["kt"], half0)], axis=0))
        o["g_ab"] = jnp.where(tri_strict, g[:cm, :LANES], 0.0)
        o["g_ak"] = jnp.where(tri_strict, g[:cm, LANES:], 0.0)
        o["g_r"] = jnp.concatenate([jnp.where(tri_incl, g[cm:, :LANES], 0.0),
                                    jnp.where(tri_incl, g[cm:, LANES:], 0.0)], axis=1).astype(BF16)
    fill()

    for blk in blocks:
        ops[blk]["t_inv"] = eye_c + ops[blk]["g_ab"]
    if n_levels > 0:
        for blk in blocks:
            ops[blk]["a_pow"] = _dot(ops[blk]["g_ab"], _stack(ops[blk]["g_ab"], half0))
        fill()
        for lvl in range(n_levels):
            last = lvl == n_levels - 1
            for blk in blocks:
                o = ops[blk]
                if last:
                    o["t_inv"] = o["t_inv"] + _dot(o["a_pow"], _stack(o["t_inv"], half0))
                else:
                    m = _dot(o["a_pow"], jnp.concatenate([_stack(o["a_pow"], half0), _stack(o["t_inv"], half0)], axis=1))
                    o["t_inv"] = o["t_inv"] + m[:, LANES:]
                    o["a_pow"] = m[:, :LANES]
            fill()

    for blk in blocks:
        o = ops[blk]
        gakv = _dot(o["g_ak"], o["v_s"])
        z = _dot(o["t_inv"], jnp.concatenate([_stack(o["at"], half0), _stack(gakv, half0)], axis=1))
        o["a_hat"], o["p0"] = z[:, :LANES], z[:, LANES:]
    fill()

    n_seg = n_rows // seg
    n_states = n_seg // segs_per_state
    per_blk = cm // seg
    p_parts = {blk: [None] * per_blk for blk in blocks}
    y_parts = {blk: [None] * per_blk for blk in blocks}
    for step in range(segs_per_state):
        segs = [st * segs_per_state + step for st in range(n_states)]
        proj = {}
        for g_i in segs:
            rb, off = (g_i * seg) // cm, (g_i * seg) % cm
            for p in range(N_PAIRS):
                o = ops[rb, p]
                lhs = jnp.concatenate([o["a_hat"][off:off + seg], o["rt"][off:off + seg]], axis=0)
                proj[g_i, p] = _dot_nt(lhs, s_scr[(g_i // segs_per_state) * N_PAIRS + p])
        for g_i in segs:
            rb, off = (g_i * seg) // cm, (g_i * seg) % cm
            for p in range(N_PAIRS):
                o = ops[rb, p]
                p_seg = proj[g_i, p][:seg] + o["p0"][off:off + seg]
                p_parts[rb, p][off // seg] = p_seg
                y_parts[rb, p][off // seg] = proj[g_i, p][seg:]
                upd = _dot_tn(jnp.concatenate([p_seg, o["v"][off:off + seg]], axis=0),
                              jnp.concatenate([o["bt"][off:off + seg], o["kt"][off:off + seg]], axis=0))
                si = (g_i // segs_per_state) * N_PAIRS + p
                row_end = g_i * seg + seg - 1
                w_end = e_cum[row_end:row_end + 1, p * LANES:(p + 1) * LANES]
                s_scr[si] = w_end * (s_scr[si] + jnp.where(state_mask, upd, 0.0))

    cat = lambda parts: parts[0] if len(parts) == 1 else jnp.concatenate(parts, axis=0)
    rows = []
    for rb in range(n_blk):
        tiles = []
        for p in range(N_PAIRS):
            o = ops[rb, p]
            pv_s = jnp.concatenate([_stack(cat(p_parts[rb, p]), half0), o["v_s"]], axis=0)
            tiles.append(cat(y_parts[rb, p]) + jnp.dot(o["g_r"], pv_s, preferred_element_type=F32))
        rows.append(jnp.concatenate(tiles, axis=1))
    return cat(rows)


def _time_mix(f, prev_rows, tb, seg, cm, vec_refs, s_scr, fill=lambda: None):
    mu_ref, w0_ref, a0_ref, lora_ref, kk_ref, ka_ref, rk_ref, lnw_ref, lnb_ref = vec_refs
    n_rows = f.shape[0]
    row = lax.broadcasted_iota(jnp.int32, (n_rows, 1), 0)
    f_prev = pltpu.roll(f, 1, 0)
    for b, prev in enumerate(prev_rows):
        f_prev = jnp.where(row == b * tb, prev, f_prev)
    fs = f + (f_prev - f) * mu_ref[...]
    r = fs[:, 0:D_RWKV]
    k = fs[:, D_RWKV:2 * D_RWKV]
    v = fs[:, 2 * D_RWKV:3 * D_RWKV]
    wa = fs[:, 3 * D_RWKV:D_SHIFT]
    lane = lax.broadcasted_iota(jnp.int32, (1, LANES), 1)
    lora = _dot(jnp.where(lane < D_LORA, jnp.tanh(wa), wa), lora_ref[...])
    lw = (-math.exp(-0.5)) * jax.nn.sigmoid(w0_ref[...] + lora[:, 0:D_RWKV])
    a_sig = jax.nn.sigmoid(a0_ref[...] + lora[:, D_RWKV:2 * D_RWKV])
    ones = _head_ones()
    kk = k * kk_ref[...]
    kk = kk * lax.rsqrt(jnp.maximum(_head_sum(kk * kk, ones), 1e-24))
    k2 = k * (1.0 + (a_sig - 1.0) * ka_ref[...])

    cum = _segment_cumsum(lw, seg)
    e_cum = jnp.exp(cum)
    e_inv = jnp.exp(-cum)
    y = _rwkv_recurrence(-kk * jnp.exp(cum - lw), r * e_cum, kk * a_sig * e_inv, k2 * e_inv, v, e_cum,
                         s_scr, cm, seg, tb // seg, fill)

    inv_n = 1.0 / HEAD_DIM
    yc = y - _head_sum(y, ones) * inv_n
    var = _head_sum(yc * yc, ones) * inv_n
    yn = yc * lax.rsqrt(var + LNX_EPS) * lnw_ref[...] + lnb_ref[...]
    bonus = _head_sum(r * k2 * rk_ref[...], ones) * v
    return yn + bonus


def _load_states(s0_ref, s_scr, bb):
    zero = jnp.zeros((HEAD_DIM, HEAD_DIM), F32)
    for b in range(bb):
        for p in range(N_PAIRS):
            top = jnp.concatenate([s0_ref[b, 2 * p], zero], axis=1)
            bot = jnp.concatenate([zero, s0_ref[b, 2 * p + 1]], axis=1)
            s_scr[b * N_PAIRS + p] = jnp.concatenate([top, bot], axis=0)


def _store_states(s_scr, s_out_ref, bb):
    for b in range(bb):
        for p in range(N_PAIRS):
            s = s_scr[b * N_PAIRS + p]
            s_out_ref[b, 2 * p] = s[:HEAD_DIM, :HEAD_DIM]
            s_out_ref[b, 2 * p + 1] = s[HEAD_DIM:, HEAD_DIM:]


def _rwkv_sample_kernel(bb, tb, f_ref, prev0_ref, *rest):
    vec_refs, (s0_ref, o_ref, s_out_ref, s_scr) = rest[:9], rest[9:]
    _load_states(s0_ref, s_scr, bb)
    o_ref[...] = _time_mix(f_ref[...], [prev0_ref[b] for b in range(bb)], tb, tb, HEAD_DIM, vec_refs, s_scr)
    _store_states(s_scr, s_out_ref, bb)


def _vec_specs(index_map):
    vec = lambda n: pl.BlockSpec((1, n), index_map)
    return [vec(D_SHIFT), vec(D_RWKV), vec(D_RWKV), pl.BlockSpec((LANES, 2 * D_RWKV), index_map),
            vec(D_RWKV), vec(D_RWKV), vec(D_RWKV), vec(D_RWKV), vec(D_RWKV)]


def _rwkv_sample(f, prev0, s0, vecs, batch, bb, tb):
    n_rows = bb * tb
    assert n_rows % HEAD_DIM == 0 and HEAD_DIM % tb == 0
    state_spec = pl.BlockSpec((bb, 2 * N_PAIRS, HEAD_DIM, HEAD_DIM), lambda b: (b, 0, 0, 0))
    return pl.pallas_call(
        functools.partial(_rwkv_sample_kernel, bb, tb),
        grid=(batch // bb,),
        in_specs=[pl.BlockSpec((n_rows, D_SHIFT), lambda b: (b, 0)),
                  pl.BlockSpec((bb, 1, D_SHIFT), lambda b: (b, 0, 0))] + _vec_specs(lambda b: (0, 0)) + [state_spec],
        out_specs=[pl.BlockSpec((n_rows, D_RWKV), lambda b: (b, 0)), state_spec],
        out_shape=[jax.ShapeDtypeStruct((batch * tb, D_RWKV), F32),
                   jax.ShapeDtypeStruct((batch, 2 * N_PAIRS, HEAD_DIM, HEAD_DIM), F32)],
        scratch_shapes=[pltpu.VMEM((bb * N_PAIRS, LANES, LANES), F32)],
        compiler_params=pltpu.CompilerParams(dimension_semantics=("arbitrary",), vmem_limit_bytes=VMEM_LIMIT),
        name="rwkv_sample",
    )(f, prev0, *vecs, s0)


def _qk_norm(q, k, qw, kw, ones):
    inv_n = 1.0 / HEAD_DIM
    qn = q * lax.rsqrt(_head_sum(q * q, ones) * inv_n + NORM_EPS) * (qw * (HEAD_DIM ** -0.5))
    kn = k * lax.rsqrt(_head_sum(k * k, ones) * inv_n + NORM_EPS) * kw
    return qn, kn


def _swa_prompt_steps(q, k, v, qw, kw, sinks_ref, kprev, kprev_rot, vprev_t, has_prev, out):
    ones = _head_ones()
    half0 = lax.broadcasted_iota(jnp.int32, (1, LANES), 1) < HEAD_DIM
    qn, kn = _qk_norm(q, k, qw, kw, ones)
    kn_b = kn.astype(BF16)
    kn_rot = pltpu.roll(kn, HEAD_DIM, 1).astype(BF16)
    v_t = v.T.astype(BF16)
    keys = jnp.concatenate([kprev[...], kn_b], axis=0)
    keys_rot = jnp.concatenate([kprev_rot[...], kn_rot], axis=0)
    vals_t = jnp.concatenate([vprev_t[...], v_t], axis=1)
    kprev[...] = kn_b
    kprev_rot[...] = kn_rot
    vprev_t[...] = v_t
    kj = lax.broadcasted_iota(jnp.int32, (2 * WINDOW, WINDOW), 0)
    qi = lax.broadcasted_iota(jnp.int32, (2 * WINDOW, WINDOW), 1)
    valid_t = (kj > qi) & (kj <= qi + WINDOW) & ((kj >= WINDOW) | has_prev)
    heads = range(N_Q_HEADS)
    yield
    qm = [jnp.where(half0 if h % 2 == 0 else jnp.logical_not(half0), qn[:, (h // 2) * LANES:(h // 2 + 1) * LANES],
                    0.0).astype(BF16) for h in heads]
    straight = [h for h in heads if h // Q_PER_KV == h % 2]
    swapped = [h for h in heads if h // Q_PER_KV != h % 2]
    scores = [None] * N_Q_HEADS
    for group, kmat in ((straight, keys), (swapped, keys_rot)):
        for h0, h1 in zip(group[0::2], group[1::2]):
            s2 = _dot_nt(kmat, jnp.concatenate([qm[h0], qm[h1]], axis=0))
            scores[h0] = jnp.where(valid_t, s2[:, :WINDOW], NEG_INF)
            scores[h1] = jnp.where(valid_t, s2[:, WINDOW:], NEG_INF)
    yield
    probs = []
    for h in heads:
        sink = sinks_ref[h:h + 1, 0:1]
        m = jnp.maximum(jnp.max(scores[h], axis=0, keepdims=True), sink)
        pr = jnp.exp(scores[h] - m)
        denom = jnp.sum(pr, axis=0, keepdims=True) + jnp.exp(sink - m)
        probs.append((pr * (1.0 / denom)).astype(BF16))
    yield
    tiles = []
    for t in range(D_ATTN // LANES):
        g = (2 * t) // Q_PER_KV
        o2 = jnp.dot(vals_t[g * HEAD_DIM:(g + 1) * HEAD_DIM, :], jnp.concatenate(probs[2 * t:2 * t + 2], axis=1),
                     preferred_element_type=F32)
        tiles.append(jnp.concatenate([o2[:, :WINDOW], o2[:, WINDOW:]], axis=0).T)
    out["o_a"] = jnp.concatenate(tiles, axis=1)
    out["kn"] = kn


def _attn_sample_kernel(t_new, q_ref, k_ref, v_ref, ck_ref, cv_ref, qw_ref, kw_ref, sinks_ref,
                        o_ref, ko_ref, vo_ref):
    n_seq = q_ref.shape[0]
    wb = ck_ref.shape[1]
    rows = N_Q_HEADS * t_new
    ones = _head_ones()
    half0 = lax.broadcasted_iota(jnp.int32, (1, LANES), 1) < HEAD_DIM
    qi_c = lax.broadcasted_iota(jnp.int32, (rows, wb), 0) % t_new
    kj_c = lax.broadcasted_iota(jnp.int32, (rows, wb), 1)
    valid_c = (qi_c + wb - kj_c) < WINDOW
    qi_n = lax.broadcasted_iota(jnp.int32, (rows, t_new), 0) % t_new
    kj_n = lax.broadcasted_iota(jnp.int32, (rows, t_new), 1)
    valid_n = kj_n <= qi_n
    sink = sinks_ref[:, 0:1]
    heads = range(N_Q_HEADS)
    swap = [h // Q_PER_KV != h % 2 for h in heads]

    seqs = []
    for i in range(n_seq):
        v = v_ref[i]
        qn, kn = _qk_norm(q_ref[i], k_ref[i], qw_ref[...], kw_ref[...], ones)
        pieces = []
        for h in heads:
            qm = jnp.where(half0 if h % 2 == 0 else jnp.logical_not(half0), qn[:, (h // 2) * LANES:(h // 2 + 1) * LANES],
                           0.0)
            pieces.append(pltpu.roll(qm, HEAD_DIM, 1) if swap[h] else qm)
        seqs.append(dict(q=jnp.concatenate(pieces, axis=0).astype(BF16), kn=kn, v=v, ck=ck_ref[i], cv=cv_ref[i]))
    for s in seqs:
        s["s_c"] = jnp.where(valid_c, _dot_nt(s["q"], s["ck"]), NEG_INF)
        s["s_n"] = jnp.where(valid_n, _dot_nt(s["q"], s["kn"]), NEG_INF)
    for s in seqs:
        m = jnp.maximum(jnp.maximum(jnp.max(s["s_c"], axis=-1, keepdims=True),
                                    jnp.max(s["s_n"], axis=-1, keepdims=True)), sink)
        p_c = jnp.exp(s["s_c"] - m)
        p_n = jnp.exp(s["s_n"] - m)
        denom = jnp.sum(p_c, axis=-1, keepdims=True) + jnp.sum(p_n, axis=-1, keepdims=True) + jnp.exp(sink - m)
        s["p_c"], s["p_n"], s["inv"] = p_c, p_n, 1.0 / denom
    for i, s in enumerate(seqs):
        o = (_dot(s["p_c"], s["cv"]) + _dot(s["p_n"], s["v"])) * s["inv"]
        tiles = []
        for t in range(D_ATTN // LANES):
            pair = []
            for h in (2 * t, 2 * t + 1):
                o_h = o[h * t_new:(h + 1) * t_new]
                pair.append(pltpu.roll(o_h, HEAD_DIM, 1) if swap[h] else o_h)
            tiles.append(jnp.where(half0, pair[0], pair[1]))
        o_ref[i] = jnp.concatenate(tiles, axis=1)
        ko_ref[i] = jnp.concatenate([s["ck"][t_new:], s["kn"]], axis=0)
        vo_ref[i] = jnp.concatenate([s["cv"][t_new:], s["v"]], axis=0)


def _attn_sample(q, k, v, ck, cv, qw, kw, sinks_rows):
    b, t_new, _ = q.shape
    wb = ck.shape[1]
    gb = SAMPLE_GROUP
    spec = lambda r, w: pl.BlockSpec((gb, r, w), lambda i: (i, 0, 0))
    const = lambda r, w: pl.BlockSpec((r, w), lambda i: (0, 0))
    return pl.pallas_call(
        functools.partial(_attn_sample_kernel, t_new),
        grid=(b // gb,),
        in_specs=[spec(t_new, D_ATTN), spec(t_new, D_KV), spec(t_new, D_KV), spec(wb, D_KV), spec(wb, D_KV),
                  const(1, D_ATTN), const(1, D_KV), const(N_Q_HEADS * t_new, LANES)],
        out_specs=[spec(t_new, D_ATTN), spec(wb, D_KV), spec(wb, D_KV)],
        out_shape=[jax.ShapeDtypeStruct((b, t_new, D_ATTN), F32),
                   jax.ShapeDtypeStruct((b, wb, D_KV), F32),
                   jax.ShapeDtypeStruct((b, wb, D_KV), F32)],
        compiler_params=pltpu.CompilerParams(dimension_semantics=("arbitrary",)),
        name="attn_sample",
    )(q, k, v, ck, cv, qw, kw, sinks_rows)


def _merge_rows(x, o_r, z_r, o_a, z_a, p, wout_ref, g_ref, wgate_ref, wproj_ref, fill=lambda: None):
    gr = o_r * jax.nn.silu(z_r)
    ga = o_a * jax.nn.silu(z_a)
    ple = _dot(p, wproj_ref[...])
    h = x + _dot(gr, wout_ref[0:D_RWKV, :]) + _dot(ga, wout_ref[D_RWKV:D_MODEL, :])
    fill()
    gate = jax.nn.sigmoid(_dot(_rms_rows(h, g_ref[...]), wgate_ref[...]))
    fill()
    return h + gate * ple


def _merge_kernel(x_ref, or_ref, zr_ref, oa_ref, za_ref, p_ref, wout_ref, g_ref, wgate_ref, wproj_ref, y_ref):
    y_ref[...] = _merge_rows(x_ref[...], or_ref[...], zr_ref[...], oa_ref[...], za_ref[...], p_ref[...],
                             wout_ref, g_ref, wgate_ref, wproj_ref)


def _merge(x, o_r, z_r, o_a, z_a, p, w_out, g_ple, w_gate, w_proj, tm):
    m = x.shape[0]
    tok = lambda w: pl.BlockSpec((tm, w), lambda i: (i, 0))
    const = lambda r, w: pl.BlockSpec((r, w), lambda i: (0, 0))
    return pl.pallas_call(
        _merge_kernel,
        grid=(m // tm,),
        in_specs=[tok(D_MODEL), tok(D_RWKV), tok(D_RWKV), tok(D_ATTN), tok(D_ATTN), tok(D_PLE),
                  const(D_MODEL, D_MODEL), const(1, D_MODEL), const(D_MODEL, D_MODEL), const(D_PLE, D_MODEL)],
        out_specs=tok(D_MODEL),
        out_shape=jax.ShapeDtypeStruct((m, D_MODEL), F32),
        compiler_params=pltpu.CompilerParams(dimension_semantics=("arbitrary",), vmem_limit_bytes=VMEM_LIMIT),
        name="merge",
    )(x, o_r, z_r, o_a, z_a, p, w_out, g_ple, w_gate, w_proj)


_IN_CHUNK = 512
PROMPT_ROWS = 2
_FILL_HEAD = 3
_FILL_TAIL = 2


def _fill_order(n_proj, n_attn):
    mid = n_proj - _FILL_HEAD - _FILL_TAIL
    order = ["p"] * _FILL_HEAD
    done = 0
    for i in range(n_attn):
        while done < mid and done * n_attn <= i * mid:
            order.append("p")
            done += 1
        order.append("a")
    return order + ["p"] * (mid - done + _FILL_TAIL)


def _prompt_layer_kernel(x_ref, p_ref, gn_ref, win_ref, *rest):
    vec_refs = rest[:9]
    (qw_ref, kw_ref, sinks_ref, wout_ref, gp_ref, wgate_ref, wproj_ref,
     y_ref, s_out_ref, shift_ref, kc_ref, vc_ref,
     hbuf, xbuf, s_scr, prev_scr, kprev, kprev_rot, vprev_t) = rest[9:]
    bb = x_ref.shape[0]
    n_rows = bb * WINDOW
    j = pl.program_id(1)
    slot = j % 2

    @pl.when(j == 0)
    def _():
        hbuf[1] = jnp.zeros(hbuf.shape[1:], F32)
        xbuf[1] = jnp.zeros(xbuf.shape[1:], F32)
        s_scr[...] = jnp.zeros(s_scr.shape, F32)
        prev_scr[...] = jnp.zeros(prev_scr.shape, F32)
        kprev[...] = jnp.zeros(kprev.shape, BF16)
        kprev_rot[...] = jnp.zeros(kprev_rot.shape, BF16)
        vprev_t[...] = jnp.zeros(vprev_t.shape, BF16)

    h_cur = hbuf.at[1 - slot]
    x_new = x_ref[...].reshape(n_rows, D_MODEL)
    xbuf[slot] = x_new
    xn = _rms_rows(x_new, gn_ref[...]).astype(BF16)
    h_new = hbuf.at[slot]

    def project(c0):
        c1 = min(c0 + _IN_CHUNK, D_IN)
        h_new[:, c0:c1] = jnp.dot(xn, win_ref[:, c0:c1], preferred_element_type=F32)

    offs = [0]
    for w in _IN_SPLITS:
        offs.append(offs[-1] + w)
    part = lambda i: h_cur[:, offs[i]:offs[i + 1]]
    q, k, v = part(2), part(3), part(4)
    attn = [{} for _ in range(bb)]
    attn_steps = []
    for b in range(bb):
        rows = slice(b * WINDOW, (b + 1) * WINDOW)
        attn_steps.append(_swa_prompt_steps(q[rows], k[rows], v[rows], qw_ref[...], kw_ref[...], sinks_ref,
                                            kprev.at[b], kprev_rot.at[b], vprev_t.at[b], j > 1, attn[b]))
    proj_items = [functools.partial(project, c0) for c0 in range(0, D_IN, _IN_CHUNK)]
    attn_items = [functools.partial(next, attn_steps[b], None) for _ in range(4) for b in range(bb)]
    queue = [proj_items.pop(0) if kind == "p" else attn_items.pop(0)
             for kind in _fill_order(len(proj_items), len(attn_items))]
    assert not proj_items and not attn_items

    def fill():
        if queue:
            queue.pop(0)()

    for _ in range(_FILL_HEAD):
        fill()
    f = part(0)
    o_r = _time_mix(f, [prev_scr[b] for b in range(bb)], WINDOW, PROMPT_CHUNK, HEAD_DIM, vec_refs, s_scr, fill)
    for b in range(bb):
        prev_scr[b] = f[(b + 1) * WINDOW - 1:(b + 1) * WINDOW, :]
    while len(queue) > _FILL_TAIL:
        fill()
    o_a = jnp.concatenate([attn[b]["o_a"] for b in range(bb)], axis=0)
    y = _merge_rows(xbuf[1 - slot], o_r, part(1), o_a, part(5), p_ref[...].reshape(n_rows, D_PLE), wout_ref, gp_ref,
                    wgate_ref, wproj_ref, fill)
    y_ref[...] = y.reshape(y_ref.shape)
    while queue:
        fill()

    @pl.when(j == pl.num_programs(1) - 1)
    def _():
        _store_states(s_scr, s_out_ref, bb)
        shift_ref[...] = prev_scr[...]
        for b in range(bb):
            kc_ref[b] = attn[b]["kn"]
            vc_ref[b] = v[b * WINDOW:(b + 1) * WINDOW]


def _prompt_layer(x, p, gn, w_in_b, vecs, qw, kw, sinks_b, w_out_b, gp, w_gate_b, w_proj_b):
    batch, seq, _ = x.shape
    bb = PROMPT_ROWS
    nblk = seq // WINDOW
    const = lambda r, w: pl.BlockSpec((r, w), lambda g, j: (0, 0))
    per_row = lambda *shape: pl.BlockSpec((bb,) + shape, lambda g, j: (g,) + (0,) * len(shape))
    in_specs = ([pl.BlockSpec((bb, WINDOW, D_MODEL), lambda g, j: (g, jnp.minimum(j, nblk - 1), 0)),
                 pl.BlockSpec((bb, WINDOW, D_PLE), lambda g, j: (g, jnp.maximum(j - 1, 0), 0)),
                 const(1, D_MODEL), const(D_MODEL, D_IN)]
                + _vec_specs(lambda g, j: (0, 0))
                + [const(1, D_ATTN), const(1, D_KV), const(N_Q_HEADS, LANES), const(D_MODEL, D_MODEL),
                   const(1, D_MODEL), const(D_MODEL, D_MODEL), const(D_PLE, D_MODEL)])
    return pl.pallas_call(
        _prompt_layer_kernel,
        grid=(batch // bb, nblk + 1),
        in_specs=in_specs,
        out_specs=[pl.BlockSpec((bb, WINDOW, D_MODEL), lambda g, j: (g, jnp.maximum(j - 1, 0), 0)),
                   per_row(2 * N_PAIRS, HEAD_DIM, HEAD_DIM), per_row(1, D_SHIFT),
                   per_row(WINDOW, D_KV), per_row(WINDOW, D_KV)],
        out_shape=[jax.ShapeDtypeStruct((batch, seq, D_MODEL), F32),
                   jax.ShapeDtypeStruct((batch, 2 * N_PAIRS, HEAD_DIM, HEAD_DIM), F32),
                   jax.ShapeDtypeStruct((batch, 1, D_SHIFT), F32),
                   jax.ShapeDtypeStruct((batch, WINDOW, D_KV), F32),
                   jax.ShapeDtypeStruct((batch, WINDOW, D_KV), F32)],
        scratch_shapes=[pltpu.VMEM((2, bb * WINDOW, D_IN), F32), pltpu.VMEM((2, bb * WINDOW, D_MODEL), F32),
                        pltpu.VMEM((bb * N_PAIRS, LANES, LANES), F32), pltpu.VMEM((bb, 1, D_SHIFT), F32),
                        pltpu.VMEM((bb, WINDOW, D_KV), BF16), pltpu.VMEM((bb, WINDOW, D_KV), BF16),
                        pltpu.VMEM((bb, D_KV, WINDOW), BF16)],
        compiler_params=pltpu.CompilerParams(dimension_semantics=("arbitrary", "arbitrary"),
                                             vmem_limit_bytes=VMEM_LIMIT),
        name="prompt_layer",
    )(x, p, gn, w_in_b, *vecs, qw, kw, sinks_b, w_out_b, gp, w_gate_b, w_proj_b)


def kernel(x_prompt, x_sample, state_rwkv, state_shift, cache_k, cache_v, p_prompt, p_sample, g_norm, w_in, mu_shift, w0, w_dec2, a0, w_a2, k_k, k_a, r_k, lnx_w, lnx_b, q_norm_w, k_norm_w, sinks, w_out, g_ple, w_ple_gate, w_ple_proj):
    depth = w_in.shape[0]
    bp, seq, _ = x_prompt.shape
    bs, dec, _ = x_sample.shape
    wb = cache_k.shape[2]
    xp = x_prompt
    xs = x_sample.reshape(bs * dec, D_MODEL)
    outs = [[] for _ in range(8)]
    for i in range(depth):
        w_in_b = w_in[i].astype(BF16)
        w_out_b = w_out[i].astype(BF16)
        w_gate_b = w_ple_gate[i].astype(BF16)
        w_proj_b = w_ple_proj[i].astype(BF16)
        zl = jnp.zeros((D_LORA, D_RWKV), F32)
        lora_w = jnp.concatenate([jnp.concatenate([w_dec2[i], zl], axis=1),
                                  jnp.concatenate([zl, w_a2[i]], axis=1)], axis=0).astype(BF16)
        row = lambda t: t.reshape(1, -1)
        vecs = (row(mu_shift[i]), row(w0[i]), row(a0[i]), lora_w, row(k_k[i]), row(k_a[i]), row(r_k[i]),
                row(lnx_w[i]), row(lnx_b[i]))
        qw = row(jnp.tile(q_norm_w[i], N_Q_HEADS))
        kw = row(jnp.tile(k_norm_w[i], N_KV_HEADS))
        sinks_b = jnp.broadcast_to(sinks[i][:, None], (N_Q_HEADS, LANES))
        gn, gp = row(g_norm[i]), row(g_ple[i])

        xp, s_p, sh_p, kc, vc = _prompt_layer(xp, p_prompt[i], gn, w_in_b, vecs, qw, kw, sinks_b, w_out_b, gp,
                                              w_gate_b, w_proj_b)
        outs[0].append(s_p)
        outs[2].append(sh_p)
        outs[4].append(kc.reshape(bp, WINDOW, N_KV_HEADS, HEAD_DIM))
        outs[6].append(vc.reshape(bp, WINDOW, N_KV_HEADS, HEAD_DIM))

        f, z_r, q, k, v, z_a = _in_proj(xs, gn, w_in_b, 512)
        o_r, s_s = _rwkv_sample(f, state_shift[i], state_rwkv[i], vecs, bs, SAMPLE_GROUP, dec)
        o_a, k_buf, v_buf = _attn_sample(q.reshape(bs, dec, D_ATTN), k.reshape(bs, dec, D_KV), v.reshape(bs, dec, D_KV),
                                         cache_k[i].reshape(bs, wb, D_KV), cache_v[i].reshape(bs, wb, D_KV),
                                         qw, kw, jnp.repeat(sinks_b, dec, axis=0))
        outs[1].append(s_s)
        outs[3].append(f.reshape(bs, dec, D_SHIFT)[:, -1:])
        outs[5].append(k_buf.reshape(bs, wb, N_KV_HEADS, HEAD_DIM))
        outs[7].append(v_buf.reshape(bs, wb, N_KV_HEADS, HEAD_DIM))
        xs = _merge(xs, o_r, z_r, o_a.reshape(bs * dec, D_ATTN), z_a, p_sample[i].reshape(bs * dec, D_PLE),
                    w_out_b, gp, w_gate_b, w_proj_b, 512)
    st = lambda l: jnp.stack(l)
    return (xp, xs.reshape(bs, dec, D_MODEL),
            st(outs[0]), st(outs[1]), st(outs[2]), st(outs[3]), st(outs[4]), st(outs[5]), st(outs[6]), st(outs[7]))
```

```python
import functools
import math

import jax
import jax.numpy as jnp
from jax import lax
from jax.experimental import pallas as pl
from jax.experimental.pallas import tpu as pltpu

F32 = jnp.float32
BF16 = jnp.bfloat16

D_MODEL = 1024
HEAD_DIM = 64
D_RWKV = 512
D_ATTN = 512
N_KV_HEADS = 2
N_Q_HEADS = 8
Q_PER_KV = N_Q_HEADS // N_KV_HEADS
D_KV = N_KV_HEADS * HEAD_DIM
WINDOW = 128
D_LORA = 64
D_SHIFT = 3 * D_RWKV + 2 * D_LORA
D_PLE = 256
D_IN = D_SHIFT + D_RWKV + D_ATTN + 2 * D_KV + D_ATTN
NORM_EPS = 1e-6
LNX_EPS = 64e-5
NEG_INF = -1e30

LANES = 128
N_PAIRS = D_RWKV // LANES
PROMPT_CHUNK = 64
SAMPLE_GROUP = 16
VMEM_LIMIT = 56 * 1024 * 1024


def _dot(a, b):
    return jnp.dot(a.astype(BF16), b.astype(BF16), preferred_element_type=F32)


def _dot_nt(a, b):
    return lax.dot_general(a.astype(BF16), b.astype(BF16), (((1,), (1,)), ((), ())), preferred_element_type=F32)


def _dot_tn(a, b):
    return lax.dot_general(a.astype(BF16), b.astype(BF16), (((0,), (0,)), ((), ())), preferred_element_type=F32)


def _segment_cumsum(x, seg):
    blk = min(max(seg, HEAD_DIM), x.shape[0])
    ti = lax.broadcasted_iota(jnp.int32, (blk, 3 * blk), 0)
    tj = lax.broadcasted_iota(jnp.int32, (blk, 3 * blk), 1) % blk
    tri3 = ((ti // seg == tj // seg) & (tj <= ti)).astype(BF16)
    hi = x.astype(BF16)
    r1 = x - hi.astype(F32)
    mid = r1.astype(BF16)
    lo = (r1 - mid.astype(F32)).astype(BF16)
    parts = []
    for r0 in range(0, x.shape[0], blk):
        rows = slice(r0, r0 + blk)
        parts.append(jnp.dot(tri3, jnp.concatenate([hi[rows], mid[rows], lo[rows]], axis=0),
                             preferred_element_type=F32))
    return parts[0] if len(parts) == 1 else jnp.concatenate(parts, axis=0)


def _head_ones():
    r = (lax.broadcasted_iota(jnp.int32, (2 * LANES, LANES), 0) % LANES) // HEAD_DIM
    c = lax.broadcasted_iota(jnp.int32, (2 * LANES, LANES), 1) // HEAD_DIM
    return (r == c).astype(BF16)


def _head_sum(x, ones):
    tiles = []
    for i in range(0, x.shape[1], LANES):
        xt = x[:, i:i + LANES]
        hi = xt.astype(BF16)
        lo = (xt - hi.astype(F32)).astype(BF16)
        tiles.append(jnp.dot(jnp.concatenate([hi, lo], axis=1), ones, preferred_element_type=F32))
    return tiles[0] if len(tiles) == 1 else jnp.concatenate(tiles, axis=1)


def _rms_rows(x, g):
    return x * lax.rsqrt(jnp.mean(x * x, axis=-1, keepdims=True) + NORM_EPS) * g


_IN_SPLITS = (D_SHIFT, D_RWKV, D_ATTN, D_KV, D_KV, D_ATTN)


def _in_proj_kernel(x_ref, g_ref, w_ref, *out_refs):
    h = _dot(_rms_rows(x_ref[...], g_ref[...]), w_ref[...])
    off = 0
    for o_ref, width in zip(out_refs, _IN_SPLITS):
        o_ref[...] = h[:, off:off + width]
        off += width


def _in_proj(x, g_norm, w_in_bf16, tm):
    m = x.shape[0]
    return pl.pallas_call(
        _in_proj_kernel,
        grid=(m // tm,),
        in_specs=[pl.BlockSpec((tm, D_MODEL), lambda i: (i, 0)),
                  pl.BlockSpec((1, D_MODEL), lambda i: (0, 0)),
                  pl.BlockSpec((D_MODEL, D_IN), lambda i: (0, 0))],
        out_specs=[pl.BlockSpec((tm, w), lambda i: (i, 0)) for w in _IN_SPLITS],
        out_shape=[jax.ShapeDtypeStruct((m, w), F32) for w in _IN_SPLITS],
        compiler_params=pltpu.CompilerParams(dimension_semantics=("arbitrary",), vmem_limit_bytes=VMEM_LIMIT),
        name="in_proj",
    )(x, g_norm, w_in_bf16)


def _stack(z, half0):
    return jnp.concatenate([jnp.where(half0, z, 0.0), jnp.where(half0, 0.0, z)], axis=0).astype(BF16)


def _rwkv_recurrence(at, rt, bt, kt, v, e_cum, s_scr, cm, seg, segs_per_state, fill):
    n_rows = at.shape[0]
    n_blk = n_rows // cm
    lane = lax.broadcasted_iota(jnp.int32, (1, LANES), 1)
    half0 = lane < HEAD_DIM
    ri = lax.broadcasted_iota(jnp.int32, (cm, LANES), 0)
    ci = lax.broadcasted_iota(jnp.int32, (cm, LANES), 1) % cm
    same = (ri // seg) == (ci // seg)
    tri_strict = same & (ci < ri)
    tri_incl = same & (ci <= ri)
    eye_c = (ri == ci).astype(F32)
    sr = lax.broadcasted_iota(jnp.int32, (LANES, LANES), 0) // HEAD_DIM
    sc = lax.broadcasted_iota(jnp.int32, (LANES, LANES), 1) // HEAD_DIM
    state_mask = sr == sc
    n_levels = max(int(math.log2(seg)) - 1, 0)
    blocks = [(rb, p) for rb in range(n_blk) for p in range(N_PAIRS)]

    def tile(x, rb, p):
        return x[rb * cm:(rb + 1) * cm, p * LANES:(p + 1) * LANES]

    ops = {}
    for rb, p in blocks:
        at_p, rt_p, bt_p, kt_p, v_p = (tile(x, rb, p) for x in (at, rt, bt, kt, v))
        ops[rb, p] = dict(at=at_p, rt=rt_p, bt=bt_p, kt=kt_p, v=v_p, v_s=_stack(v_p, half0))
    for blk in blocks:
        o = ops[blk]
        g = _dot_nt(jnp.concatenate([o["at"], o["rt"]], axis=0),
                    jnp.concatenate([_stack(o["bt"], half0), _stack(o["kt"], half0)], axis=0))
        o["g_ab"] = jnp.where(tri_strict, g[:cm, :LANES], 0.0)
        o["g_ak"] = jnp.where(tri_strict, g[:cm, LANES:], 0.0)
        o["g_r"] = jnp.concatenate([jnp.where(tri_incl, g[cm:, :LANES], 0.0),
                                    jnp.where(tri_incl, g[cm:, LANES:], 0.0)], axis=1).astype(BF16)
    fill()

    for blk in blocks:
        ops[blk]["t_inv"] = eye_c + ops[blk]["g_ab"]
    if n_levels > 0:
        for blk in blocks:
            ops[blk]["a_pow"] = _dot(ops[blk]["g_ab"], _stack(ops[blk]["g_ab"], half0))
        fill()
        for lvl in range(n_levels):
            last = lvl == n_levels - 1
            for blk in blocks:
                o = ops[blk]
                if last:
                    o["t_inv"] = o["t_inv"] + _dot(o["a_pow"], _stack(o["t_inv"], half0))
                else:
                    m = _dot(o["a_pow"], jnp.concatenate([_stack(o["a_pow"], half0), _stack(o["t_inv"], half0)], axis=1))
                    o["t_inv"] = o["t_inv"] + m[:, LANES:]
                    o["a_pow"] = m[:, :LANES]
            fill()

    for blk in blocks:
        o = ops[blk]
        gakv = _dot(o["g_ak"], o["v_s"])
        z = _dot(o["t_inv"], jnp.concatenate([_stack(o["at"], half0), _stack(gakv, half0)], axis=1))
        o["a_hat"], o["p0"] = z[:, :LANES], z[:, LANES:]
    fill()

    n_seg = n_rows // seg
    n_states = n_seg // segs_per_state
    per_blk = cm // seg
    p_parts = {blk: [None] * per_blk for blk in blocks}
    y_parts = {blk: [None] * per_blk for blk in blocks}
    for step in range(segs_per_state):
        segs = [st * segs_per_state + step for st in range(n_states)]
        proj = {}
        for g_i in segs:
            rb, off = (g_i * seg) // cm, (g_i * seg) % cm
            for p in range(N_PAIRS):
                o = ops[rb, p]
                lhs = jnp.concatenate([o["a_hat"][off:off + seg], o["rt"][off:off + seg]], axis=0)
                proj[g_i, p] = _dot_nt(lhs, s_scr[(g_i // segs_per_state) * N_PAIRS + p])
        for g_i in segs:
            rb, off = (g_i * seg) // cm, (g_i * seg) % cm
            for p in range(N_PAIRS):
                o = ops[rb, p]
                p_seg = proj[g_i, p][:seg] + o["p0"][off:off + seg]
                p_parts[rb, p][off // seg] = p_seg
                y_parts[rb, p][off // seg] = proj[g_i, p][seg:]
                upd = _dot_tn(jnp.concatenate([p_seg, o["v"][off:off + seg]], axis=0),
                              jnp.concatenate([o["bt"][off:off + seg], o["kt"][off:off + seg]], axis=0))
                si = (g_i // segs_per_state) * N_PAIRS + p
                row_end = g_i * seg + seg - 1
                w_end = e_cum[row_end:row_end + 1, p * LANES:(p + 1) * LANES]
                s_scr[si] = w_end * (s_scr[si] + jnp.where(state_mask, upd, 0.0))

    cat = lambda parts: parts[0] if len(parts) == 1 else jnp.concatenate(parts, axis=0)
    rows = []
    for rb in range(n_blk):
        tiles = []
        for p in range(N_PAIRS):
            o = ops[rb, p]
            pv_s = jnp.concatenate([_stack(cat(p_parts[rb, p]), half0), o["v_s"]], axis=0)
            tiles.append(cat(y_parts[rb, p]) + jnp.dot(o["g_r"], pv_s, preferred_element_type=F32))
        rows.append(jnp.concatenate(tiles, axis=1))
    return cat(rows)


def _time_mix(f, prev_rows, tb, seg, cm, vec_refs, s_scr, fill=lambda: None):
    mu_ref, w0_ref, a0_ref, lora_ref, kk_ref, ka_ref, rk_ref, lnw_ref, lnb_ref = vec_refs
    n_rows = f.shape[0]
    row = lax.broadcasted_iota(jnp.int32, (n_rows, 1), 0)
    f_prev = pltpu.roll(f, 1, 0)
    for b, prev in enumerate(prev_rows):
        f_prev = jnp.where(row == b * tb, prev, f_prev)
    fs = f + (f_prev - f) * mu_ref[...]
    r = fs[:, 0:D_RWKV]
    k = fs[:, D_RWKV:2 * D_RWKV]
    v = fs[:, 2 * D_RWKV:3 * D_RWKV]
    wa = fs[:, 3 * D_RWKV:D_SHIFT]
    lane = lax.broadcasted_iota(jnp.int32, (1, LANES), 1)
    lora = _dot(jnp.where(lane < D_LORA, jnp.tanh(wa), wa), lora_ref[...])
    lw = (-math.exp(-0.5)) * jax.nn.sigmoid(w0_ref[...] + lora[:, 0:D_RWKV])
    a_sig = jax.nn.sigmoid(a0_ref[...] + lora[:, D_RWKV:2 * D_RWKV])
    ones = _head_ones()
    kk = k * kk_ref[...]
    kk = kk * lax.rsqrt(jnp.maximum(_head_sum(kk * kk, ones), 1e-24))
    k2 = k * (1.0 + (a_sig - 1.0) * ka_ref[...])

    cum = _segment_cumsum(lw, seg)
    e_cum = jnp.exp(cum)
    e_inv = jnp.exp(-cum)
    y = _rwkv_recurrence(-kk * jnp.exp(cum - lw), r * e_cum, kk * a_sig * e_inv, k2 * e_inv, v, e_cum,
                         s_scr, cm, seg, tb // seg, fill)

    inv_n = 1.0 / HEAD_DIM
    yc = y - _head_sum(y, ones) * inv_n
    var = _head_sum(yc * yc, ones) * inv_n
    yn = yc * lax.rsqrt(var + LNX_EPS) * lnw_ref[...] + lnb_ref[...]
    bonus = _head_sum(r * k2 * rk_ref[...], ones) * v
    return yn + bonus


def _load_states(s0_ref, s_scr, bb):
    zero = jnp.zeros((HEAD_DIM, HEAD_DIM), F32)
    for b in range(bb):
        for p in range(N_PAIRS):
            top = jnp.concatenate([s0_ref[b, 2 * p], zero], axis=1)
            bot = jnp.concatenate([zero, s0_ref[b, 2 * p + 1]], axis=1)
            s_scr[b * N_PAIRS + p] = jnp.concatenate([top, bot], axis=0)


def _store_states(s_scr, s_out_ref, bb):
    for b in range(bb):
        for p in range(N_PAIRS):
            s = s_scr[b * N_PAIRS + p]
            s_out_ref[b, 2 * p] = s[:HEAD_DIM, :HEAD_DIM]
            s_out_ref[b, 2 * p + 1] = s[HEAD_DIM:, HEAD_DIM:]


def _rwkv_sample_kernel(bb, tb, f_ref, prev0_ref, *rest):
    vec_refs, (s0_ref, o_ref, s_out_ref, s_scr) = rest[:9], rest[9:]
    _load_states(s0_ref, s_scr, bb)
    o_ref[...] = _time_mix(f_ref[...], [prev0_ref[b] for b in range(bb)], tb, tb, HEAD_DIM, vec_refs, s_scr)
    _store_states(s_scr, s_out_ref, bb)


def _vec_specs(index_map):
    vec = lambda n: pl.BlockSpec((1, n), index_map)
    return [vec(D_SHIFT), vec(D_RWKV), vec(D_RWKV), pl.BlockSpec((LANES, 2 * D_RWKV), index_map),
            vec(D_RWKV), vec(D_RWKV), vec(D_RWKV), vec(D_RWKV), vec(D_RWKV)]


def _rwkv_sample(f, prev0, s0, vecs, batch, bb, tb):
    n_rows = bb * tb
    assert n_rows % HEAD_DIM == 0 and HEAD_DIM % tb == 0
    state_spec = pl.BlockSpec((bb, 2 * N_PAIRS, HEAD_DIM, HEAD_DIM), lambda b: (b, 0, 0, 0))
    return pl.pallas_call(
        functools.partial(_rwkv_sample_kernel, bb, tb),
        grid=(batch // bb,),
        in_specs=[pl.BlockSpec((n_rows, D_SHIFT), lambda b: (b, 0)),
                  pl.BlockSpec((bb, 1, D_SHIFT), lambda b: (b, 0, 0))] + _vec_specs(lambda b: (0, 0)) + [state_spec],
        out_specs=[pl.BlockSpec((n_rows, D_RWKV), lambda b: (b, 0)), state_spec],
        out_shape=[jax.ShapeDtypeStruct((batch * tb, D_RWKV), F32),
                   jax.ShapeDtypeStruct((batch, 2 * N_PAIRS, HEAD_DIM, HEAD_DIM), F32)],
        scratch_shapes=[pltpu.VMEM((bb * N_PAIRS, LANES, LANES), F32)],
        compiler_params=pltpu.CompilerParams(dimension_semantics=("arbitrary",), vmem_limit_bytes=VMEM_LIMIT),
        name="rwkv_sample",
    )(f, prev0, *vecs, s0)


def _qk_norm(q, k, qw, kw, ones):
    inv_n = 1.0 / HEAD_DIM
    qn = q * lax.rsqrt(_head_sum(q * q, ones) * inv_n + NORM_EPS) * (qw * (HEAD_DIM ** -0.5))
    kn = k * lax.rsqrt(_head_sum(k * k, ones) * inv_n + NORM_EPS) * kw
    return qn, kn


def _swa_prompt_steps(q, k, v, qw, kw, sinks_ref, kprev, kprev_rot, vprev_t, has_prev, out):
    ones = _head_ones()
    half0 = lax.broadcasted_iota(jnp.int32, (1, LANES), 1) < HEAD_DIM
    qn, kn = _qk_norm(q, k, qw, kw, ones)
    kn_b = kn.astype(BF16)
    kn_rot = pltpu.roll(kn, HEAD_DIM, 1).astype(BF16)
    v_t = v.T.astype(BF16)
    keys = jnp.concatenate([kprev[...], kn_b], axis=0)
    keys_rot = jnp.concatenate([kprev_rot[...], kn_rot], axis=0)
    vals_t = jnp.concatenate([vprev_t[...], v_t], axis=1)
    kprev[...] = kn_b
    kprev_rot[...] = kn_rot
    vprev_t[...] = v_t
    kj = lax.broadcasted_iota(jnp.int32, (2 * WINDOW, WINDOW), 0)
    qi = lax.broadcasted_iota(jnp.int32, (2 * WINDOW, WINDOW), 1)
    valid_t = (kj > qi) & (kj <= qi + WINDOW) & ((kj >= WINDOW) | has_prev)
    heads = range(N_Q_HEADS)
    yield
    qm = [jnp.where(half0 if h % 2 == 0 else jnp.logical_not(half0), qn[:, (h // 2) * LANES:(h // 2 + 1) * LANES],
                    0.0).astype(BF16) for h in heads]
    straight = [h for h in heads if h // Q_PER_KV == h % 2]
    swapped = [h for h in heads if h // Q_PER_KV != h % 2]
    scores = [None] * N_Q_HEADS
    for group, kmat in ((straight, keys), (swapped, keys_rot)):
        for h0, h1 in zip(group[0::2], group[1::2]):
            s2 = _dot_nt(kmat, jnp.concatenate([qm[h0], qm[h1]], axis=0))
            scores[h0] = jnp.where(valid_t, s2[:, :WINDOW], NEG_INF)
            scores[h1] = jnp.where(valid_t, s2[:, WINDOW:], NEG_INF)
    yield
    probs = []
    for h in heads:
        sink = sinks_ref[h:h + 1, 0:1]
        m = jnp.maximum(jnp.max(scores[h], axis=0, keepdims=True), sink)
        pr = jnp.exp(scores[h] - m)
        denom = jnp.sum(pr, axis=0, keepdims=True) + jnp.exp(sink - m)
        probs.append((pr * (1.0 / denom)).astype(BF16))
    yield
    tiles = []
    for t in range(D_ATTN // LANES):
        g = (2 * t) // Q_PER_KV
        o2 = jnp.dot(vals_t[g * HEAD_DIM:(g + 1) * HEAD_DIM, :], jnp.concatenate(probs[2 * t:2 * t + 2], axis=1),
                     preferred_element_type=F32)
        tiles.append(jnp.concatenate([o2[:, :WINDOW], o2[:, WINDOW:]], axis=0).T)
    out["o_a"] = jnp.concatenate(tiles, axis=1)
    out["kn"] = kn


def _attn_sample_kernel(t_new, q_ref, k_ref, v_ref, ckt_ref, cvt_ref, qw_ref, kw_ref, sinks_ref,
                        o_ref, kot_ref, vot_ref):
    n_seq = q_ref.shape[0]
    wb = ckt_ref.shape[2]
    rows = N_Q_HEADS * t_new
    ones = _head_ones()
    half0 = lax.broadcasted_iota(jnp.int32, (1, LANES), 1) < HEAD_DIM
    qi_c = lax.broadcasted_iota(jnp.int32, (rows, wb), 0) % t_new
    kj_c = lax.broadcasted_iota(jnp.int32, (rows, wb), 1)
    valid_c = (qi_c + wb - kj_c) < WINDOW
    qi_n = lax.broadcasted_iota(jnp.int32, (rows, t_new), 0) % t_new
    kj_n = lax.broadcasted_iota(jnp.int32, (rows, t_new), 1)
    valid_n = kj_n <= qi_n
    sink = sinks_ref[:, 0:1]
    heads = range(N_Q_HEADS)
    swap = [h // Q_PER_KV != h % 2 for h in heads]

    seqs = []
    for i in range(n_seq):
        qn, kn = _qk_norm(q_ref[i], k_ref[i], qw_ref[...], kw_ref[...], ones)
        pieces = []
        for h in heads:
            qm = jnp.where(half0 if h % 2 == 0 else jnp.logical_not(half0), qn[:, (h // 2) * LANES:(h // 2 + 1) * LANES],
                           0.0)
            pieces.append(pltpu.roll(qm, HEAD_DIM, 1) if swap[h] else qm)
        seqs.append(dict(q=jnp.concatenate(pieces, axis=0).astype(BF16), knt=kn.T, vt=v_ref[i].T,
                         ckt=ckt_ref[i], cvt=cvt_ref[i]))
    for s in seqs:
        s["s_c"] = jnp.where(valid_c, _dot(s["q"], s["ckt"]), NEG_INF)
        s["s_n"] = jnp.where(valid_n, _dot(s["q"], s["knt"]), NEG_INF)
    for s in seqs:
        m = jnp.maximum(jnp.maximum(jnp.max(s["s_c"], axis=-1, keepdims=True),
                                    jnp.max(s["s_n"], axis=-1, keepdims=True)), sink)
        p_c = jnp.exp(s["s_c"] - m)
        p_n = jnp.exp(s["s_n"] - m)
        denom = jnp.sum(p_c, axis=-1, keepdims=True) + jnp.sum(p_n, axis=-1, keepdims=True) + jnp.exp(sink - m)
        s["p_c"], s["p_n"], s["inv"] = p_c, p_n, 1.0 / denom
    for i, s in enumerate(seqs):
        o = (_dot_nt(s["p_c"], s["cvt"]) + _dot_nt(s["p_n"], s["vt"])) * s["inv"]
        tiles = []
        for t in range(D_ATTN // LANES):
            pair = []
            for h in (2 * t, 2 * t + 1):
                o_h = o[h * t_new:(h + 1) * t_new]
                pair.append(pltpu.roll(o_h, HEAD_DIM, 1) if swap[h] else o_h)
            tiles.append(jnp.where(half0, pair[0], pair[1]))
        o_ref[i] = jnp.concatenate(tiles, axis=1)
        kot_ref[i] = jnp.concatenate([s["ckt"][:, t_new:], s["knt"]], axis=1)
        vot_ref[i] = jnp.concatenate([s["cvt"][:, t_new:], s["vt"]], axis=1)


def _attn_sample(q, k, v, ckt, cvt, qw, kw, sinks_rows):
    b, t_new, _ = q.shape
    wb = ckt.shape[2]
    gb = SAMPLE_GROUP
    spec = lambda r, w: pl.BlockSpec((gb, r, w), lambda i: (i, 0, 0))
    const = lambda r, w: pl.BlockSpec((r, w), lambda i: (0, 0))
    return pl.pallas_call(
        functools.partial(_attn_sample_kernel, t_new),
        grid=(b // gb,),
        in_specs=[spec(t_new, D_ATTN), spec(t_new, D_KV), spec(t_new, D_KV), spec(D_KV, wb), spec(D_KV, wb),
                  const(1, D_ATTN), const(1, D_KV), const(N_Q_HEADS * t_new, LANES)],
        out_specs=[spec(t_new, D_ATTN), spec(D_KV, wb), spec(D_KV, wb)],
        out_shape=[jax.ShapeDtypeStruct((b, t_new, D_ATTN), F32),
                   jax.ShapeDtypeStruct((b, D_KV, wb), F32),
                   jax.ShapeDtypeStruct((b, D_KV, wb), F32)],
        compiler_params=pltpu.CompilerParams(dimension_semantics=("arbitrary",)),
        name="attn_sample",
    )(q, k, v, ckt, cvt, qw, kw, sinks_rows)


def _merge_rows(x, o_r, z_r, o_a, z_a, p, wout_ref, g_ref, wgate_ref, wproj_ref, fill=lambda: None):
    gr = o_r * jax.nn.silu(z_r)
    ga = o_a * jax.nn.silu(z_a)
    ple = _dot(p, wproj_ref[...])
    h = x + _dot(gr, wout_ref[0:D_RWKV, :]) + _dot(ga, wout_ref[D_RWKV:D_MODEL, :])
    fill()
    gate = jax.nn.sigmoid(_dot(_rms_rows(h, g_ref[...]), wgate_ref[...]))
    fill()
    return h + gate * ple


def _merge_kernel(x_ref, or_ref, zr_ref, oa_ref, za_ref, p_ref, wout_ref, g_ref, wgate_ref, wproj_ref, y_ref):
    y_ref[...] = _merge_rows(x_ref[...], or_ref[...], zr_ref[...], oa_ref[...], za_ref[...], p_ref[...],
                             wout_ref, g_ref, wgate_ref, wproj_ref)


def _merge(x, o_r, z_r, o_a, z_a, p, w_out, g_ple, w_gate, w_proj, tm):
    m = x.shape[0]
    tok = lambda w: pl.BlockSpec((tm, w), lambda i: (i, 0))
    const = lambda r, w: pl.BlockSpec((r, w), lambda i: (0, 0))
    return pl.pallas_call(
        _merge_kernel,
        grid=(m // tm,),
        in_specs=[tok(D_MODEL), tok(D_RWKV), tok(D_RWKV), tok(D_ATTN), tok(D_ATTN), tok(D_PLE),
                  const(D_MODEL, D_MODEL), const(1, D_MODEL), const(D_MODEL, D_MODEL), const(D_PLE, D_MODEL)],
        out_specs=tok(D_MODEL),
        out_shape=jax.ShapeDtypeStruct((m, D_MODEL), F32),
        compiler_params=pltpu.CompilerParams(dimension_semantics=("arbitrary",), vmem_limit_bytes=VMEM_LIMIT),
        name="merge",
    )(x, o_r, z_r, o_a, z_a, p, w_out, g_ple, w_gate, w_proj)


_IN_CHUNK = 512
PROMPT_ROWS = 2
_FILL_HEAD = 3
_FILL_TAIL = 2


def _fill_order(n_proj, n_attn):
    mid = n_proj - _FILL_HEAD - _FILL_TAIL
    order = ["p"] * _FILL_HEAD
    done = 0
    for i in range(n_attn):
        while done < mid and done * n_attn <= i * mid:
            order.append("p")
            done += 1
        order.append("a")
    return order + ["p"] * (mid - done + _FILL_TAIL)


def _prompt_layer_kernel(x_ref, p_ref, gn_ref, win_ref, *rest):
    vec_refs = rest[:9]
    (qw_ref, kw_ref, sinks_ref, wout_ref, gp_ref, wgate_ref, wproj_ref,
     y_ref, s_out_ref, shift_ref, kc_ref, vc_ref,
     hbuf, xbuf, s_scr, prev_scr, kprev, kprev_rot, vprev_t) = rest[9:]
    bb = x_ref.shape[0]
    n_rows = bb * WINDOW
    j = pl.program_id(1)
    slot = j % 2

    @pl.when(j == 0)
    def _():
        hbuf[1] = jnp.zeros(hbuf.shape[1:], F32)
        xbuf[1] = jnp.zeros(xbuf.shape[1:], F32)
        s_scr[...] = jnp.zeros(s_scr.shape, F32)
        prev_scr[...] = jnp.zeros(prev_scr.shape, F32)
        kprev[...] = jnp.zeros(kprev.shape, BF16)
        kprev_rot[...] = jnp.zeros(kprev_rot.shape, BF16)
        vprev_t[...] = jnp.zeros(vprev_t.shape, BF16)

    h_cur = hbuf.at[1 - slot]
    x_new = x_ref[...].reshape(n_rows, D_MODEL)
    xbuf[slot] = x_new
    xn = _rms_rows(x_new, gn_ref[...]).astype(BF16)
    h_new = hbuf.at[slot]

    def project(c0):
        c1 = min(c0 + _IN_CHUNK, D_IN)
        h_new[:, c0:c1] = jnp.dot(xn, win_ref[:, c0:c1], preferred_element_type=F32)

    offs = [0]
    for w in _IN_SPLITS:
        offs.append(offs[-1] + w)
    part = lambda i: h_cur[:, offs[i]:offs[i + 1]]
    q, k, v = part(2), part(3), part(4)
    attn = [{} for _ in range(bb)]
    attn_steps = []
    for b in range(bb):
        rows = slice(b * WINDOW, (b + 1) * WINDOW)
        attn_steps.append(_swa_prompt_steps(q[rows], k[rows], v[rows], qw_ref[...], kw_ref[...], sinks_ref,
                                            kprev.at[b], kprev_rot.at[b], vprev_t.at[b], j > 1, attn[b]))
    proj_items = [functools.partial(project, c0) for c0 in range(0, D_IN, _IN_CHUNK)]
    attn_items = [functools.partial(next, attn_steps[b], None) for _ in range(4) for b in range(bb)]
    queue = [proj_items.pop(0) if kind == "p" else attn_items.pop(0)
             for kind in _fill_order(len(proj_items), len(attn_items))]
    assert not proj_items and not attn_items

    def fill():
        if queue:
            queue.pop(0)()

    for _ in range(_FILL_HEAD):
        fill()
    f = part(0)
    o_r = _time_mix(f, [prev_scr[b] for b in range(bb)], WINDOW, PROMPT_CHUNK, HEAD_DIM, vec_refs, s_scr, fill)
    for b in range(bb):
        prev_scr[b] = f[(b + 1) * WINDOW - 1:(b + 1) * WINDOW, :]
    while len(queue) > _FILL_TAIL:
        fill()
    o_a = jnp.concatenate([attn[b]["o_a"] for b in range(bb)], axis=0)
    y = _merge_rows(xbuf[1 - slot], o_r, part(1), o_a, part(5), p_ref[...].reshape(n_rows, D_PLE), wout_ref, gp_ref,
                    wgate_ref, wproj_ref, fill)
    y_ref[...] = y.reshape(y_ref.shape)
    while queue:
        fill()

    @pl.when(j == pl.num_programs(1) - 1)
    def _():
        _store_states(s_scr, s_out_ref, bb)
        shift_ref[...] = prev_scr[...]
        for b in range(bb):
            kc_ref[b] = attn[b]["kn"].T
            vc_ref[b] = v[b * WINDOW:(b + 1) * WINDOW].T


def _prompt_layer(x, p, gn, w_in_b, vecs, qw, kw, sinks_b, w_out_b, gp, w_gate_b, w_proj_b):
    batch, seq, _ = x.shape
    bb = PROMPT_ROWS
    nblk = seq // WINDOW
    const = lambda r, w: pl.BlockSpec((r, w), lambda g, j: (0, 0))
    per_row = lambda *shape: pl.BlockSpec((bb,) + shape, lambda g, j: (g,) + (0,) * len(shape))
    in_specs = ([pl.BlockSpec((bb, WINDOW, D_MODEL), lambda g, j: (g, jnp.minimum(j, nblk - 1), 0)),
                 pl.BlockSpec((bb, WINDOW, D_PLE), lambda g, j: (g, jnp.maximum(j - 1, 0), 0)),
                 const(1, D_MODEL), const(D_MODEL, D_IN)]
                + _vec_specs(lambda g, j: (0, 0))
                + [const(1, D_ATTN), const(1, D_KV), const(N_Q_HEADS, LANES), const(D_MODEL, D_MODEL),
                   const(1, D_MODEL), const(D_MODEL, D_MODEL), const(D_PLE, D_MODEL)])
    return pl.pallas_call(
        _prompt_layer_kernel,
        grid=(batch // bb, nblk + 1),
        in_specs=in_specs,
        out_specs=[pl.BlockSpec((bb, WINDOW, D_MODEL), lambda g, j: (g, jnp.maximum(j - 1, 0), 0)),
                   per_row(2 * N_PAIRS, HEAD_DIM, HEAD_DIM), per_row(1, D_SHIFT),
                   per_row(D_KV, WINDOW), per_row(D_KV, WINDOW)],
        out_shape=[jax.ShapeDtypeStruct((batch, seq, D_MODEL), F32),
                   jax.ShapeDtypeStruct((batch, 2 * N_PAIRS, HEAD_DIM, HEAD_DIM), F32),
                   jax.ShapeDtypeStruct((batch, 1, D_SHIFT), F32),
                   jax.ShapeDtypeStruct((batch, D_KV, WINDOW), F32),
                   jax.ShapeDtypeStruct((batch, D_KV, WINDOW), F32)],
        scratch_shapes=[pltpu.VMEM((2, bb * WINDOW, D_IN), F32), pltpu.VMEM((2, bb * WINDOW, D_MODEL), F32),
                        pltpu.VMEM((bb * N_PAIRS, LANES, LANES), F32), pltpu.VMEM((bb, 1, D_SHIFT), F32),
                        pltpu.VMEM((bb, WINDOW, D_KV), BF16), pltpu.VMEM((bb, WINDOW, D_KV), BF16),
                        pltpu.VMEM((bb, D_KV, WINDOW), BF16)],
        compiler_params=pltpu.CompilerParams(dimension_semantics=("arbitrary", "arbitrary"),
                                             vmem_limit_bytes=VMEM_LIMIT),
        name="prompt_layer",
    )(x, p, gn, w_in_b, *vecs, qw, kw, sinks_b, w_out_b, gp, w_gate_b, w_proj_b)


def kernel(x_prompt, x_sample, state_rwkv, state_shift, cache_k, cache_v, p_prompt, p_sample, g_norm, w_in, mu_shift, w0, w_dec2, a0, w_a2, k_k, k_a, r_k, lnx_w, lnx_b, q_norm_w, k_norm_w, sinks, w_out, g_ple, w_ple_gate, w_ple_proj):
    depth = w_in.shape[0]
    bp, seq, _ = x_prompt.shape
    bs, dec, _ = x_sample.shape
    wb = cache_k.shape[2]
    xp = x_prompt
    xs = x_sample.reshape(bs * dec, D_MODEL)
    outs = [[] for _ in range(8)]
    for i in range(depth):
        w_in_b = w_in[i].astype(BF16)
        w_out_b = w_out[i].astype(BF16)
        w_gate_b = w_ple_gate[i].astype(BF16)
        w_proj_b = w_ple_proj[i].astype(BF16)
        zl = jnp.zeros((D_LORA, D_RWKV), F32)
        lora_w = jnp.concatenate([jnp.concatenate([w_dec2[i], zl], axis=1),
                                  jnp.concatenate([zl, w_a2[i]], axis=1)], axis=0).astype(BF16)
        row = lambda t: t.reshape(1, -1)
        vecs = (row(mu_shift[i]), row(w0[i]), row(a0[i]), lora_w, row(k_k[i]), row(k_a[i]), row(r_k[i]),
                row(lnx_w[i]), row(lnx_b[i]))
        qw = row(jnp.tile(q_norm_w[i], N_Q_HEADS))
        kw = row(jnp.tile(k_norm_w[i], N_KV_HEADS))
        sinks_b = jnp.broadcast_to(sinks[i][:, None], (N_Q_HEADS, LANES))
        gn, gp = row(g_norm[i]), row(g_ple[i])

        to_t = lambda c: jnp.transpose(c, (0, 2, 3, 1)).reshape(c.shape[0], D_KV, c.shape[1])
        from_t = lambda c: jnp.transpose(c.reshape(c.shape[0], N_KV_HEADS, HEAD_DIM, c.shape[2]), (0, 3, 1, 2))

        xp, s_p, sh_p, kc, vc = _prompt_layer(xp, p_prompt[i], gn, w_in_b, vecs, qw, kw, sinks_b, w_out_b, gp,
                                              w_gate_b, w_proj_b)
        outs[0].append(s_p)
        outs[2].append(sh_p)
        outs[4].append(from_t(kc))
        outs[6].append(from_t(vc))

        f, z_r, q, k, v, z_a = _in_proj(xs, gn, w_in_b, 512)
        o_r, s_s = _rwkv_sample(f, state_shift[i], state_rwkv[i], vecs, bs, SAMPLE_GROUP, dec)
        o_a, k_buf, v_buf = _attn_sample(q.reshape(bs, dec, D_ATTN), k.reshape(bs, dec, D_KV), v.reshape(bs, dec, D_KV),
                                         to_t(cache_k[i]), to_t(cache_v[i]), qw, kw, jnp.repeat(sinks_b, dec, axis=0))
        outs[1].append(s_s)
        outs[3].append(f.reshape(bs, dec, D_SHIFT)[:, -1:])
        outs[5].append(from_t(k_buf))
        outs[7].append(from_t(v_buf))
        xs = _merge(xs, o_r, z_r, o_a.reshape(bs * dec, D_ATTN), z_a, p_sample[i].reshape(bs * dec, D_PLE),
                    w_out_b, gp, w_gate_b, w_proj_b, 512)
    st = lambda l: jnp.stack(l)
    return (xp, xs.reshape(bs, dec, D_MODEL),
            st(outs[0]), st(outs[1]), st(outs[2]), st(outs[3]), st(outs[4]), st(outs[5]), st(outs[6]), st(outs[7]))
```

```python
import functools
import math

import jax
import jax.numpy as jnp
from jax import lax
from jax.experimental import pallas as pl
from jax.experimental.pallas import tpu as pltpu

F32 = jnp.float32
BF16 = jnp.bfloat16

D_MODEL = 1024
HEAD_DIM = 64
D_RWKV = 512
D_ATTN = 512
N_KV_HEADS = 2
N_Q_HEADS = 8
Q_PER_KV = N_Q_HEADS // N_KV_HEADS
D_KV = N_KV_HEADS * HEAD_DIM
WINDOW = 128
D_LORA = 64
D_SHIFT = 3 * D_RWKV + 2 * D_LORA
D_PLE = 256
D_IN = D_SHIFT + D_RWKV + D_ATTN + 2 * D_KV + D_ATTN
NORM_EPS = 1e-6
LNX_EPS = 64e-5
NEG_INF = -1e30

LANES = 128
N_PAIRS = D_RWKV // LANES
PROMPT_CHUNK = 64
SAMPLE_GROUP = 16
VMEM_LIMIT = 56 * 1024 * 1024


def _dot(a, b):
    return jnp.dot(a.astype(BF16), b.astype(BF16), preferred_element_type=F32)


def _dot_nt(a, b):
    return lax.dot_general(a.astype(BF16), b.astype(BF16), (((1,), (1,)), ((), ())), preferred_element_type=F32)


def _dot_tn(a, b):
    return lax.dot_general(a.astype(BF16), b.astype(BF16), (((0,), (0,)), ((), ())), preferred_element_type=F32)


def _segment_cumsum(x, seg):
    blk = min(max(seg, HEAD_DIM), x.shape[0])
    ti = lax.broadcasted_iota(jnp.int32, (blk, 3 * blk), 0)
    tj = lax.broadcasted_iota(jnp.int32, (blk, 3 * blk), 1) % blk
    tri3 = ((ti // seg == tj // seg) & (tj <= ti)).astype(BF16)
    hi = x.astype(BF16)
    r1 = x - hi.astype(F32)
    mid = r1.astype(BF16)
    lo = (r1 - mid.astype(F32)).astype(BF16)
    parts = []
    for r0 in range(0, x.shape[0], blk):
        rows = slice(r0, r0 + blk)
        parts.append(jnp.dot(tri3, jnp.concatenate([hi[rows], mid[rows], lo[rows]], axis=0),
                             preferred_element_type=F32))
    return parts[0] if len(parts) == 1 else jnp.concatenate(parts, axis=0)


def _head_ones():
    r = (lax.broadcasted_iota(jnp.int32, (2 * LANES, LANES), 0) % LANES) // HEAD_DIM
    c = lax.broadcasted_iota(jnp.int32, (2 * LANES, LANES), 1) // HEAD_DIM
    return (r == c).astype(BF16)


def _head_sum(x, ones):
    tiles = []
    for i in range(0, x.shape[1], LANES):
        xt = x[:, i:i + LANES]
        hi = xt.astype(BF16)
        lo = (xt - hi.astype(F32)).astype(BF16)
        tiles.append(jnp.dot(jnp.concatenate([hi, lo], axis=1), ones, preferred_element_type=F32))
    return tiles[0] if len(tiles) == 1 else jnp.concatenate(tiles, axis=1)


def _rms_rows(x, g):
    return x * lax.rsqrt(jnp.mean(x * x, axis=-1, keepdims=True) + NORM_EPS) * g


_IN_SPLITS = (D_SHIFT, D_RWKV, D_ATTN, D_KV, D_KV, D_ATTN)


def _in_proj_kernel(x_ref, g_ref, w_ref, *out_refs):
    h = _dot(_rms_rows(x_ref[...], g_ref[...]), w_ref[...])
    off = 0
    for o_ref, width in zip(out_refs, _IN_SPLITS):
        o_ref[...] = h[:, off:off + width]
        off += width


def _in_proj(x, g_norm, w_in_bf16, tm):
    m = x.shape[0]
    return pl.pallas_call(
        _in_proj_kernel,
        grid=(m // tm,),
        in_specs=[pl.BlockSpec((tm, D_MODEL), lambda i: (i, 0)),
                  pl.BlockSpec((1, D_MODEL), lambda i: (0, 0)),
                  pl.BlockSpec((D_MODEL, D_IN), lambda i: (0, 0))],
        out_specs=[pl.BlockSpec((tm, w), lambda i: (i, 0)) for w in _IN_SPLITS],
        out_shape=[jax.ShapeDtypeStruct((m, w), F32) for w in _IN_SPLITS],
        compiler_params=pltpu.CompilerParams(dimension_semantics=("arbitrary",), vmem_limit_bytes=VMEM_LIMIT),
        name="in_proj",
    )(x, g_norm, w_in_bf16)


def _stack(z, half0):
    return jnp.concatenate([jnp.where(half0, z, 0.0), jnp.where(half0, 0.0, z)], axis=0).astype(BF16)


def _rwkv_recurrence(at, rt, bt, kt, v, e_cum, s_scr, cm, seg, segs_per_state, fill):
    n_rows = at.shape[0]
    n_blk = n_rows // cm
    lane = lax.broadcasted_iota(jnp.int32, (1, LANES), 1)
    half0 = lane < HEAD_DIM
    ri = lax.broadcasted_iota(jnp.int32, (cm, LANES), 0)
    ci = lax.broadcasted_iota(jnp.int32, (cm, LANES), 1) % cm
    same = (ri // seg) == (ci // seg)
    tri_strict = same & (ci < ri)
    tri_incl = same & (ci <= ri)
    eye_c = (ri == ci).astype(F32)
    sr = lax.broadcasted_iota(jnp.int32, (LANES, LANES), 0) // HEAD_DIM
    sc = lax.broadcasted_iota(jnp.int32, (LANES, LANES), 1) // HEAD_DIM
    state_mask = sr == sc
    n_levels = max(int(math.log2(seg)) - 1, 0)
    blocks = [(rb, p) for rb in range(n_blk) for p in range(N_PAIRS)]

    def tile(x, rb, p):
        return x[rb * cm:(rb + 1) * cm, p * LANES:(p + 1) * LANES]

    ops = {}
    for rb, p in blocks:
        at_p, rt_p, bt_p, kt_p, v_p = (tile(x, rb, p) for x in (at, rt, bt, kt, v))
        ops[rb, p] = dict(at=at_p, rt=rt_p, bt=bt_p, kt=kt_p, v=v_p, v_s=_stack(v_p, half0))
    for blk in blocks:
        o = ops[blk]
        g = _dot_nt(jnp.concatenate([o["at"], o["rt"]], axis=0),
                    jnp.concatenate([_stack(o["bt"], half0), _stack(o["kt"], half0)], axis=0))
        o["g_ab"] = jnp.where(tri_strict, g[:cm, :LANES], 0.0)
        o["g_ak"] = jnp.where(tri_strict, g[:cm, LANES:], 0.0)
        o["g_r"] = jnp.concatenate([jnp.where(tri_incl, g[cm:, :LANES], 0.0),
                                    jnp.where(tri_incl, g[cm:, LANES:], 0.0)], axis=1).astype(BF16)
    fill()

    for blk in blocks:
        ops[blk]["t_inv"] = eye_c + ops[blk]["g_ab"]
    if n_levels > 0:
        for blk in blocks:
            ops[blk]["a_pow"] = _dot(ops[blk]["g_ab"], _stack(ops[blk]["g_ab"], half0))
        fill()
        for lvl in range(n_levels):
            last = lvl == n_levels - 1
            for blk in blocks:
                o = ops[blk]
                if last:
                    o["t_inv"] = o["t_inv"] + _dot(o["a_pow"], _stack(o["t_inv"], half0))
                else:
                    m = _dot(o["a_pow"], jnp.concatenate([_stack(o["a_pow"], half0), _stack(o["t_inv"], half0)], axis=1))
                    o["t_inv"] = o["t_inv"] + m[:, LANES:]
                    o["a_pow"] = m[:, :LANES]
            fill()

    for blk in blocks:
        o = ops[blk]
        gakv = _dot(o["g_ak"], o["v_s"])
        z = _dot(o["t_inv"], jnp.concatenate([_stack(o["at"], half0), _stack(gakv, half0)], axis=1))
        o["a_hat"], o["p0"] = z[:, :LANES], z[:, LANES:]
    fill()

    n_seg = n_rows // seg
    n_states = n_seg // segs_per_state
    per_blk = cm // seg
    p_parts = {blk: [None] * per_blk for blk in blocks}
    y_parts = {blk: [None] * per_blk for blk in blocks}
    for step in range(segs_per_state):
        segs = [st * segs_per_state + step for st in range(n_states)]
        proj = {}
        for g_i in segs:
            rb, off = (g_i * seg) // cm, (g_i * seg) % cm
            for p in range(N_PAIRS):
                o = ops[rb, p]
                lhs = jnp.concatenate([o["a_hat"][off:off + seg], o["rt"][off:off + seg]], axis=0)
                proj[g_i, p] = _dot_nt(lhs, s_scr[(g_i // segs_per_state) * N_PAIRS + p])
        for g_i in segs:
            rb, off = (g_i * seg) // cm, (g_i * seg) % cm
            for p in range(N_PAIRS):
                o = ops[rb, p]
                p_seg = proj[g_i, p][:seg] + o["p0"][off:off + seg]
                p_parts[rb, p][off // seg] = p_seg
                y_parts[rb, p][off // seg] = proj[g_i, p][seg:]
                upd = _dot_tn(jnp.concatenate([p_seg, o["v"][off:off + seg]], axis=0),
                              jnp.concatenate([o["bt"][off:off + seg], o["kt"][off:off + seg]], axis=0))
                si = (g_i // segs_per_state) * N_PAIRS + p
                row_end = g_i * seg + seg - 1
                w_end = e_cum[row_end:row_end + 1, p * LANES:(p + 1) * LANES]
                s_scr[si] = w_end * (s_scr[si] + jnp.where(state_mask, upd, 0.0))

    cat = lambda parts: parts[0] if len(parts) == 1 else jnp.concatenate(parts, axis=0)
    rows = []
    for rb in range(n_blk):
        tiles = []
        for p in range(N_PAIRS):
            o = ops[rb, p]
            pv_s = jnp.concatenate([_stack(cat(p_parts[rb, p]), half0), o["v_s"]], axis=0)
            tiles.append(cat(y_parts[rb, p]) + jnp.dot(o["g_r"], pv_s, preferred_element_type=F32))
        rows.append(jnp.concatenate(tiles, axis=1))
    return cat(rows)


def _time_mix(f, prev_rows, tb, seg, cm, vec_refs, s_scr, fill=lambda: None):
    mu_ref, w0_ref, a0_ref, lora_ref, kk_ref, ka_ref, rk_ref, lnw_ref, lnb_ref = vec_refs
    n_rows = f.shape[0]
    row = lax.broadcasted_iota(jnp.int32, (n_rows, 1), 0)
    f_prev = pltpu.roll(f, 1, 0)
    for b, prev in enumerate(prev_rows):
        f_prev = jnp.where(row == b * tb, prev, f_prev)
    fs = f + (f_prev - f) * mu_ref[...]
    r = fs[:, 0:D_RWKV]
    k = fs[:, D_RWKV:2 * D_RWKV]
    v = fs[:, 2 * D_RWKV:3 * D_RWKV]
    wa = fs[:, 3 * D_RWKV:D_SHIFT]
    lane = lax.broadcasted_iota(jnp.int32, (1, LANES), 1)
    lora = _dot(jnp.where(lane < D_LORA, jnp.tanh(wa), wa), lora_ref[...])
    lw = (-math.exp(-0.5)) * jax.nn.sigmoid(w0_ref[...] + lora[:, 0:D_RWKV])
    a_sig = jax.nn.sigmoid(a0_ref[...] + lora[:, D_RWKV:2 * D_RWKV])
    ones = _head_ones()
    kk = k * kk_ref[...]
    kk = kk * lax.rsqrt(jnp.maximum(_head_sum(kk * kk, ones), 1e-24))
    k2 = k * (1.0 + (a_sig - 1.0) * ka_ref[...])

    cum = _segment_cumsum(lw, seg)
    e_cum = jnp.exp(cum)
    e_inv = jnp.exp(-cum)
    y = _rwkv_recurrence(-kk * jnp.exp(cum - lw), r * e_cum, kk * a_sig * e_inv, k2 * e_inv, v, e_cum,
                         s_scr, cm, seg, tb // seg, fill)

    inv_n = 1.0 / HEAD_DIM
    yc = y - _head_sum(y, ones) * inv_n
    var = _head_sum(yc * yc, ones) * inv_n
    yn = yc * lax.rsqrt(var + LNX_EPS) * lnw_ref[...] + lnb_ref[...]
    bonus = _head_sum(r * k2 * rk_ref[...], ones) * v
    return yn + bonus


def _load_states(s0_ref, s_scr, bb):
    zero = jnp.zeros((HEAD_DIM, HEAD_DIM), F32)
    for b in range(bb):
        for p in range(N_PAIRS):
            top = jnp.concatenate([s0_ref[b, 2 * p], zero], axis=1)
            bot = jnp.concatenate([zero, s0_ref[b, 2 * p + 1]], axis=1)
            s_scr[b * N_PAIRS + p] = jnp.concatenate([top, bot], axis=0)


def _store_states(s_scr, s_out_ref, bb):
    for b in range(bb):
        for p in range(N_PAIRS):
            s = s_scr[b * N_PAIRS + p]
            s_out_ref[b, 2 * p] = s[:HEAD_DIM, :HEAD_DIM]
            s_out_ref[b, 2 * p + 1] = s[HEAD_DIM:, HEAD_DIM:]


def _rwkv_sample_kernel(bb, tb, f_ref, prev0_ref, *rest):
    vec_refs, (s0_ref, o_ref, s_out_ref, s_scr) = rest[:9], rest[9:]
    _load_states(s0_ref, s_scr, bb)
    o_ref[...] = _time_mix(f_ref[...], [prev0_ref[b] for b in range(bb)], tb, tb, HEAD_DIM, vec_refs, s_scr)
    _store_states(s_scr, s_out_ref, bb)


def _vec_specs(index_map):
    vec = lambda n: pl.BlockSpec((1, n), index_map)
    return [vec(D_SHIFT), vec(D_RWKV), vec(D_RWKV), pl.BlockSpec((LANES, 2 * D_RWKV), index_map),
            vec(D_RWKV), vec(D_RWKV), vec(D_RWKV), vec(D_RWKV), vec(D_RWKV)]


def _rwkv_sample(f, prev0, s0, vecs, batch, bb, tb):
    n_rows = bb * tb
    assert n_rows % HEAD_DIM == 0 and HEAD_DIM % tb == 0
    state_spec = pl.BlockSpec((bb, 2 * N_PAIRS, HEAD_DIM, HEAD_DIM), lambda b: (b, 0, 0, 0))
    return pl.pallas_call(
        functools.partial(_rwkv_sample_kernel, bb, tb),
        grid=(batch // bb,),
        in_specs=[pl.BlockSpec((n_rows, D_SHIFT), lambda b: (b, 0)),
                  pl.BlockSpec((bb, 1, D_SHIFT), lambda b: (b, 0, 0))] + _vec_specs(lambda b: (0, 0)) + [state_spec],
        out_specs=[pl.BlockSpec((n_rows, D_RWKV), lambda b: (b, 0)), state_spec],
        out_shape=[jax.ShapeDtypeStruct((batch * tb, D_RWKV), F32),
                   jax.ShapeDtypeStruct((batch, 2 * N_PAIRS, HEAD_DIM, HEAD_DIM), F32)],
        scratch_shapes=[pltpu.VMEM((bb * N_PAIRS, LANES, LANES), F32)],
        compiler_params=pltpu.CompilerParams(dimension_semantics=("arbitrary",), vmem_limit_bytes=VMEM_LIMIT),
        name="rwkv_sample",
    )(f, prev0, *vecs, s0)


def _qk_norm(q, k, qw, kw, ones):
    inv_n = 1.0 / HEAD_DIM
    qn = q * lax.rsqrt(_head_sum(q * q, ones) * inv_n + NORM_EPS) * (qw * (HEAD_DIM ** -0.5))
    kn = k * lax.rsqrt(_head_sum(k * k, ones) * inv_n + NORM_EPS) * kw
    return qn, kn


def _swa_prompt_steps(q, k, v, qw, kw, sinks_ref, kprev, kprev_rot, vprev_t, has_prev, out):
    ones = _head_ones()
    half0 = lax.broadcasted_iota(jnp.int32, (1, LANES), 1) < HEAD_DIM
    qn, kn = _qk_norm(q, k, qw, kw, ones)
    kn_b = kn.astype(BF16)
    kn_rot = pltpu.roll(kn, HEAD_DIM, 1).astype(BF16)
    v_t = v.T.astype(BF16)
    keys = jnp.concatenate([kprev[...], kn_b], axis=0)
    keys_rot = jnp.concatenate([kprev_rot[...], kn_rot], axis=0)
    vals_t = jnp.concatenate([vprev_t[...], v_t], axis=1)
    kprev[...] = kn_b
    kprev_rot[...] = kn_rot
    vprev_t[...] = v_t
    kj = lax.broadcasted_iota(jnp.int32, (2 * WINDOW, WINDOW), 0)
    qi = lax.broadcasted_iota(jnp.int32, (2 * WINDOW, WINDOW), 1)
    valid_t = (kj > qi) & (kj <= qi + WINDOW) & ((kj >= WINDOW) | has_prev)
    heads = range(N_Q_HEADS)
    yield
    qm = [jnp.where(half0 if h % 2 == 0 else jnp.logical_not(half0), qn[:, (h // 2) * LANES:(h // 2 + 1) * LANES],
                    0.0).astype(BF16) for h in heads]
    straight = [h for h in heads if h // Q_PER_KV == h % 2]
    swapped = [h for h in heads if h // Q_PER_KV != h % 2]
    scores = [None] * N_Q_HEADS
    for group, kmat in ((straight, keys), (swapped, keys_rot)):
        for h0, h1 in zip(group[0::2], group[1::2]):
            s2 = _dot_nt(kmat, jnp.concatenate([qm[h0], qm[h1]], axis=0))
            scores[h0] = jnp.where(valid_t, s2[:, :WINDOW], NEG_INF)
            scores[h1] = jnp.where(valid_t, s2[:, WINDOW:], NEG_INF)
    yield
    probs = []
    for h in heads:
        sink = sinks_ref[h:h + 1, 0:1]
        m = jnp.maximum(jnp.max(scores[h], axis=0, keepdims=True), sink)
        pr = jnp.exp(scores[h] - m)
        denom = jnp.sum(pr, axis=0, keepdims=True) + jnp.exp(sink - m)
        probs.append((pr * (1.0 / denom)).astype(BF16))
    yield
    tiles = []
    for t in range(D_ATTN // LANES):
        g = (2 * t) // Q_PER_KV
        o2 = jnp.dot(vals_t[g * HEAD_DIM:(g + 1) * HEAD_DIM, :], jnp.concatenate(probs[2 * t:2 * t + 2], axis=1),
                     preferred_element_type=F32)
        tiles.append(jnp.concatenate([o2[:, :WINDOW], o2[:, WINDOW:]], axis=0).T)
    out["o_a"] = jnp.concatenate(tiles, axis=1)
    out["kn"] = kn


def _attn_sample_kernel(t_new, q_ref, k_ref, v_ref, ckt_ref, cvt_ref, qw_ref, kw_ref, sinks_ref,
                        o_ref, kot_ref, vot_ref):
    n_seq = q_ref.shape[0]
    wb = ckt_ref.shape[2]
    rows = N_Q_HEADS * t_new
    ones = _head_ones()
    half0 = lax.broadcasted_iota(jnp.int32, (1, LANES), 1) < HEAD_DIM
    qi_c = lax.broadcasted_iota(jnp.int32, (rows, wb), 0) % t_new
    kj_c = lax.broadcasted_iota(jnp.int32, (rows, wb), 1)
    valid_c = (qi_c + wb - kj_c) < WINDOW
    qi_n = lax.broadcasted_iota(jnp.int32, (rows, t_new), 0) % t_new
    kj_n = lax.broadcasted_iota(jnp.int32, (rows, t_new), 1)
    valid_n = kj_n <= qi_n
    sink = sinks_ref[:, 0:1]
    heads = range(N_Q_HEADS)
    swap = [h // Q_PER_KV != h % 2 for h in heads]

    keep = lax.broadcasted_iota(jnp.int32, (1, wb), 1) < wb - t_new
    sel_t = lax.broadcasted_iota(jnp.int32, (3 * t_new, wb), 0) % t_new
    sel_l = lax.broadcasted_iota(jnp.int32, (3 * t_new, wb), 1)
    sel3 = (sel_l == sel_t + (wb - t_new)).astype(BF16)

    def _place_new(x):
        hi = x.astype(BF16)
        r1 = x - hi.astype(F32)
        mid = r1.astype(BF16)
        lo = (r1 - mid.astype(F32)).astype(BF16)
        return lax.dot_general(jnp.concatenate([hi, mid, lo], axis=0), sel3, (((0,), (0,)), ((), ())),
                               preferred_element_type=F32)

    seqs = []
    for i in range(n_seq):
        qn, kn = _qk_norm(q_ref[i], k_ref[i], qw_ref[...], kw_ref[...], ones)
        pieces = []
        for h in heads:
            qm = jnp.where(half0 if h % 2 == 0 else jnp.logical_not(half0), qn[:, (h // 2) * LANES:(h // 2 + 1) * LANES],
                           0.0)
            pieces.append(pltpu.roll(qm, HEAD_DIM, 1) if swap[h] else qm)
        seqs.append(dict(q=jnp.concatenate(pieces, axis=0).astype(BF16), kn=kn, v=v_ref[i],
                         ckt=ckt_ref[i], cvt=cvt_ref[i]))
    for s in seqs:
        s["s_c"] = jnp.where(valid_c, _dot(s["q"], s["ckt"]), NEG_INF)
        s["s_n"] = jnp.where(valid_n, _dot_nt(s["q"], s["kn"]), NEG_INF)
    for s in seqs:
        m = jnp.maximum(jnp.maximum(jnp.max(s["s_c"], axis=-1, keepdims=True),
                                    jnp.max(s["s_n"], axis=-1, keepdims=True)), sink)
        p_c = jnp.exp(s["s_c"] - m)
        p_n = jnp.exp(s["s_n"] - m)
        denom = jnp.sum(p_c, axis=-1, keepdims=True) + jnp.sum(p_n, axis=-1, keepdims=True) + jnp.exp(sink - m)
        s["p_c"], s["p_n"], s["inv"] = p_c, p_n, 1.0 / denom
    for i, s in enumerate(seqs):
        o = (_dot_nt(s["p_c"], s["cvt"]) + _dot(s["p_n"], s["v"])) * s["inv"]
        tiles = []
        for t in range(D_ATTN // LANES):
            pair = []
            for h in (2 * t, 2 * t + 1):
                o_h = o[h * t_new:(h + 1) * t_new]
                pair.append(pltpu.roll(o_h, HEAD_DIM, 1) if swap[h] else o_h)
            tiles.append(jnp.where(half0, pair[0], pair[1]))
        o_ref[i] = jnp.concatenate(tiles, axis=1)
        kot_ref[i] = jnp.where(keep, pltpu.roll(s["ckt"], wb - t_new, 1), _place_new(s["kn"]))
        vot_ref[i] = jnp.where(keep, pltpu.roll(s["cvt"], wb - t_new, 1), _place_new(s["v"]))


def _attn_sample(q, k, v, ckt, cvt, qw, kw, sinks_rows):
    b, t_new, _ = q.shape
    wb = ckt.shape[2]
    gb = SAMPLE_GROUP
    spec = lambda r, w: pl.BlockSpec((gb, r, w), lambda i: (i, 0, 0))
    const = lambda r, w: pl.BlockSpec((r, w), lambda i: (0, 0))
    return pl.pallas_call(
        functools.partial(_attn_sample_kernel, t_new),
        grid=(b // gb,),
        in_specs=[spec(t_new, D_ATTN), spec(t_new, D_KV), spec(t_new, D_KV), spec(D_KV, wb), spec(D_KV, wb),
                  const(1, D_ATTN), const(1, D_KV), const(N_Q_HEADS * t_new, LANES)],
        out_specs=[spec(t_new, D_ATTN), spec(D_KV, wb), spec(D_KV, wb)],
        out_shape=[jax.ShapeDtypeStruct((b, t_new, D_ATTN), F32),
                   jax.ShapeDtypeStruct((b, D_KV, wb), F32),
                   jax.ShapeDtypeStruct((b, D_KV, wb), F32)],
        compiler_params=pltpu.CompilerParams(dimension_semantics=("arbitrary",)),
        name="attn_sample",
    )(q, k, v, ckt, cvt, qw, kw, sinks_rows)


def _merge_rows(x, o_r, z_r, o_a, z_a, p, wout_ref, g_ref, wgate_ref, wproj_ref, fill=lambda: None):
    gr = o_r * jax.nn.silu(z_r)
    ga = o_a * jax.nn.silu(z_a)
    ple = _dot(p, wproj_ref[...])
    h = x + _dot(gr, wout_ref[0:D_RWKV, :]) + _dot(ga, wout_ref[D_RWKV:D_MODEL, :])
    fill()
    gate = jax.nn.sigmoid(_dot(_rms_rows(h, g_ref[...]), wgate_ref[...]))
    fill()
    return h + gate * ple


def _merge_kernel(x_ref, or_ref, zr_ref, oa_ref, za_ref, p_ref, wout_ref, g_ref, wgate_ref, wproj_ref, y_ref):
    y_ref[...] = _merge_rows(x_ref[...], or_ref[...], zr_ref[...], oa_ref[...], za_ref[...], p_ref[...],
                             wout_ref, g_ref, wgate_ref, wproj_ref)


def _merge(x, o_r, z_r, o_a, z_a, p, w_out, g_ple, w_gate, w_proj, tm):
    m = x.shape[0]
    tok = lambda w: pl.BlockSpec((tm, w), lambda i: (i, 0))
    const = lambda r, w: pl.BlockSpec((r, w), lambda i: (0, 0))
    return pl.pallas_call(
        _merge_kernel,
        grid=(m // tm,),
        in_specs=[tok(D_MODEL), tok(D_RWKV), tok(D_RWKV), tok(D_ATTN), tok(D_ATTN), tok(D_PLE),
                  const(D_MODEL, D_MODEL), const(1, D_MODEL), const(D_MODEL, D_MODEL), const(D_PLE, D_MODEL)],
        out_specs=tok(D_MODEL),
        out_shape=jax.ShapeDtypeStruct((m, D_MODEL), F32),
        compiler_params=pltpu.CompilerParams(dimension_semantics=("arbitrary",), vmem_limit_bytes=VMEM_LIMIT),
        name="merge",
    )(x, o_r, z_r, o_a, z_a, p, w_out, g_ple, w_gate, w_proj)


_IN_CHUNK = 512
PROMPT_ROWS = 2
_FILL_HEAD = 3
_FILL_TAIL = 2


def _fill_order(n_proj, n_attn):
    mid = n_proj - _FILL_HEAD - _FILL_TAIL
    order = ["p"] * _FILL_HEAD
    done = 0
    for i in range(n_attn):
        while done < mid and done * n_attn <= i * mid:
            order.append("p")
            done += 1
        order.append("a")
    return order + ["p"] * (mid - done + _FILL_TAIL)


def _prompt_layer_kernel(nblk, x_ref, p_ref, gn_ref, win_ref, *rest):
    vec_refs = rest[:9]
    (qw_ref, kw_ref, sinks_ref, wout_ref, gp_ref, wgate_ref, wproj_ref,
     y_ref, s_out_ref, shift_ref, kc_ref, vc_ref,
     hbuf, xbuf, s_scr, prev_scr, kprev, kprev_rot, vprev_t) = rest[9:]
    bb = x_ref.shape[0]
    n_rows = bb * WINDOW
    s = pl.program_id(0)
    slot = s % 2
    j = jnp.maximum(s - 1, 0) % nblk

    @pl.when(s == 0)
    def _():
        hbuf[1] = jnp.zeros(hbuf.shape[1:], F32)
        xbuf[1] = jnp.zeros(xbuf.shape[1:], F32)

    @pl.when(j == 0)
    def _():
        s_scr[...] = jnp.zeros(s_scr.shape, F32)
        prev_scr[...] = jnp.zeros(prev_scr.shape, F32)
        kprev[...] = jnp.zeros(kprev.shape, BF16)
        kprev_rot[...] = jnp.zeros(kprev_rot.shape, BF16)
        vprev_t[...] = jnp.zeros(vprev_t.shape, BF16)

    h_cur = hbuf.at[1 - slot]
    x_new = x_ref[...].reshape(n_rows, D_MODEL)
    xbuf[slot] = x_new
    xn = _rms_rows(x_new, gn_ref[...]).astype(BF16)
    h_new = hbuf.at[slot]

    def project(c0):
        c1 = min(c0 + _IN_CHUNK, D_IN)
        h_new[:, c0:c1] = jnp.dot(xn, win_ref[:, c0:c1], preferred_element_type=F32)

    offs = [0]
    for w in _IN_SPLITS:
        offs.append(offs[-1] + w)
    part = lambda i: h_cur[:, offs[i]:offs[i + 1]]
    q, k, v = part(2), part(3), part(4)
    attn = [{} for _ in range(bb)]
    attn_steps = []
    for b in range(bb):
        rows = slice(b * WINDOW, (b + 1) * WINDOW)
        attn_steps.append(_swa_prompt_steps(q[rows], k[rows], v[rows], qw_ref[...], kw_ref[...], sinks_ref,
                                            kprev.at[b], kprev_rot.at[b], vprev_t.at[b], j > 0, attn[b]))
    proj_items = [functools.partial(project, c0) for c0 in range(0, D_IN, _IN_CHUNK)]
    attn_items = [functools.partial(next, attn_steps[b], None) for _ in range(4) for b in range(bb)]
    queue = [proj_items.pop(0) if kind == "p" else attn_items.pop(0)
             for kind in _fill_order(len(proj_items), len(attn_items))]
    assert not proj_items and not attn_items

    def fill():
        if queue:
            queue.pop(0)()

    for _ in range(_FILL_HEAD):
        fill()
    f = part(0)
    o_r = _time_mix(f, [prev_scr[b] for b in range(bb)], WINDOW, PROMPT_CHUNK, HEAD_DIM, vec_refs, s_scr, fill)
    for b in range(bb):
        prev_scr[b] = f[(b + 1) * WINDOW - 1:(b + 1) * WINDOW, :]
    while len(queue) > _FILL_TAIL:
        fill()
    o_a = jnp.concatenate([attn[b]["o_a"] for b in range(bb)], axis=0)
    y = _merge_rows(xbuf[1 - slot], o_r, part(1), o_a, part(5), p_ref[...].reshape(n_rows, D_PLE), wout_ref, gp_ref,
                    wgate_ref, wproj_ref, fill)
    y_ref[...] = y.reshape(y_ref.shape)
    while queue:
        fill()

    @pl.when((s > 0) & (j == nblk - 1))
    def _():
        _store_states(s_scr, s_out_ref, bb)
        shift_ref[...] = prev_scr[...]
        for b in range(bb):
            kc_ref[b] = attn[b]["kn"].T
            vc_ref[b] = v[b * WINDOW:(b + 1) * WINDOW].T


def _prompt_layer(x, p, gn, w_in_b, vecs, qw, kw, sinks_b, w_out_b, gp, w_gate_b, w_proj_b):
    batch, seq, _ = x.shape
    bb = PROMPT_ROWS
    nblk = seq // WINDOW
    n_steps = (batch // bb) * nblk
    const = lambda r, w: pl.BlockSpec((r, w), lambda s: (0, 0))
    cur = lambda s: jnp.minimum(s, n_steps - 1)
    prv = lambda s: jnp.maximum(s - 1, 0)
    per_row = lambda *shape: pl.BlockSpec((bb,) + shape, lambda s: (prv(s) // nblk,) + (0,) * len(shape))
    in_specs = ([pl.BlockSpec((bb, WINDOW, D_MODEL), lambda s: (cur(s) // nblk, cur(s) % nblk, 0)),
                 pl.BlockSpec((bb, WINDOW, D_PLE), lambda s: (prv(s) // nblk, prv(s) % nblk, 0)),
                 const(1, D_MODEL), const(D_MODEL, D_IN)]
                + _vec_specs(lambda s: (0, 0))
                + [const(1, D_ATTN), const(1, D_KV), const(N_Q_HEADS, LANES), const(D_MODEL, D_MODEL),
                   const(1, D_MODEL), const(D_MODEL, D_MODEL), const(D_PLE, D_MODEL)])
    return pl.pallas_call(
        functools.partial(_prompt_layer_kernel, nblk),
        grid=(n_steps + 1,),
        in_specs=in_specs,
        out_specs=[pl.BlockSpec((bb, WINDOW, D_MODEL), lambda s: (prv(s) // nblk, prv(s) % nblk, 0)),
                   per_row(2 * N_PAIRS, HEAD_DIM, HEAD_DIM), per_row(1, D_SHIFT),
                   per_row(D_KV, WINDOW), per_row(D_KV, WINDOW)],
        out_shape=[jax.ShapeDtypeStruct((batch, seq, D_MODEL), F32),
                   jax.ShapeDtypeStruct((batch, 2 * N_PAIRS, HEAD_DIM, HEAD_DIM), F32),
                   jax.ShapeDtypeStruct((batch, 1, D_SHIFT), F32),
                   jax.ShapeDtypeStruct((batch, D_KV, WINDOW), F32),
                   jax.ShapeDtypeStruct((batch, D_KV, WINDOW), F32)],
        scratch_shapes=[pltpu.VMEM((2, bb * WINDOW, D_IN), F32), pltpu.VMEM((2, bb * WINDOW, D_MODEL), F32),
                        pltpu.VMEM((bb * N_PAIRS, LANES, LANES), F32), pltpu.VMEM((bb, 1, D_SHIFT), F32),
                        pltpu.VMEM((bb, WINDOW, D_KV), BF16), pltpu.VMEM((bb, WINDOW, D_KV), BF16),
                        pltpu.VMEM((bb, D_KV, WINDOW), BF16)],
        compiler_params=pltpu.CompilerParams(dimension_semantics=("arbitrary",), vmem_limit_bytes=VMEM_LIMIT),
        name="prompt_layer",
    )(x, p, gn, w_in_b, *vecs, qw, kw, sinks_b, w_out_b, gp, w_gate_b, w_proj_b)


def kernel(x_prompt, x_sample, state_rwkv, state_shift, cache_k, cache_v, p_prompt, p_sample, g_norm, w_in, mu_shift, w0, w_dec2, a0, w_a2, k_k, k_a, r_k, lnx_w, lnx_b, q_norm_w, k_norm_w, sinks, w_out, g_ple, w_ple_gate, w_ple_proj):
    depth = w_in.shape[0]
    bp, seq, _ = x_prompt.shape
    bs, dec, _ = x_sample.shape
    xp = x_prompt
    xs = x_sample.reshape(bs * dec, D_MODEL)
    outs = [[] for _ in range(8)]
    for i in range(depth):
        w_in_b = w_in[i].astype(BF16)
        w_out_b = w_out[i].astype(BF16)
        w_gate_b = w_ple_gate[i].astype(BF16)
        w_proj_b = w_ple_proj[i].astype(BF16)
        zl = jnp.zeros((D_LORA, D_RWKV), F32)
        lora_w = jnp.concatenate([jnp.concatenate([w_dec2[i], zl], axis=1),
                                  jnp.concatenate([zl, w_a2[i]], axis=1)], axis=0).astype(BF16)
        row = lambda t: t.reshape(1, -1)
        vecs = (row(mu_shift[i]), row(w0[i]), row(a0[i]), lora_w, row(k_k[i]), row(k_a[i]), row(r_k[i]),
                row(lnx_w[i]), row(lnx_b[i]))
        qw = row(jnp.tile(q_norm_w[i], N_Q_HEADS))
        kw = row(jnp.tile(k_norm_w[i], N_KV_HEADS))
        sinks_b = jnp.broadcast_to(sinks[i][:, None], (N_Q_HEADS, LANES))
        gn, gp = row(g_norm[i]), row(g_ple[i])

        to_t = lambda c: jnp.transpose(c, (0, 2, 3, 1)).reshape(c.shape[0], D_KV, c.shape[1])
        from_t = lambda c: jnp.transpose(c.reshape(c.shape[0], N_KV_HEADS, HEAD_DIM, c.shape[2]), (0, 3, 1, 2))

        xp, s_p, sh_p, kc, vc = _prompt_layer(xp, p_prompt[i], gn, w_in_b, vecs, qw, kw, sinks_b, w_out_b, gp,
                                              w_gate_b, w_proj_b)
        outs[0].append(s_p)
        outs[2].append(sh_p)
        outs[4].append(from_t(kc))
        outs[6].append(from_t(vc))

        f, z_r, q, k, v, z_a = _in_proj(xs, gn, w_in_b, 512)
        o_r, s_s = _rwkv_sample(f, state_shift[i], state_rwkv[i], vecs, bs, SAMPLE_GROUP, dec)
        o_a, k_buf, v_buf = _attn_sample(q.reshape(bs, dec, D_ATTN), k.reshape(bs, dec, D_KV), v.reshape(bs, dec, D_KV),
                                         to_t(cache_k[i]), to_t(cache_v[i]), qw, kw, jnp.repeat(sinks_b, dec, axis=0))
        outs[1].append(s_s)
        outs[3].append(f.reshape(bs, dec, D_SHIFT)[:, -1:])
        outs[5].append(from_t(k_buf))
        outs[7].append(from_t(v_buf))
        xs = _merge(xs, o_r, z_r, o_a.reshape(bs * dec, D_ATTN), z_a, p_sample[i].reshape(bs * dec, D_PLE),
                    w_out_b, gp, w_gate_b, w_proj_b, 512)
    st = lambda l: jnp.stack(l)
    return (xp, xs.reshape(bs, dec, D_MODEL),
            st(outs[0]), st(outs[1]), st(outs[2]), st(outs[3]), st(outs[4]), st(outs[5]), st(outs[6]), st(outs[7]))
```

```python
import functools
import math

import jax
import jax.numpy as jnp
from jax import lax
from jax.experimental import pallas as pl
from jax.experimental.pallas import tpu as pltpu

F32 = jnp.float32
BF16 = jnp.bfloat16

D_MODEL = 1024
HEAD_DIM = 64
D_RWKV = 512
D_ATTN = 512
N_KV_HEADS = 2
N_Q_HEADS = 8
Q_PER_KV = N_Q_HEADS // N_KV_HEADS
D_KV = N_KV_HEADS * HEAD_DIM
WINDOW = 128
D_LORA = 64
D_SHIFT = 3 * D_RWKV + 2 * D_LORA
D_PLE = 256
D_IN = D_SHIFT + D_RWKV + D_ATTN + 2 * D_KV + D_ATTN
NORM_EPS = 1e-6
LNX_EPS = 64e-5
NEG_INF = -1e30

LANES = 128
N_PAIRS = D_RWKV // LANES
PROMPT_CHUNK = 64
SAMPLE_GROUP = 16
VMEM_LIMIT = 56 * 1024 * 1024


def _dot(a, b):
    return jnp.dot(a.astype(BF16), b.astype(BF16), preferred_element_type=F32)


def _dot_nt(a, b):
    return lax.dot_general(a.astype(BF16), b.astype(BF16), (((1,), (1,)), ((), ())), preferred_element_type=F32)


def _dot_tn(a, b):
    return lax.dot_general(a.astype(BF16), b.astype(BF16), (((0,), (0,)), ((), ())), preferred_element_type=F32)


def _segment_cumsum(x, seg):
    blk = min(max(seg, HEAD_DIM), x.shape[0])
    ti = lax.broadcasted_iota(jnp.int32, (blk, 3 * blk), 0)
    tj = lax.broadcasted_iota(jnp.int32, (blk, 3 * blk), 1) % blk
    tri3 = ((ti // seg == tj // seg) & (tj <= ti)).astype(BF16)
    hi = x.astype(BF16)
    r1 = x - hi.astype(F32)
    mid = r1.astype(BF16)
    lo = (r1 - mid.astype(F32)).astype(BF16)
    parts = []
    for r0 in range(0, x.shape[0], blk):
        rows = slice(r0, r0 + blk)
        parts.append(jnp.dot(tri3, jnp.concatenate([hi[rows], mid[rows], lo[rows]], axis=0),
                             preferred_element_type=F32))
    return parts[0] if len(parts) == 1 else jnp.concatenate(parts, axis=0)


def _head_ones():
    r = (lax.broadcasted_iota(jnp.int32, (2 * LANES, LANES), 0) % LANES) // HEAD_DIM
    c = lax.broadcasted_iota(jnp.int32, (2 * LANES, LANES), 1) // HEAD_DIM
    return (r == c).astype(BF16)


def _head_sum(x, ones):
    tiles = []
    for i in range(0, x.shape[1], LANES):
        xt = x[:, i:i + LANES]
        hi = xt.astype(BF16)
        lo = (xt - hi.astype(F32)).astype(BF16)
        tiles.append(jnp.dot(jnp.concatenate([hi, lo], axis=1), ones, preferred_element_type=F32))
    return tiles[0] if len(tiles) == 1 else jnp.concatenate(tiles, axis=1)


def _rms_rows(x, g):
    return x * lax.rsqrt(jnp.mean(x * x, axis=-1, keepdims=True) + NORM_EPS) * g


_IN_SPLITS = (D_SHIFT, D_RWKV, D_ATTN, D_KV, D_KV, D_ATTN)


def _in_proj_kernel(x_ref, g_ref, w_ref, *out_refs):
    h = _dot(_rms_rows(x_ref[...], g_ref[...]), w_ref[...])
    off = 0
    for o_ref, width in zip(out_refs, _IN_SPLITS):
        o_ref[...] = h[:, off:off + width]
        off += width


def _in_proj(x, g_norm, w_in_bf16, tm):
    m = x.shape[0]
    return pl.pallas_call(
        _in_proj_kernel,
        grid=(m // tm,),
        in_specs=[pl.BlockSpec((tm, D_MODEL), lambda i: (i, 0)),
                  pl.BlockSpec((1, D_MODEL), lambda i: (0, 0)),
                  pl.BlockSpec((D_MODEL, D_IN), lambda i: (0, 0))],
        out_specs=[pl.BlockSpec((tm, w), lambda i: (i, 0)) for w in _IN_SPLITS],
        out_shape=[jax.ShapeDtypeStruct((m, w), F32) for w in _IN_SPLITS],
        compiler_params=pltpu.CompilerParams(dimension_semantics=("arbitrary",), vmem_limit_bytes=VMEM_LIMIT),
        name="in_proj",
    )(x, g_norm, w_in_bf16)


def _stack(z, half0):
    return jnp.concatenate([jnp.where(half0, z, 0.0), jnp.where(half0, 0.0, z)], axis=0).astype(BF16)


def _rwkv_recurrence(at, rt, bt, kt, v, e_cum, s_scr, cm, seg, segs_per_state, fill):
    n_rows = at.shape[0]
    n_blk = n_rows // cm
    lane = lax.broadcasted_iota(jnp.int32, (1, LANES), 1)
    half0 = lane < HEAD_DIM
    ri = lax.broadcasted_iota(jnp.int32, (cm, LANES), 0)
    ci = lax.broadcasted_iota(jnp.int32, (cm, LANES), 1) % cm
    same = (ri // seg) == (ci // seg)
    tri_strict = same & (ci < ri)
    tri_incl = same & (ci <= ri)
    eye_c = (ri == ci).astype(F32)
    sr = lax.broadcasted_iota(jnp.int32, (LANES, LANES), 0) // HEAD_DIM
    sc = lax.broadcasted_iota(jnp.int32, (LANES, LANES), 1) // HEAD_DIM
    state_mask = sr == sc
    n_levels = max(int(math.log2(seg)) - 1, 0)
    blocks = [(rb, p) for rb in range(n_blk) for p in range(N_PAIRS)]

    def tile(x, rb, p):
        return x[rb * cm:(rb + 1) * cm, p * LANES:(p + 1) * LANES]

    ops = {}
    for rb, p in blocks:
        at_p, rt_p, bt_p, kt_p, v_p = (tile(x, rb, p) for x in (at, rt, bt, kt, v))
        ops[rb, p] = dict(at=at_p, rt=rt_p, bt=bt_p, kt=kt_p, v=v_p, v_s=_stack(v_p, half0))
    for blk in blocks:
        o = ops[blk]
        g = _dot_nt(jnp.concatenate([o["at"], o["rt"]], axis=0),
                    jnp.concatenate([_stack(o["bt"], half0), _stack(o["kt"], half0)], axis=0))
        o["g_ab"] = jnp.where(tri_strict, g[:cm, :LANES], 0.0)
        o["g_ak"] = jnp.where(tri_strict, g[:cm, LANES:], 0.0)
        o["g_r"] = jnp.concatenate([jnp.where(tri_incl, g[cm:, :LANES], 0.0),
                                    jnp.where(tri_incl, g[cm:, LANES:], 0.0)], axis=1).astype(BF16)
    fill()

    for blk in blocks:
        ops[blk]["t_inv"] = eye_c + ops[blk]["g_ab"]
    if n_levels > 0:
        for blk in blocks:
            ops[blk]["a_pow"] = _dot(ops[blk]["g_ab"], _stack(ops[blk]["g_ab"], half0))
        fill()
        for lvl in range(n_levels):
            last = lvl == n_levels - 1
            for blk in blocks:
                o = ops[blk]
                if last:
                    o["t_inv"] = o["t_inv"] + _dot(o["a_pow"], _stack(o["t_inv"], half0))
                else:
                    m = _dot(o["a_pow"], jnp.concatenate([_stack(o["a_pow"], half0), _stack(o["t_inv"], half0)], axis=1))
                    o["t_inv"] = o["t_inv"] + m[:, LANES:]
                    o["a_pow"] = m[:, :LANES]
            fill()

    for blk in blocks:
        o = ops[blk]
        gakv = _dot(o["g_ak"], o["v_s"])
        z = _dot(o["t_inv"], jnp.concatenate([_stack(o["at"], half0), _stack(gakv, half0)], axis=1))
        o["a_hat"], o["p0"] = z[:, :LANES], z[:, LANES:]
    fill()

    n_seg = n_rows // seg
    n_states = n_seg // segs_per_state
    per_blk = cm // seg
    p_parts = {blk: [None] * per_blk for blk in blocks}
    y_parts = {blk: [None] * per_blk for blk in blocks}
    for step in range(segs_per_state):
        segs = [st * segs_per_state + step for st in range(n_states)]
        proj = {}
        for g_i in segs:
            rb, off = (g_i * seg) // cm, (g_i * seg) % cm
            for p in range(N_PAIRS):
                o = ops[rb, p]
                lhs = jnp.concatenate([o["a_hat"][off:off + seg], o["rt"][off:off + seg]], axis=0)
                proj[g_i, p] = _dot_nt(lhs, s_scr[(g_i // segs_per_state) * N_PAIRS + p])
        for g_i in segs:
            rb, off = (g_i * seg) // cm, (g_i * seg) % cm
            for p in range(N_PAIRS):
                o = ops[rb, p]
                p_seg = proj[g_i, p][:seg] + o["p0"][off:off + seg]
                p_parts[rb, p][off // seg] = p_seg
                y_parts[rb, p][off // seg] = proj[g_i, p][seg:]
                upd = _dot_tn(jnp.concatenate([p_seg, o["v"][off:off + seg]], axis=0),
                              jnp.concatenate([o["bt"][off:off + seg], o["kt"][off:off + seg]], axis=0))
                si = (g_i // segs_per_state) * N_PAIRS + p
                row_end = g_i * seg + seg - 1
                w_end = e_cum[row_end:row_end + 1, p * LANES:(p + 1) * LANES]
                s_scr[si] = w_end * (s_scr[si] + jnp.where(state_mask, upd, 0.0))

    cat = lambda parts: parts[0] if len(parts) == 1 else jnp.concatenate(parts, axis=0)
    rows = []
    for rb in range(n_blk):
        tiles = []
        for p in range(N_PAIRS):
            o = ops[rb, p]
            pv_s = jnp.concatenate([_stack(cat(p_parts[rb, p]), half0), o["v_s"]], axis=0)
            tiles.append(cat(y_parts[rb, p]) + jnp.dot(o["g_r"], pv_s, preferred_element_type=F32))
        rows.append(jnp.concatenate(tiles, axis=1))
    return cat(rows)


def _time_mix(f, prev_rows, tb, seg, cm, vec_refs, s_scr, fill=lambda: None):
    mu_ref, w0_ref, a0_ref, lora_ref, kk_ref, ka_ref, rk_ref, lnw_ref, lnb_ref = vec_refs
    n_rows = f.shape[0]
    row = lax.broadcasted_iota(jnp.int32, (n_rows, 1), 0)
    f_prev = pltpu.roll(f, 1, 0)
    for b, prev in enumerate(prev_rows):
        f_prev = jnp.where(row == b * tb, prev, f_prev)
    fs = f + (f_prev - f) * mu_ref[...]
    r = fs[:, 0:D_RWKV]
    k = fs[:, D_RWKV:2 * D_RWKV]
    v = fs[:, 2 * D_RWKV:3 * D_RWKV]
    wa = fs[:, 3 * D_RWKV:D_SHIFT]
    lane = lax.broadcasted_iota(jnp.int32, (1, LANES), 1)
    lora = _dot(jnp.where(lane < D_LORA, jnp.tanh(wa), wa), lora_ref[...])
    lw = (-math.exp(-0.5)) * jax.nn.sigmoid(w0_ref[...] + lora[:, 0:D_RWKV])
    a_sig = jax.nn.sigmoid(a0_ref[...] + lora[:, D_RWKV:2 * D_RWKV])
    ones = _head_ones()
    kk = k * kk_ref[...]
    kk = kk * lax.rsqrt(jnp.maximum(_head_sum(kk * kk, ones), 1e-24))
    k2 = k * (1.0 + (a_sig - 1.0) * ka_ref[...])

    cum = _segment_cumsum(lw, seg)
    e_cum = jnp.exp(cum)
    e_inv = jnp.exp(-cum)
    y = _rwkv_recurrence(-kk * jnp.exp(cum - lw), r * e_cum, kk * a_sig * e_inv, k2 * e_inv, v, e_cum,
                         s_scr, cm, seg, tb // seg, fill)

    inv_n = 1.0 / HEAD_DIM
    yc = y - _head_sum(y, ones) * inv_n
    var = _head_sum(yc * yc, ones) * inv_n
    yn = yc * lax.rsqrt(var + LNX_EPS) * lnw_ref[...] + lnb_ref[...]
    bonus = _head_sum(r * k2 * rk_ref[...], ones) * v
    return yn + bonus


def _load_states(s0_ref, s_scr, bb):
    zero = jnp.zeros((HEAD_DIM, HEAD_DIM), F32)
    for b in range(bb):
        for p in range(N_PAIRS):
            top = jnp.concatenate([s0_ref[b, 2 * p], zero], axis=1)
            bot = jnp.concatenate([zero, s0_ref[b, 2 * p + 1]], axis=1)
            s_scr[b * N_PAIRS + p] = jnp.concatenate([top, bot], axis=0)


def _store_states(s_scr, s_out_ref, bb):
    for b in range(bb):
        for p in range(N_PAIRS):
            s = s_scr[b * N_PAIRS + p]
            s_out_ref[b, 2 * p] = s[:HEAD_DIM, :HEAD_DIM]
            s_out_ref[b, 2 * p + 1] = s[HEAD_DIM:, HEAD_DIM:]


def _rwkv_sample_kernel(bb, tb, f_ref, prev0_ref, *rest):
    vec_refs, (s0_ref, _, o_ref, s_out_ref, s_scr) = rest[:9], rest[9:]
    _load_states(s0_ref, s_scr, bb)
    o_ref[...] = _time_mix(f_ref[...], [prev0_ref[b] for b in range(bb)], tb, tb, HEAD_DIM, vec_refs, s_scr)
    _store_states(s_scr, s_out_ref, bb)


def _vec_specs(index_map):
    vec = lambda n: pl.BlockSpec((1, n), index_map)
    return [vec(D_SHIFT), vec(D_RWKV), vec(D_RWKV), pl.BlockSpec((LANES, 2 * D_RWKV), index_map),
            vec(D_RWKV), vec(D_RWKV), vec(D_RWKV), vec(D_RWKV), vec(D_RWKV)]


def _rwkv_sample(f, prev0, s0, vecs, batch, bb, tb, after):
    n_rows = bb * tb
    assert n_rows % HEAD_DIM == 0 and HEAD_DIM % tb == 0
    state_spec = pl.BlockSpec((bb, 2 * N_PAIRS, HEAD_DIM, HEAD_DIM), lambda b: (b, 0, 0, 0))
    return pl.pallas_call(
        functools.partial(_rwkv_sample_kernel, bb, tb),
        grid=(batch // bb,),
        in_specs=[pl.BlockSpec((n_rows, D_SHIFT), lambda b: (b, 0)),
                  pl.BlockSpec((bb, 1, D_SHIFT), lambda b: (b, 0, 0))] + _vec_specs(lambda b: (0, 0))
        + [state_spec, pl.BlockSpec(memory_space=pl.ANY)],
        out_specs=[pl.BlockSpec((n_rows, D_RWKV), lambda b: (b, 0)), state_spec],
        out_shape=[jax.ShapeDtypeStruct((batch * tb, D_RWKV), F32),
                   jax.ShapeDtypeStruct((batch, 2 * N_PAIRS, HEAD_DIM, HEAD_DIM), F32)],
        scratch_shapes=[pltpu.VMEM((bb * N_PAIRS, LANES, LANES), F32)],
        compiler_params=pltpu.CompilerParams(dimension_semantics=("arbitrary",), vmem_limit_bytes=VMEM_LIMIT),
        name="rwkv_sample",
    )(f, prev0, *vecs, s0, after)


def _qk_norm(q, k, qw, kw, ones):
    inv_n = 1.0 / HEAD_DIM
    qn = q * lax.rsqrt(_head_sum(q * q, ones) * inv_n + NORM_EPS) * (qw * (HEAD_DIM ** -0.5))
    kn = k * lax.rsqrt(_head_sum(k * k, ones) * inv_n + NORM_EPS) * kw
    return qn, kn


def _swa_prompt_steps(q, k, v, qw, kw, sinks_ref, kprev, kprev_rot, vprev_t, has_prev, out):
    ones = _head_ones()
    half0 = lax.broadcasted_iota(jnp.int32, (1, LANES), 1) < HEAD_DIM
    qn, kn = _qk_norm(q, k, qw, kw, ones)
    kn_b = kn.astype(BF16)
    kn_rot = pltpu.roll(kn, HEAD_DIM, 1).astype(BF16)
    v_t = v.T.astype(BF16)
    keys = jnp.concatenate([kprev[...], kn_b], axis=0)
    keys_rot = jnp.concatenate([kprev_rot[...], kn_rot], axis=0)
    vals_t = jnp.concatenate([vprev_t[...], v_t], axis=1)
    kprev[...] = kn_b
    kprev_rot[...] = kn_rot
    vprev_t[...] = v_t
    kj = lax.broadcasted_iota(jnp.int32, (2 * WINDOW, WINDOW), 0)
    qi = lax.broadcasted_iota(jnp.int32, (2 * WINDOW, WINDOW), 1)
    valid_t = (kj > qi) & (kj <= qi + WINDOW) & ((kj >= WINDOW) | has_prev)
    heads = range(N_Q_HEADS)
    yield
    qm = [jnp.where(half0 if h % 2 == 0 else jnp.logical_not(half0), qn[:, (h // 2) * LANES:(h // 2 + 1) * LANES],
                    0.0).astype(BF16) for h in heads]
    straight = [h for h in heads if h // Q_PER_KV == h % 2]
    swapped = [h for h in heads if h // Q_PER_KV != h % 2]
    scores = [None] * N_Q_HEADS
    for group, kmat in ((straight, keys), (swapped, keys_rot)):
        for h0, h1 in zip(group[0::2], group[1::2]):
            s2 = _dot_nt(kmat, jnp.concatenate([qm[h0], qm[h1]], axis=0))
            scores[h0] = jnp.where(valid_t, s2[:, :WINDOW], NEG_INF)
            scores[h1] = jnp.where(valid_t, s2[:, WINDOW:], NEG_INF)
    yield
    probs = []
    for h in heads:
        sink = sinks_ref[h:h + 1, 0:1]
        m = jnp.maximum(jnp.max(scores[h], axis=0, keepdims=True), sink)
        pr = jnp.exp(scores[h] - m)
        denom = jnp.sum(pr, axis=0, keepdims=True) + jnp.exp(sink - m)
        probs.append((pr * (1.0 / denom)).astype(BF16))
    yield
    tiles = []
    for t in range(D_ATTN // LANES):
        g = (2 * t) // Q_PER_KV
        o2 = jnp.dot(vals_t[g * HEAD_DIM:(g + 1) * HEAD_DIM, :], jnp.concatenate(probs[2 * t:2 * t + 2], axis=1),
                     preferred_element_type=F32)
        tiles.append(jnp.concatenate([o2[:, :WINDOW], o2[:, WINDOW:]], axis=0).T)
    out["o_a"] = jnp.concatenate(tiles, axis=1)
    out["kn"] = kn


def _attn_sample_kernel(t_new, q_ref, k_ref, v_ref, ckt_ref, cvt_ref, qw_ref, kw_ref, sinks_ref,
                        o_ref, kot_ref, vot_ref):
    n_seq = q_ref.shape[0]
    wb = ckt_ref.shape[2]
    rows = N_Q_HEADS * t_new
    ones = _head_ones()
    half0 = lax.broadcasted_iota(jnp.int32, (1, LANES), 1) < HEAD_DIM
    qi_c = lax.broadcasted_iota(jnp.int32, (rows, wb), 0) % t_new
    kj_c = lax.broadcasted_iota(jnp.int32, (rows, wb), 1)
    valid_c = (qi_c + wb - kj_c) < WINDOW
    qi_n = lax.broadcasted_iota(jnp.int32, (rows, t_new), 0) % t_new
    kj_n = lax.broadcasted_iota(jnp.int32, (rows, t_new), 1)
    valid_n = kj_n <= qi_n
    sink = sinks_ref[:, 0:1]
    heads = range(N_Q_HEADS)
    swap = [h // Q_PER_KV != h % 2 for h in heads]

    keep = lax.broadcasted_iota(jnp.int32, (1, wb), 1) < wb - t_new
    sel_t = lax.broadcasted_iota(jnp.int32, (3 * t_new, wb), 0) % t_new
    sel_l = lax.broadcasted_iota(jnp.int32, (3 * t_new, wb), 1)
    sel3 = (sel_l == sel_t + (wb - t_new)).astype(BF16)

    def _place_new(x):
        hi = x.astype(BF16)
        r1 = x - hi.astype(F32)
        mid = r1.astype(BF16)
        lo = (r1 - mid.astype(F32)).astype(BF16)
        return lax.dot_general(jnp.concatenate([hi, mid, lo], axis=0), sel3, (((0,), (0,)), ((), ())),
                               preferred_element_type=F32)

    seqs = []
    for i in range(n_seq):
        qn, kn = _qk_norm(q_ref[i], k_ref[i], qw_ref[...], kw_ref[...], ones)
        pieces = []
        for h in heads:
            qm = jnp.where(half0 if h % 2 == 0 else jnp.logical_not(half0), qn[:, (h // 2) * LANES:(h // 2 + 1) * LANES],
                           0.0)
            pieces.append(pltpu.roll(qm, HEAD_DIM, 1) if swap[h] else qm)
        seqs.append(dict(q=jnp.concatenate(pieces, axis=0).astype(BF16), kn=kn, v=v_ref[i],
                         ckt=ckt_ref[i], cvt=cvt_ref[i]))
    for s in seqs:
        s["s_c"] = jnp.where(valid_c, _dot(s["q"], s["ckt"]), NEG_INF)
        s["s_n"] = jnp.where(valid_n, _dot_nt(s["q"], s["kn"]), NEG_INF)
    for s in seqs:
        m = jnp.maximum(jnp.maximum(jnp.max(s["s_c"], axis=-1, keepdims=True),
                                    jnp.max(s["s_n"], axis=-1, keepdims=True)), sink)
        p_c = jnp.exp(s["s_c"] - m)
        p_n = jnp.exp(s["s_n"] - m)
        denom = jnp.sum(p_c, axis=-1, keepdims=True) + jnp.sum(p_n, axis=-1, keepdims=True) + jnp.exp(sink - m)
        s["p_c"], s["p_n"], s["inv"] = p_c, p_n, 1.0 / denom
    for i, s in enumerate(seqs):
        o = (_dot_nt(s["p_c"], s["cvt"]) + _dot(s["p_n"], s["v"])) * s["inv"]
        tiles = []
        for t in range(D_ATTN // LANES):
            pair = []
            for h in (2 * t, 2 * t + 1):
                o_h = o[h * t_new:(h + 1) * t_new]
                pair.append(pltpu.roll(o_h, HEAD_DIM, 1) if swap[h] else o_h)
            tiles.append(jnp.where(half0, pair[0], pair[1]))
        o_ref[i] = jnp.concatenate(tiles, axis=1)
        kot_ref[i] = jnp.where(keep, pltpu.roll(s["ckt"], wb - t_new, 1), _place_new(s["kn"]))
        vot_ref[i] = jnp.where(keep, pltpu.roll(s["cvt"], wb - t_new, 1), _place_new(s["v"]))


def _attn_sample(q, k, v, ckt, cvt, qw, kw, sinks_rows):
    b, t_new, _ = q.shape
    wb = ckt.shape[2]
    gb = SAMPLE_GROUP
    spec = lambda r, w: pl.BlockSpec((gb, r, w), lambda i: (i, 0, 0))
    const = lambda r, w: pl.BlockSpec((r, w), lambda i: (0, 0))
    return pl.pallas_call(
        functools.partial(_attn_sample_kernel, t_new),
        grid=(b // gb,),
        in_specs=[spec(t_new, D_ATTN), spec(t_new, D_KV), spec(t_new, D_KV), spec(D_KV, wb), spec(D_KV, wb),
                  const(1, D_ATTN), const(1, D_KV), const(N_Q_HEADS * t_new, LANES)],
        out_specs=[spec(t_new, D_ATTN), spec(D_KV, wb), spec(D_KV, wb)],
        out_shape=[jax.ShapeDtypeStruct((b, t_new, D_ATTN), F32),
                   jax.ShapeDtypeStruct((b, D_KV, wb), F32),
                   jax.ShapeDtypeStruct((b, D_KV, wb), F32)],
        compiler_params=pltpu.CompilerParams(dimension_semantics=("arbitrary",)),
        name="attn_sample",
    )(q, k, v, ckt, cvt, qw, kw, sinks_rows)


def _merge_rows(x, o_r, z_r, o_a, z_a, p, wout_ref, g_ref, wgate_ref, wproj_ref, fill=lambda: None):
    gr = o_r * jax.nn.silu(z_r)
    ga = o_a * jax.nn.silu(z_a)
    ple = _dot(p, wproj_ref[...])
    h = x + _dot(gr, wout_ref[0:D_RWKV, :]) + _dot(ga, wout_ref[D_RWKV:D_MODEL, :])
    fill()
    gate = jax.nn.sigmoid(_dot(_rms_rows(h, g_ref[...]), wgate_ref[...]))
    fill()
    return h + gate * ple


def _merge_kernel(x_ref, or_ref, zr_ref, oa_ref, za_ref, p_ref, wout_ref, g_ref, wgate_ref, wproj_ref, y_ref):
    y_ref[...] = _merge_rows(x_ref[...], or_ref[...], zr_ref[...], oa_ref[...], za_ref[...], p_ref[...],
                             wout_ref, g_ref, wgate_ref, wproj_ref)


def _merge(x, o_r, z_r, o_a, z_a, p, w_out, g_ple, w_gate, w_proj, tm):
    m = x.shape[0]
    tok = lambda w: pl.BlockSpec((tm, w), lambda i: (i, 0))
    const = lambda r, w: pl.BlockSpec((r, w), lambda i: (0, 0))
    return pl.pallas_call(
        _merge_kernel,
        grid=(m // tm,),
        in_specs=[tok(D_MODEL), tok(D_RWKV), tok(D_RWKV), tok(D_ATTN), tok(D_ATTN), tok(D_PLE),
                  const(D_MODEL, D_MODEL), const(1, D_MODEL), const(D_MODEL, D_MODEL), const(D_PLE, D_MODEL)],
        out_specs=tok(D_MODEL),
        out_shape=jax.ShapeDtypeStruct((m, D_MODEL), F32),
        compiler_params=pltpu.CompilerParams(dimension_semantics=("arbitrary",), vmem_limit_bytes=VMEM_LIMIT),
        name="merge",
    )(x, o_r, z_r, o_a, z_a, p, w_out, g_ple, w_gate, w_proj)


_IN_CHUNK = 512
PROMPT_ROWS = 2
_FILL_HEAD = 3
_FILL_TAIL = 2


def _fill_order(n_proj, n_attn):
    mid = n_proj - _FILL_HEAD - _FILL_TAIL
    order = ["p"] * _FILL_HEAD
    done = 0
    for i in range(n_attn):
        while done < mid and done * n_attn <= i * mid:
            order.append("p")
            done += 1
        order.append("a")
    return order + ["p"] * (mid - done + _FILL_TAIL)


def _prompt_layer_kernel(nblk, x_ref, p_ref, gn_ref, win_ref, *rest):
    vec_refs = rest[:9]
    (qw_ref, kw_ref, sinks_ref, wout_ref, gp_ref, wgate_ref, wproj_ref,
     y_ref, s_out_ref, shift_ref, kc_ref, vc_ref,
     hbuf, xbuf, s_scr, prev_scr, kprev, kprev_rot, vprev_t) = rest[9:]
    bb = x_ref.shape[0]
    n_rows = bb * WINDOW
    s = pl.program_id(0)
    slot = s % 2
    j = jnp.maximum(s - 1, 0) % nblk

    @pl.when(s == 0)
    def _():
        hbuf[1] = jnp.zeros(hbuf.shape[1:], F32)
        xbuf[1] = jnp.zeros(xbuf.shape[1:], F32)

    @pl.when(j == 0)
    def _():
        s_scr[...] = jnp.zeros(s_scr.shape, F32)
        prev_scr[...] = jnp.zeros(prev_scr.shape, F32)
        kprev[...] = jnp.zeros(kprev.shape, BF16)
        kprev_rot[...] = jnp.zeros(kprev_rot.shape, BF16)
        vprev_t[...] = jnp.zeros(vprev_t.shape, BF16)

    h_cur = hbuf.at[1 - slot]
    x_new = x_ref[...].reshape(n_rows, D_MODEL)
    xbuf[slot] = x_new
    xn = _rms_rows(x_new, gn_ref[...]).astype(BF16)
    h_new = hbuf.at[slot]

    def project(c0):
        c1 = min(c0 + _IN_CHUNK, D_IN)
        h_new[:, c0:c1] = jnp.dot(xn, win_ref[:, c0:c1], preferred_element_type=F32)

    offs = [0]
    for w in _IN_SPLITS:
        offs.append(offs[-1] + w)
    part = lambda i: h_cur[:, offs[i]:offs[i + 1]]
    q, k, v = part(2), part(3), part(4)
    attn = [{} for _ in range(bb)]
    attn_steps = []
    for b in range(bb):
        rows = slice(b * WINDOW, (b + 1) * WINDOW)
        attn_steps.append(_swa_prompt_steps(q[rows], k[rows], v[rows], qw_ref[...], kw_ref[...], sinks_ref,
                                            kprev.at[b], kprev_rot.at[b], vprev_t.at[b], j > 0, attn[b]))
    proj_items = [functools.partial(project, c0) for c0 in range(0, D_IN, _IN_CHUNK)]
    attn_items = [functools.partial(next, attn_steps[b], None) for _ in range(4) for b in range(bb)]
    queue = [proj_items.pop(0) if kind == "p" else attn_items.pop(0)
             for kind in _fill_order(len(proj_items), len(attn_items))]
    assert not proj_items and not attn_items

    def fill():
        if queue:
            queue.pop(0)()

    for _ in range(_FILL_HEAD):
        fill()
    f = part(0)
    o_r = _time_mix(f, [prev_scr[b] for b in range(bb)], WINDOW, PROMPT_CHUNK, HEAD_DIM, vec_refs, s_scr, fill)
    for b in range(bb):
        prev_scr[b] = f[(b + 1) * WINDOW - 1:(b + 1) * WINDOW, :]
    while len(queue) > _FILL_TAIL:
        fill()
    o_a = jnp.concatenate([attn[b]["o_a"] for b in range(bb)], axis=0)
    y = _merge_rows(xbuf[1 - slot], o_r, part(1), o_a, part(5), p_ref[...].reshape(n_rows, D_PLE), wout_ref, gp_ref,
                    wgate_ref, wproj_ref, fill)
    y_ref[...] = y.reshape(y_ref.shape)
    while queue:
        fill()

    @pl.when((s > 0) & (j == nblk - 1))
    def _():
        _store_states(s_scr, s_out_ref, bb)
        shift_ref[...] = prev_scr[...]
        for b in range(bb):
            kc_ref[b] = attn[b]["kn"].T
            vc_ref[b] = v[b * WINDOW:(b + 1) * WINDOW].T


def _prompt_layer(x, p, gn, w_in_b, vecs, qw, kw, sinks_b, w_out_b, gp, w_gate_b, w_proj_b):
    batch, seq, _ = x.shape
    bb = PROMPT_ROWS
    nblk = seq // WINDOW
    n_steps = (batch // bb) * nblk
    const = lambda r, w: pl.BlockSpec((r, w), lambda s: (0, 0))
    cur = lambda s: jnp.minimum(s, n_steps - 1)
    prv = lambda s: jnp.maximum(s - 1, 0)
    per_row = lambda *shape: pl.BlockSpec((bb,) + shape, lambda s: (prv(s) // nblk,) + (0,) * len(shape))
    in_specs = ([pl.BlockSpec((bb, WINDOW, D_MODEL), lambda s: (cur(s) // nblk, cur(s) % nblk, 0)),
                 pl.BlockSpec((bb, WINDOW, D_PLE), lambda s: (prv(s) // nblk, prv(s) % nblk, 0)),
                 const(1, D_MODEL), const(D_MODEL, D_IN)]
                + _vec_specs(lambda s: (0, 0))
                + [const(1, D_ATTN), const(1, D_KV), const(N_Q_HEADS, LANES), const(D_MODEL, D_MODEL),
                   const(1, D_MODEL), const(D_MODEL, D_MODEL), const(D_PLE, D_MODEL)])
    return pl.pallas_call(
        functools.partial(_prompt_layer_kernel, nblk),
        grid=(n_steps + 1,),
        in_specs=in_specs,
        out_specs=[pl.BlockSpec((bb, WINDOW, D_MODEL), lambda s: (prv(s) // nblk, prv(s) % nblk, 0)),
                   per_row(2 * N_PAIRS, HEAD_DIM, HEAD_DIM), per_row(1, D_SHIFT),
                   per_row(D_KV, WINDOW), per_row(D_KV, WINDOW)],
        out_shape=[jax.ShapeDtypeStruct((batch, seq, D_MODEL), F32),
                   jax.ShapeDtypeStruct((batch, 2 * N_PAIRS, HEAD_DIM, HEAD_DIM), F32),
                   jax.ShapeDtypeStruct((batch, 1, D_SHIFT), F32),
                   jax.ShapeDtypeStruct((batch, D_KV, WINDOW), F32),
                   jax.ShapeDtypeStruct((batch, D_KV, WINDOW), F32)],
        scratch_shapes=[pltpu.VMEM((2, bb * WINDOW, D_IN), F32), pltpu.VMEM((2, bb * WINDOW, D_MODEL), F32),
                        pltpu.VMEM((bb * N_PAIRS, LANES, LANES), F32), pltpu.VMEM((bb, 1, D_SHIFT), F32),
                        pltpu.VMEM((bb, WINDOW, D_KV), BF16), pltpu.VMEM((bb, WINDOW, D_KV), BF16),
                        pltpu.VMEM((bb, D_KV, WINDOW), BF16)],
        compiler_params=pltpu.CompilerParams(dimension_semantics=("arbitrary",), vmem_limit_bytes=VMEM_LIMIT),
        name="prompt_layer",
    )(x, p, gn, w_in_b, *vecs, qw, kw, sinks_b, w_out_b, gp, w_gate_b, w_proj_b)


def kernel(x_prompt, x_sample, state_rwkv, state_shift, cache_k, cache_v, p_prompt, p_sample, g_norm, w_in, mu_shift, w0, w_dec2, a0, w_a2, k_k, k_a, r_k, lnx_w, lnx_b, q_norm_w, k_norm_w, sinks, w_out, g_ple, w_ple_gate, w_ple_proj):
    depth = w_in.shape[0]
    bp, seq, _ = x_prompt.shape
    bs, dec, _ = x_sample.shape
    xp = x_prompt
    xs = x_sample.reshape(bs * dec, D_MODEL)
    outs = [[] for _ in range(8)]
    for i in range(depth):
        w_in_b = w_in[i].astype(BF16)
        w_out_b = w_out[i].astype(BF16)
        w_gate_b = w_ple_gate[i].astype(BF16)
        w_proj_b = w_ple_proj[i].astype(BF16)
        zl = jnp.zeros((D_LORA, D_RWKV), F32)
        lora_w = jnp.concatenate([jnp.concatenate([w_dec2[i], zl], axis=1),
                                  jnp.concatenate([zl, w_a2[i]], axis=1)], axis=0).astype(BF16)
        row = lambda t: t.reshape(1, -1)
        vecs = (row(mu_shift[i]), row(w0[i]), row(a0[i]), lora_w, row(k_k[i]), row(k_a[i]), row(r_k[i]),
                row(lnx_w[i]), row(lnx_b[i]))
        qw = row(jnp.tile(q_norm_w[i], N_Q_HEADS))
        kw = row(jnp.tile(k_norm_w[i], N_KV_HEADS))
        sinks_b = jnp.broadcast_to(sinks[i][:, None], (N_Q_HEADS, LANES))
        gn, gp = row(g_norm[i]), row(g_ple[i])

        to_t = lambda c: jnp.transpose(c, (0, 2, 3, 1)).reshape(c.shape[0], D_KV, c.shape[1])
        from_t = lambda c: jnp.transpose(c.reshape(c.shape[0], N_KV_HEADS, HEAD_DIM, c.shape[2]), (0, 3, 1, 2))

        xp, s_p, sh_p, kc, vc = _prompt_layer(xp, p_prompt[i], gn, w_in_b, vecs, qw, kw, sinks_b, w_out_b, gp,
                                              w_gate_b, w_proj_b)
        outs[0].append(s_p)
        outs[2].append(sh_p)
        outs[4].append(from_t(kc))
        outs[6].append(from_t(vc))

        f, z_r, q, k, v, z_a = _in_proj(xs, gn, w_in_b, 512)
        o_r, s_s = _rwkv_sample(f, state_shift[i], state_rwkv[i], vecs, bs, SAMPLE_GROUP, dec, sh_p)
        o_a, k_buf, v_buf = _attn_sample(q.reshape(bs, dec, D_ATTN), k.reshape(bs, dec, D_KV), v.reshape(bs, dec, D_KV),
                                         to_t(cache_k[i]), to_t(cache_v[i]), qw, kw, jnp.repeat(sinks_b, dec, axis=0))
        outs[1].append(s_s)
        outs[3].append(f.reshape(bs, dec, D_SHIFT)[:, -1:])
        outs[5].append(from_t(k_buf))
        outs[7].append(from_t(v_buf))
        xs = _merge(xs, o_r, z_r, o_a.reshape(bs * dec, D_ATTN), z_a, p_sample[i].reshape(bs * dec, D_PLE),
                    w_out_b, gp, w_gate_b, w_proj_b, 512)
    st = lambda l: jnp.stack(l)
    return (xp, xs.reshape(bs, dec, D_MODEL),
            st(outs[0]), st(outs[1]), st(outs[2]), st(outs[3]), st(outs[4]), st(outs[5]), st(outs[6]), st(outs[7]))
```

```python
import functools
import math

import jax
import jax.numpy as jnp
from jax import lax
from jax.experimental import pallas as pl
from jax.experimental.pallas import tpu as pltpu

F32 = jnp.float32
BF16 = jnp.bfloat16

D_MODEL = 1024
HEAD_DIM = 64
D_RWKV = 512
D_ATTN = 512
N_KV_HEADS = 2
N_Q_HEADS = 8
Q_PER_KV = N_Q_HEADS // N_KV_HEADS
D_KV = N_KV_HEADS * HEAD_DIM
WINDOW = 128
D_LORA = 64
D_SHIFT = 3 * D_RWKV + 2 * D_LORA
D_PLE = 256
D_IN = D_SHIFT + D_RWKV + D_ATTN + 2 * D_KV + D_ATTN
NORM_EPS = 1e-6
LNX_EPS = 64e-5
NEG_INF = -1e30

LANES = 128
N_PAIRS = D_RWKV // LANES
PROMPT_CHUNK = 64
SAMPLE_GROUP = 16
VMEM_LIMIT = 56 * 1024 * 1024


def _dot(a, b):
    return jnp.dot(a.astype(BF16), b.astype(BF16), preferred_element_type=F32)


def _dot_nt(a, b):
    return lax.dot_general(a.astype(BF16), b.astype(BF16), (((1,), (1,)), ((), ())), preferred_element_type=F32)


def _dot_tn(a, b):
    return lax.dot_general(a.astype(BF16), b.astype(BF16), (((0,), (0,)), ((), ())), preferred_element_type=F32)


def _segment_cumsum(x, seg):
    blk = min(max(seg, HEAD_DIM), x.shape[0])
    ti = lax.broadcasted_iota(jnp.int32, (blk, 3 * blk), 0)
    tj = lax.broadcasted_iota(jnp.int32, (blk, 3 * blk), 1) % blk
    tri3 = ((ti // seg == tj // seg) & (tj <= ti)).astype(BF16)
    hi = x.astype(BF16)
    r1 = x - hi.astype(F32)
    mid = r1.astype(BF16)
    lo = (r1 - mid.astype(F32)).astype(BF16)
    parts = []
    for r0 in range(0, x.shape[0], blk):
        rows = slice(r0, r0 + blk)
        parts.append(jnp.dot(tri3, jnp.concatenate([hi[rows], mid[rows], lo[rows]], axis=0),
                             preferred_element_type=F32))
    return parts[0] if len(parts) == 1 else jnp.concatenate(parts, axis=0)


def _head_ones():
    r = (lax.broadcasted_iota(jnp.int32, (2 * LANES, LANES), 0) % LANES) // HEAD_DIM
    c = lax.broadcasted_iota(jnp.int32, (2 * LANES, LANES), 1) // HEAD_DIM
    return (r == c).astype(BF16)


def _head_sum(x, ones):
    tiles = []
    for i in range(0, x.shape[1], LANES):
        xt = x[:, i:i + LANES]
        hi = xt.astype(BF16)
        lo = (xt - hi.astype(F32)).astype(BF16)
        tiles.append(jnp.dot(jnp.concatenate([hi, lo], axis=1), ones, preferred_element_type=F32))
    return tiles[0] if len(tiles) == 1 else jnp.concatenate(tiles, axis=1)


def _rms_rows(x, g):
    return x * lax.rsqrt(jnp.mean(x * x, axis=-1, keepdims=True) + NORM_EPS) * g


_IN_SPLITS = (D_SHIFT, D_RWKV, D_ATTN, D_KV, D_KV, D_ATTN)


def _in_proj_kernel(x_ref, g_ref, w_ref, *out_refs):
    h = _dot(_rms_rows(x_ref[...], g_ref[...]), w_ref[...])
    off = 0
    for o_ref, width in zip(out_refs, _IN_SPLITS):
        o_ref[...] = h[:, off:off + width]
        off += width


def _in_proj(x, g_norm, w_in_bf16, tm):
    m = x.shape[0]
    return pl.pallas_call(
        _in_proj_kernel,
        grid=(m // tm,),
        in_specs=[pl.BlockSpec((tm, D_MODEL), lambda i: (i, 0)),
                  pl.BlockSpec((1, D_MODEL), lambda i: (0, 0)),
                  pl.BlockSpec((D_MODEL, D_IN), lambda i: (0, 0))],
        out_specs=[pl.BlockSpec((tm, w), lambda i: (i, 0)) for w in _IN_SPLITS],
        out_shape=[jax.ShapeDtypeStruct((m, w), F32) for w in _IN_SPLITS],
        compiler_params=pltpu.CompilerParams(dimension_semantics=("arbitrary",), vmem_limit_bytes=VMEM_LIMIT),
        name="in_proj",
    )(x, g_norm, w_in_bf16)


def _stack(z, half0):
    return jnp.concatenate([jnp.where(half0, z, 0.0), jnp.where(half0, 0.0, z)], axis=0).astype(BF16)


def _rwkv_recurrence(at, rt, bt, kt, v, e_cum, s_scr, cm, seg, segs_per_state, fill):
    n_rows = at.shape[0]
    n_blk = n_rows // cm
    lane = lax.broadcasted_iota(jnp.int32, (1, LANES), 1)
    half0 = lane < HEAD_DIM
    ri = lax.broadcasted_iota(jnp.int32, (cm, LANES), 0)
    ci = lax.broadcasted_iota(jnp.int32, (cm, LANES), 1) % cm
    same = (ri // seg) == (ci // seg)
    tri_strict = same & (ci < ri)
    tri_incl = same & (ci <= ri)
    eye_c = (ri == ci).astype(F32)
    sr = lax.broadcasted_iota(jnp.int32, (LANES, LANES), 0) // HEAD_DIM
    sc = lax.broadcasted_iota(jnp.int32, (LANES, LANES), 1) // HEAD_DIM
    state_mask = sr == sc
    n_levels = max(int(math.log2(seg)) - 1, 0)
    blocks = [(rb, p) for rb in range(n_blk) for p in range(N_PAIRS)]

    def tile(x, rb, p):
        return x[rb * cm:(rb + 1) * cm, p * LANES:(p + 1) * LANES]

    ops = {}
    for rb, p in blocks:
        at_p, rt_p, bt_p, kt_p, v_p = (tile(x, rb, p) for x in (at, rt, bt, kt, v))
        ops[rb, p] = dict(at=at_p, rt=rt_p, bt=bt_p, kt=kt_p, v=v_p, v_s=_stack(v_p, half0))
    for blk in blocks:
        o = ops[blk]
        g = _dot_nt(jnp.concatenate([o["at"], o["rt"]], axis=0),
                    jnp.concatenate([_stack(o["bt"], half0), _stack(o["kt"], half0)], axis=0))
        o["g_ab"] = jnp.where(tri_strict, g[:cm, :LANES], 0.0)
        o["g_ak"] = jnp.where(tri_strict, g[:cm, LANES:], 0.0)
        o["g_r"] = jnp.concatenate([jnp.where(tri_incl, g[cm:, :LANES], 0.0),
                                    jnp.where(tri_incl, g[cm:, LANES:], 0.0)], axis=1).astype(BF16)
    fill()

    for blk in blocks:
        ops[blk]["t_inv"] = eye_c + ops[blk]["g_ab"]
    if n_levels > 0:
        for blk in blocks:
            ops[blk]["a_pow"] = _dot(ops[blk]["g_ab"], _stack(ops[blk]["g_ab"], half0))
        fill()
        for lvl in range(n_levels):
            last = lvl == n_levels - 1
            for blk in blocks:
                o = ops[blk]
                if last:
                    o["t_inv"] = o["t_inv"] + _dot(o["a_pow"], _stack(o["t_inv"], half0))
                else:
                    m = _dot(o["a_pow"], jnp.concatenate([_stack(o["a_pow"], half0), _stack(o["t_inv"], half0)], axis=1))
                    o["t_inv"] = o["t_inv"] + m[:, LANES:]
                    o["a_pow"] = m[:, :LANES]
            fill()

    for blk in blocks:
        o = ops[blk]
        gakv = _dot(o["g_ak"], o["v_s"])
        z = _dot(o["t_inv"], jnp.concatenate([_stack(o["at"], half0), _stack(gakv, half0)], axis=1))
        o["a_hat"], o["p0"] = z[:, :LANES], z[:, LANES:]
    fill()

    n_seg = n_rows // seg
    n_states = n_seg // segs_per_state
    per_blk = cm // seg
    p_parts = {blk: [None] * per_blk for blk in blocks}
    y_parts = {blk: [None] * per_blk for blk in blocks}
    for step in range(segs_per_state):
        segs = [st * segs_per_state + step for st in range(n_states)]
        proj = {}
        for g_i in segs:
            rb, off = (g_i * seg) // cm, (g_i * seg) % cm
            for p in range(N_PAIRS):
                o = ops[rb, p]
                lhs = jnp.concatenate([o["a_hat"][off:off + seg], o["rt"][off:off + seg]], axis=0)
                proj[g_i, p] = _dot_nt(lhs, s_scr[(g_i // segs_per_state) * N_PAIRS + p])
        for g_i in segs:
            rb, off = (g_i * seg) // cm, (g_i * seg) % cm
            for p in range(N_PAIRS):
                o = ops[rb, p]
                p_seg = proj[g_i, p][:seg] + o["p0"][off:off + seg]
                p_parts[rb, p][off // seg] = p_seg
                y_parts[rb, p][off // seg] = proj[g_i, p][seg:]
                upd = _dot_tn(jnp.concatenate([p_seg, o["v"][off:off + seg]], axis=0),
                              jnp.concatenate([o["bt"][off:off + seg], o["kt"][off:off + seg]], axis=0))
                si = (g_i // segs_per_state) * N_PAIRS + p
                row_end = g_i * seg + seg - 1
                w_end = e_cum[row_end:row_end + 1, p * LANES:(p + 1) * LANES]
                s_scr[si] = w_end * (s_scr[si] + jnp.where(state_mask, upd, 0.0))

    cat = lambda parts: parts[0] if len(parts) == 1 else jnp.concatenate(parts, axis=0)
    rows = []
    for rb in range(n_blk):
        tiles = []
        for p in range(N_PAIRS):
            o = ops[rb, p]
            pv_s = jnp.concatenate([_stack(cat(p_parts[rb, p]), half0), o["v_s"]], axis=0)
            tiles.append(cat(y_parts[rb, p]) + jnp.dot(o["g_r"], pv_s, preferred_element_type=F32))
        rows.append(jnp.concatenate(tiles, axis=1))
    return cat(rows)


def _time_mix(f, prev_rows, tb, seg, cm, vec_refs, s_scr, fill=lambda: None):
    mu_ref, w0_ref, a0_ref, lora_ref, kk_ref, ka_ref, rk_ref, lnw_ref, lnb_ref = vec_refs
    n_rows = f.shape[0]
    row = lax.broadcasted_iota(jnp.int32, (n_rows, 1), 0)
    f_prev = pltpu.roll(f, 1, 0)
    for b, prev in enumerate(prev_rows):
        f_prev = jnp.where(row == b * tb, prev, f_prev)
    fs = f + (f_prev - f) * mu_ref[...]
    r = fs[:, 0:D_RWKV]
    k = fs[:, D_RWKV:2 * D_RWKV]
    v = fs[:, 2 * D_RWKV:3 * D_RWKV]
    wa = fs[:, 3 * D_RWKV:D_SHIFT]
    lane = lax.broadcasted_iota(jnp.int32, (1, LANES), 1)
    lora = _dot(jnp.where(lane < D_LORA, jnp.tanh(wa), wa), lora_ref[...])
    lw = (-math.exp(-0.5)) * jax.nn.sigmoid(w0_ref[...] + lora[:, 0:D_RWKV])
    a_sig = jax.nn.sigmoid(a0_ref[...] + lora[:, D_RWKV:2 * D_RWKV])
    ones = _head_ones()
    kk = k * kk_ref[...]
    kk = kk * lax.rsqrt(jnp.maximum(_head_sum(kk * kk, ones), 1e-24))
    k2 = k * (1.0 + (a_sig - 1.0) * ka_ref[...])

    cum = _segment_cumsum(lw, seg)
    e_cum = jnp.exp(cum)
    e_inv = jnp.exp(-cum)
    y = _rwkv_recurrence(-kk * jnp.exp(cum - lw), r * e_cum, kk * a_sig * e_inv, k2 * e_inv, v, e_cum,
                         s_scr, cm, seg, tb // seg, fill)

    inv_n = 1.0 / HEAD_DIM
    yc = y - _head_sum(y, ones) * inv_n
    var = _head_sum(yc * yc, ones) * inv_n
    yn = yc * lax.rsqrt(var + LNX_EPS) * lnw_ref[...] + lnb_ref[...]
    bonus = _head_sum(r * k2 * rk_ref[...], ones) * v
    return yn + bonus


def _load_states(s0_ref, s_scr, bb):
    zero = jnp.zeros((HEAD_DIM, HEAD_DIM), F32)
    for b in range(bb):
        for p in range(N_PAIRS):
            top = jnp.concatenate([s0_ref[b, 2 * p], zero], axis=1)
            bot = jnp.concatenate([zero, s0_ref[b, 2 * p + 1]], axis=1)
            s_scr[b * N_PAIRS + p] = jnp.concatenate([top, bot], axis=0)


def _store_states(s_scr, s_out_ref, bb):
    for b in range(bb):
        for p in range(N_PAIRS):
            s = s_scr[b * N_PAIRS + p]
            s_out_ref[b, 2 * p] = s[:HEAD_DIM, :HEAD_DIM]
            s_out_ref[b, 2 * p + 1] = s[HEAD_DIM:, HEAD_DIM:]


def _rwkv_sample_kernel(bb, tb, f_ref, prev0_ref, *rest):
    vec_refs, (s0_ref, _, o_ref, s_out_ref, s_scr) = rest[:9], rest[9:]
    _load_states(s0_ref, s_scr, bb)
    o_ref[...] = _time_mix(f_ref[...], [prev0_ref[b] for b in range(bb)], tb, tb, HEAD_DIM, vec_refs, s_scr)
    _store_states(s_scr, s_out_ref, bb)


def _vec_specs(index_map):
    vec = lambda n: pl.BlockSpec((1, n), index_map)
    return [vec(D_SHIFT), vec(D_RWKV), vec(D_RWKV), pl.BlockSpec((LANES, 2 * D_RWKV), index_map),
            vec(D_RWKV), vec(D_RWKV), vec(D_RWKV), vec(D_RWKV), vec(D_RWKV)]


def _rwkv_sample(f, prev0, s0, vecs, batch, bb, tb, after):
    n_rows = bb * tb
    assert n_rows % HEAD_DIM == 0 and HEAD_DIM % tb == 0
    state_spec = pl.BlockSpec((bb, 2 * N_PAIRS, HEAD_DIM, HEAD_DIM), lambda b: (b, 0, 0, 0))
    return pl.pallas_call(
        functools.partial(_rwkv_sample_kernel, bb, tb),
        grid=(batch // bb,),
        in_specs=[pl.BlockSpec((n_rows, D_SHIFT), lambda b: (b, 0)),
                  pl.BlockSpec((bb, 1, D_SHIFT), lambda b: (b, 0, 0))] + _vec_specs(lambda b: (0, 0))
        + [state_spec, pl.BlockSpec(memory_space=pl.ANY)],
        out_specs=[pl.BlockSpec((n_rows, D_RWKV), lambda b: (b, 0)), state_spec],
        out_shape=[jax.ShapeDtypeStruct((batch * tb, D_RWKV), F32),
                   jax.ShapeDtypeStruct((batch, 2 * N_PAIRS, HEAD_DIM, HEAD_DIM), F32)],
        scratch_shapes=[pltpu.VMEM((bb * N_PAIRS, LANES, LANES), F32)],
        compiler_params=pltpu.CompilerParams(dimension_semantics=("arbitrary",), vmem_limit_bytes=VMEM_LIMIT),
        name="rwkv_sample",
    )(f, prev0, *vecs, s0, after)


def _qk_norm(q, k, qw, kw, ones):
    inv_n = 1.0 / HEAD_DIM
    qn = q * lax.rsqrt(_head_sum(q * q, ones) * inv_n + NORM_EPS) * (qw * (HEAD_DIM ** -0.5))
    kn = k * lax.rsqrt(_head_sum(k * k, ones) * inv_n + NORM_EPS) * kw
    return qn, kn


def _swa_prompt_steps(q, k, v, qw, kw, sinks_ref, kprev, kprev_rot, vprev_t, has_prev, out):
    ones = _head_ones()
    half0 = lax.broadcasted_iota(jnp.int32, (1, LANES), 1) < HEAD_DIM
    qn, kn = _qk_norm(q, k, qw, kw, ones)
    kn_b = kn.astype(BF16)
    kn_rot = pltpu.roll(kn, HEAD_DIM, 1).astype(BF16)
    v_t = v.T.astype(BF16)
    keys = jnp.concatenate([kprev[...], kn_b], axis=0)
    keys_rot = jnp.concatenate([kprev_rot[...], kn_rot], axis=0)
    vals_t = jnp.concatenate([vprev_t[...], v_t], axis=1)
    kprev[...] = kn_b
    kprev_rot[...] = kn_rot
    vprev_t[...] = v_t
    kj = lax.broadcasted_iota(jnp.int32, (2 * WINDOW, WINDOW), 0)
    qi = lax.broadcasted_iota(jnp.int32, (2 * WINDOW, WINDOW), 1)
    valid_t = (kj > qi) & (kj <= qi + WINDOW) & ((kj >= WINDOW) | has_prev)
    heads = range(N_Q_HEADS)
    yield
    qm = [jnp.where(half0 if h % 2 == 0 else jnp.logical_not(half0), qn[:, (h // 2) * LANES:(h // 2 + 1) * LANES],
                    0.0).astype(BF16) for h in heads]
    straight = [h for h in heads if h // Q_PER_KV == h % 2]
    swapped = [h for h in heads if h // Q_PER_KV != h % 2]
    scores = [None] * N_Q_HEADS
    for group, kmat in ((straight, keys), (swapped, keys_rot)):
        for h0, h1 in zip(group[0::2], group[1::2]):
            s2 = _dot_nt(kmat, jnp.concatenate([qm[h0], qm[h1]], axis=0))
            scores[h0] = jnp.where(valid_t, s2[:, :WINDOW], NEG_INF)
            scores[h1] = jnp.where(valid_t, s2[:, WINDOW:], NEG_INF)
    yield
    probs = []
    for h in heads:
        sink = sinks_ref[h:h + 1, 0:1]
        m = jnp.maximum(jnp.max(scores[h], axis=0, keepdims=True), sink)
        pr = jnp.exp(scores[h] - m)
        denom = jnp.sum(pr, axis=0, keepdims=True) + jnp.exp(sink - m)
        probs.append((pr * (1.0 / denom)).astype(BF16))
    yield
    tiles = []
    for t in range(D_ATTN // LANES):
        g = (2 * t) // Q_PER_KV
        o2 = jnp.dot(vals_t[g * HEAD_DIM:(g + 1) * HEAD_DIM, :], jnp.concatenate(probs[2 * t:2 * t + 2], axis=1),
                     preferred_element_type=F32)
        tiles.append(jnp.concatenate([o2[:, :WINDOW], o2[:, WINDOW:]], axis=0).T)
    out["o_a"] = jnp.concatenate(tiles, axis=1)
    out["kn"] = kn


def _attn_sample_kernel(t_new, q_ref, k_ref, v_ref, ckt_ref, cvt_ref, qw_ref, kw_ref, sinks_ref,
                        o_ref, kot_ref, vot_ref):
    n_seq = q_ref.shape[0]
    wb = ckt_ref.shape[2]
    rows = N_Q_HEADS * t_new
    ones = _head_ones()
    half0 = lax.broadcasted_iota(jnp.int32, (1, LANES), 1) < HEAD_DIM
    qi_c = lax.broadcasted_iota(jnp.int32, (rows, wb), 0) % t_new
    kj_c = lax.broadcasted_iota(jnp.int32, (rows, wb), 1)
    valid_c = (qi_c + wb - kj_c) < WINDOW
    qi_n = lax.broadcasted_iota(jnp.int32, (rows, t_new), 0) % t_new
    kj_n = lax.broadcasted_iota(jnp.int32, (rows, t_new), 1)
    valid_n = kj_n <= qi_n
    sink = sinks_ref[:, 0:1]
    heads = range(N_Q_HEADS)
    swap = [h // Q_PER_KV != h % 2 for h in heads]

    keep = lax.broadcasted_iota(jnp.int32, (1, wb), 1) < wb - t_new
    sel_t = lax.broadcasted_iota(jnp.int32, (3 * t_new, wb), 0) % t_new
    sel_l = lax.broadcasted_iota(jnp.int32, (3 * t_new, wb), 1)
    sel3 = (sel_l == sel_t + (wb - t_new)).astype(BF16)

    def _place_new(x):
        hi = x.astype(BF16)
        r1 = x - hi.astype(F32)
        mid = r1.astype(BF16)
        lo = (r1 - mid.astype(F32)).astype(BF16)
        return lax.dot_general(jnp.concatenate([hi, mid, lo], axis=0), sel3, (((0,), (0,)), ((), ())),
                               preferred_element_type=F32)

    seqs = []
    for i in range(n_seq):
        qn, kn = _qk_norm(q_ref[i], k_ref[i], qw_ref[...], kw_ref[...], ones)
        pieces = []
        for h in heads:
            qm = jnp.where(half0 if h % 2 == 0 else jnp.logical_not(half0), qn[:, (h // 2) * LANES:(h // 2 + 1) * LANES],
                           0.0)
            pieces.append(pltpu.roll(qm, HEAD_DIM, 1) if swap[h] else qm)
        seqs.append(dict(q=jnp.concatenate(pieces, axis=0).astype(BF16), kn=kn, v=v_ref[i],
                         ckt=ckt_ref[i], cvt=cvt_ref[i]))
    for s in seqs:
        s["s_c"] = jnp.where(valid_c, _dot(s["q"], s["ckt"]), NEG_INF)
        s["s_n"] = jnp.where(valid_n, _dot_nt(s["q"], s["kn"]), NEG_INF)
    for s in seqs:
        m = jnp.maximum(jnp.maximum(jnp.max(s["s_c"], axis=-1, keepdims=True),
                                    jnp.max(s["s_n"], axis=-1, keepdims=True)), sink)
        p_c = jnp.exp(s["s_c"] - m)
        p_n = jnp.exp(s["s_n"] - m)
        denom = jnp.sum(p_c, axis=-1, keepdims=True) + jnp.sum(p_n, axis=-1, keepdims=True) + jnp.exp(sink - m)
        s["p_c"], s["p_n"], s["inv"] = p_c, p_n, 1.0 / denom
    for i, s in enumerate(seqs):
        o = (_dot_nt(s["p_c"], s["cvt"]) + _dot(s["p_n"], s["v"])) * s["inv"]
        tiles = []
        for t in range(D_ATTN // LANES):
            pair = []
            for h in (2 * t, 2 * t + 1):
                o_h = o[h * t_new:(h + 1) * t_new]
                pair.append(pltpu.roll(o_h, HEAD_DIM, 1) if swap[h] else o_h)
            tiles.append(jnp.where(half0, pair[0], pair[1]))
        o_ref[i] = jnp.concatenate(tiles, axis=1)
        kot_ref[i] = jnp.where(keep, pltpu.roll(s["ckt"], wb - t_new, 1), _place_new(s["kn"]))
        vot_ref[i] = jnp.where(keep, pltpu.roll(s["cvt"], wb - t_new, 1), _place_new(s["v"]))


def _attn_sample(q, k, v, ckt, cvt, qw, kw, sinks_rows):
    b, t_new, _ = q.shape
    wb = ckt.shape[2]
    gb = SAMPLE_GROUP
    spec = lambda r, w: pl.BlockSpec((gb, r, w), lambda i: (i, 0, 0))
    const = lambda r, w: pl.BlockSpec((r, w), lambda i: (0, 0))
    return pl.pallas_call(
        functools.partial(_attn_sample_kernel, t_new),
        grid=(b // gb,),
        in_specs=[spec(t_new, D_ATTN), spec(t_new, D_KV), spec(t_new, D_KV), spec(D_KV, wb), spec(D_KV, wb),
                  const(1, D_ATTN), const(1, D_KV), const(N_Q_HEADS * t_new, LANES)],
        out_specs=[spec(t_new, D_ATTN), spec(D_KV, wb), spec(D_KV, wb)],
        out_shape=[jax.ShapeDtypeStruct((b, t_new, D_ATTN), F32),
                   jax.ShapeDtypeStruct((b, D_KV, wb), F32),
                   jax.ShapeDtypeStruct((b, D_KV, wb), F32)],
        compiler_params=pltpu.CompilerParams(dimension_semantics=("arbitrary",)),
        name="attn_sample",
    )(q, k, v, ckt, cvt, qw, kw, sinks_rows)


def _merge_rows(x, o_r, z_r, o_a, z_a, p, wout_ref, g_ref, wgate_ref, wproj_ref, fill=lambda: None):
    gr = o_r * jax.nn.silu(z_r)
    ga = o_a * jax.nn.silu(z_a)
    ple = _dot(p, wproj_ref[...])
    h = x + _dot(gr, wout_ref[0:D_RWKV, :]) + _dot(ga, wout_ref[D_RWKV:D_MODEL, :])
    fill()
    gate = jax.nn.sigmoid(_dot(_rms_rows(h, g_ref[...]), wgate_ref[...]))
    fill()
    return h + gate * ple


def _merge_kernel(x_ref, or_ref, zr_ref, oa_ref, za_ref, p_ref, wout_ref, g_ref, wgate_ref, wproj_ref, y_ref):
    y_ref[...] = _merge_rows(x_ref[...], or_ref[...], zr_ref[...], oa_ref[...], za_ref[...], p_ref[...],
                             wout_ref, g_ref, wgate_ref, wproj_ref)


def _merge(x, o_r, z_r, o_a, z_a, p, w_out, g_ple, w_gate, w_proj, tm):
    m = x.shape[0]
    tok = lambda w: pl.BlockSpec((tm, w), lambda i: (i, 0))
    const = lambda r, w: pl.BlockSpec((r, w), lambda i: (0, 0))
    return pl.pallas_call(
        _merge_kernel,
        grid=(m // tm,),
        in_specs=[tok(D_MODEL), tok(D_RWKV), tok(D_RWKV), tok(D_ATTN), tok(D_ATTN), tok(D_PLE),
                  const(D_MODEL, D_MODEL), const(1, D_MODEL), const(D_MODEL, D_MODEL), const(D_PLE, D_MODEL)],
        out_specs=tok(D_MODEL),
        out_shape=jax.ShapeDtypeStruct((m, D_MODEL), F32),
        compiler_params=pltpu.CompilerParams(dimension_semantics=("arbitrary",), vmem_limit_bytes=VMEM_LIMIT),
        name="merge",
    )(x, o_r, z_r, o_a, z_a, p, w_out, g_ple, w_gate, w_proj)


_IN_CHUNK = 512
PROMPT_ROWS = 2
_FILL_HEAD = 3
_FILL_TAIL = 2


def _fill_order(n_proj, n_attn):
    mid = n_proj - _FILL_HEAD - _FILL_TAIL
    order = ["p"] * _FILL_HEAD
    done = 0
    for i in range(n_attn):
        while done < mid and done * n_attn <= i * mid:
            order.append("p")
            done += 1
        order.append("a")
    return order + ["p"] * (mid - done + _FILL_TAIL)


def _prompt_layer_kernel(nblk, x_ref, p_ref, gn_ref, win_ref, *rest):
    vec_refs = rest[:9]
    (qw_ref, kw_ref, sinks_ref, wout_ref, gp_ref, wgate_ref, wproj_ref, _,
     y_ref, s_out_ref, shift_ref, kc_ref, vc_ref,
     hbuf, xbuf, s_scr, prev_scr, kprev, kprev_rot, vprev_t) = rest[9:]
    bb = x_ref.shape[0]
    n_rows = bb * WINDOW
    s = pl.program_id(0)
    slot = s % 2
    j = jnp.maximum(s - 1, 0) % nblk

    @pl.when(s == 0)
    def _():
        hbuf[1] = jnp.zeros(hbuf.shape[1:], F32)
        xbuf[1] = jnp.zeros(xbuf.shape[1:], F32)

    @pl.when(j == 0)
    def _():
        s_scr[...] = jnp.zeros(s_scr.shape, F32)
        prev_scr[...] = jnp.zeros(prev_scr.shape, F32)
        kprev[...] = jnp.zeros(kprev.shape, BF16)
        kprev_rot[...] = jnp.zeros(kprev_rot.shape, BF16)
        vprev_t[...] = jnp.zeros(vprev_t.shape, BF16)

    h_cur = hbuf.at[1 - slot]
    x_new = x_ref[...].reshape(n_rows, D_MODEL)
    xbuf[slot] = x_new
    xn = _rms_rows(x_new, gn_ref[...]).astype(BF16)
    h_new = hbuf.at[slot]

    def project(c0):
        c1 = min(c0 + _IN_CHUNK, D_IN)
        h_new[:, c0:c1] = jnp.dot(xn, win_ref[:, c0:c1], preferred_element_type=F32)

    offs = [0]
    for w in _IN_SPLITS:
        offs.append(offs[-1] + w)
    part = lambda i: h_cur[:, offs[i]:offs[i + 1]]
    q, k, v = part(2), part(3), part(4)
    attn = [{} for _ in range(bb)]
    attn_steps = []
    for b in range(bb):
        rows = slice(b * WINDOW, (b + 1) * WINDOW)
        attn_steps.append(_swa_prompt_steps(q[rows], k[rows], v[rows], qw_ref[...], kw_ref[...], sinks_ref,
                                            kprev.at[b], kprev_rot.at[b], vprev_t.at[b], j > 0, attn[b]))
    proj_items = [functools.partial(project, c0) for c0 in range(0, D_IN, _IN_CHUNK)]
    attn_items = [functools.partial(next, attn_steps[b], None) for _ in range(4) for b in range(bb)]
    queue = [proj_items.pop(0) if kind == "p" else attn_items.pop(0)
             for kind in _fill_order(len(proj_items), len(attn_items))]
    assert not proj_items and not attn_items

    def fill():
        if queue:
            queue.pop(0)()

    for _ in range(_FILL_HEAD):
        fill()
    f = part(0)
    o_r = _time_mix(f, [prev_scr[b] for b in range(bb)], WINDOW, PROMPT_CHUNK, HEAD_DIM, vec_refs, s_scr, fill)
    for b in range(bb):
        prev_scr[b] = f[(b + 1) * WINDOW - 1:(b + 1) * WINDOW, :]
    while len(queue) > _FILL_TAIL:
        fill()
    o_a = jnp.concatenate([attn[b]["o_a"] for b in range(bb)], axis=0)
    y = _merge_rows(xbuf[1 - slot], o_r, part(1), o_a, part(5), p_ref[...].reshape(n_rows, D_PLE), wout_ref, gp_ref,
                    wgate_ref, wproj_ref, fill)
    y_ref[...] = y.reshape(y_ref.shape)
    while queue:
        fill()

    @pl.when((s > 0) & (j == nblk - 1))
    def _():
        _store_states(s_scr, s_out_ref, bb)
        shift_ref[...] = prev_scr[...]
        for b in range(bb):
            kc_ref[b] = attn[b]["kn"].T
            vc_ref[b] = v[b * WINDOW:(b + 1) * WINDOW].T


def _prompt_layer(x, p, gn, w_in_b, vecs, qw, kw, sinks_b, w_out_b, gp, w_gate_b, w_proj_b, after):
    batch, seq, _ = x.shape
    bb = PROMPT_ROWS
    nblk = seq // WINDOW
    n_steps = (batch // bb) * nblk
    const = lambda r, w: pl.BlockSpec((r, w), lambda s: (0, 0))
    cur = lambda s: jnp.minimum(s, n_steps - 1)
    prv = lambda s: jnp.maximum(s - 1, 0)
    per_row = lambda *shape: pl.BlockSpec((bb,) + shape, lambda s: (prv(s) // nblk,) + (0,) * len(shape))
    in_specs = ([pl.BlockSpec((bb, WINDOW, D_MODEL), lambda s: (cur(s) // nblk, cur(s) % nblk, 0)),
                 pl.BlockSpec((bb, WINDOW, D_PLE), lambda s: (prv(s) // nblk, prv(s) % nblk, 0)),
                 const(1, D_MODEL), const(D_MODEL, D_IN)]
                + _vec_specs(lambda s: (0, 0))
                + [const(1, D_ATTN), const(1, D_KV), const(N_Q_HEADS, LANES), const(D_MODEL, D_MODEL),
                   const(1, D_MODEL), const(D_MODEL, D_MODEL), const(D_PLE, D_MODEL),
                   pl.BlockSpec(memory_space=pl.ANY)])
    return pl.pallas_call(
        functools.partial(_prompt_layer_kernel, nblk),
        grid=(n_steps + 1,),
        in_specs=in_specs,
        out_specs=[pl.BlockSpec((bb, WINDOW, D_MODEL), lambda s: (prv(s) // nblk, prv(s) % nblk, 0)),
                   per_row(2 * N_PAIRS, HEAD_DIM, HEAD_DIM), per_row(1, D_SHIFT),
                   per_row(D_KV, WINDOW), per_row(D_KV, WINDOW)],
        out_shape=[jax.ShapeDtypeStruct((batch, seq, D_MODEL), F32),
                   jax.ShapeDtypeStruct((batch, 2 * N_PAIRS, HEAD_DIM, HEAD_DIM), F32),
                   jax.ShapeDtypeStruct((batch, 1, D_SHIFT), F32),
                   jax.ShapeDtypeStruct((batch, D_KV, WINDOW), F32),
                   jax.ShapeDtypeStruct((batch, D_KV, WINDOW), F32)],
        scratch_shapes=[pltpu.VMEM((2, bb * WINDOW, D_IN), F32), pltpu.VMEM((2, bb * WINDOW, D_MODEL), F32),
                        pltpu.VMEM((bb * N_PAIRS, LANES, LANES), F32), pltpu.VMEM((bb, 1, D_SHIFT), F32),
                        pltpu.VMEM((bb, WINDOW, D_KV), BF16), pltpu.VMEM((bb, WINDOW, D_KV), BF16),
                        pltpu.VMEM((bb, D_KV, WINDOW), BF16)],
        compiler_params=pltpu.CompilerParams(dimension_semantics=("arbitrary",), vmem_limit_bytes=VMEM_LIMIT),
        name="prompt_layer",
    )(x, p, gn, w_in_b, *vecs, qw, kw, sinks_b, w_out_b, gp, w_gate_b, w_proj_b, after)


def kernel(x_prompt, x_sample, state_rwkv, state_shift, cache_k, cache_v, p_prompt, p_sample, g_norm, w_in, mu_shift, w0, w_dec2, a0, w_a2, k_k, k_a, r_k, lnx_w, lnx_b, q_norm_w, k_norm_w, sinks, w_out, g_ple, w_ple_gate, w_ple_proj):
    depth = w_in.shape[0]
    bp, seq, _ = x_prompt.shape
    bs, dec, _ = x_sample.shape
    xp = x_prompt
    xs = x_sample.reshape(bs * dec, D_MODEL)
    outs = [[] for _ in range(8)]
    for i in range(depth):
        w_in_b = w_in[i].astype(BF16)
        w_out_b = w_out[i].astype(BF16)
        w_gate_b = w_ple_gate[i].astype(BF16)
        w_proj_b = w_ple_proj[i].astype(BF16)
        zl = jnp.zeros((D_LORA, D_RWKV), F32)
        lora_w = jnp.concatenate([jnp.concatenate([w_dec2[i], zl], axis=1),
                                  jnp.concatenate([zl, w_a2[i]], axis=1)], axis=0).astype(BF16)
        row = lambda t: t.reshape(1, -1)
        vecs = (row(mu_shift[i]), row(w0[i]), row(a0[i]), lora_w, row(k_k[i]), row(k_a[i]), row(r_k[i]),
                row(lnx_w[i]), row(lnx_b[i]))
        qw = row(jnp.tile(q_norm_w[i], N_Q_HEADS))
        kw = row(jnp.tile(k_norm_w[i], N_KV_HEADS))
        sinks_b = jnp.broadcast_to(sinks[i][:, None], (N_Q_HEADS, LANES))
        gn, gp = row(g_norm[i]), row(g_ple[i])

        to_t = lambda c: jnp.transpose(c, (0, 2, 3, 1)).reshape(c.shape[0], D_KV, c.shape[1])
        from_t = lambda c: jnp.transpose(c.reshape(c.shape[0], N_KV_HEADS, HEAD_DIM, c.shape[2]), (0, 3, 1, 2))

        f, z_r, q, k, v, z_a = _in_proj(xs, gn, w_in_b, 512)
        o_a, k_buf, v_buf = _attn_sample(q.reshape(bs, dec, D_ATTN), k.reshape(bs, dec, D_KV), v.reshape(bs, dec, D_KV),
                                         to_t(cache_k[i]), to_t(cache_v[i]), qw, kw, jnp.repeat(sinks_b, dec, axis=0))
        o_r, s_s = _rwkv_sample(f, state_shift[i], state_rwkv[i], vecs, bs, SAMPLE_GROUP, dec, o_a)
        outs[1].append(s_s)
        outs[3].append(f.reshape(bs, dec, D_SHIFT)[:, -1:])
        outs[5].append(from_t(k_buf))
        outs[7].append(from_t(v_buf))

        xp, s_p, sh_p, kc, vc = _prompt_layer(xp, p_prompt[i], gn, w_in_b, vecs, qw, kw, sinks_b, w_out_b, gp,
                                              w_gate_b, w_proj_b, s_s)
        outs[0].append(s_p)
        outs[2].append(sh_p)
        outs[4].append(from_t(kc))
        outs[6].append(from_t(vc))
        xs = _merge(xs, o_r, z_r, o_a.reshape(bs * dec, D_ATTN), z_a, p_sample[i].reshape(bs * dec, D_PLE),
                    w_out_b, gp, w_gate_b, w_proj_b, 512)
    st = lambda l: jnp.stack(l)
    return (xp, xs.reshape(bs, dec, D_MODEL),
            st(outs[0]), st(outs[1]), st(outs[2]), st(outs[3]), st(outs[4]), st(outs[5]), st(outs[6]), st(outs[7]))
```

```python
import functools
import math

import jax
import jax.numpy as jnp
from jax import lax
from jax.experimental import pallas as pl
from jax.experimental.pallas import tpu as pltpu

F32 = jnp.float32
BF16 = jnp.bfloat16

D_MODEL = 1024
HEAD_DIM = 64
D_RWKV = 512
D_ATTN = 512
N_KV_HEADS = 2
N_Q_HEADS = 8
Q_PER_KV = N_Q_HEADS // N_KV_HEADS
D_KV = N_KV_HEADS * HEAD_DIM
WINDOW = 128
D_LORA = 64
D_SHIFT = 3 * D_RWKV + 2 * D_LORA
D_PLE = 256
D_IN = D_SHIFT + D_RWKV + D_ATTN + 2 * D_KV + D_ATTN
NORM_EPS = 1e-6
LNX_EPS = 64e-5
NEG_INF = -1e30

LANES = 128
N_PAIRS = D_RWKV // LANES
PROMPT_CHUNK = 64
SAMPLE_GROUP = 16
VMEM_LIMIT = 56 * 1024 * 1024


def _dot(a, b):
    return jnp.dot(a.astype(BF16), b.astype(BF16), preferred_element_type=F32)


def _dot_nt(a, b):
    return lax.dot_general(a.astype(BF16), b.astype(BF16), (((1,), (1,)), ((), ())), preferred_element_type=F32)


def _dot_tn(a, b):
    return lax.dot_general(a.astype(BF16), b.astype(BF16), (((0,), (0,)), ((), ())), preferred_element_type=F32)


def _segment_cumsum(x, seg):
    blk = min(max(seg, HEAD_DIM), x.shape[0])
    ti = lax.broadcasted_iota(jnp.int32, (blk, 3 * blk), 0)
    tj = lax.broadcasted_iota(jnp.int32, (blk, 3 * blk), 1) % blk
    tri3 = ((ti // seg == tj // seg) & (tj <= ti)).astype(BF16)
    hi = x.astype(BF16)
    r1 = x - hi.astype(F32)
    mid = r1.astype(BF16)
    lo = (r1 - mid.astype(F32)).astype(BF16)
    parts = []
    for r0 in range(0, x.shape[0], blk):
        rows = slice(r0, r0 + blk)
        parts.append(jnp.dot(tri3, jnp.concatenate([hi[rows], mid[rows], lo[rows]], axis=0),
                             preferred_element_type=F32))
    return parts[0] if len(parts) == 1 else jnp.concatenate(parts, axis=0)


def _head_ones():
    r = (lax.broadcasted_iota(jnp.int32, (2 * LANES, LANES), 0) % LANES) // HEAD_DIM
    c = lax.broadcasted_iota(jnp.int32, (2 * LANES, LANES), 1) // HEAD_DIM
    return (r == c).astype(BF16)


def _head_sum(x, ones):
    tiles = []
    for i in range(0, x.shape[1], LANES):
        xt = x[:, i:i + LANES]
        hi = xt.astype(BF16)
        lo = (xt - hi.astype(F32)).astype(BF16)
        tiles.append(jnp.dot(jnp.concatenate([hi, lo], axis=1), ones, preferred_element_type=F32))
    return tiles[0] if len(tiles) == 1 else jnp.concatenate(tiles, axis=1)


def _rms_rows(x, g):
    return x * lax.rsqrt(jnp.mean(x * x, axis=-1, keepdims=True) + NORM_EPS) * g


_IN_SPLITS = (D_SHIFT, D_RWKV, D_ATTN, D_KV, D_KV, D_ATTN)


def _in_proj_kernel(x_ref, g_ref, w_ref, *out_refs):
    h = _dot(_rms_rows(x_ref[...], g_ref[...]), w_ref[...])
    off = 0
    for o_ref, width in zip(out_refs, _IN_SPLITS):
        o_ref[...] = h[:, off:off + width]
        off += width


def _in_proj(x, g_norm, w_in_bf16, tm):
    m = x.shape[0]
    return pl.pallas_call(
        _in_proj_kernel,
        grid=(m // tm,),
        in_specs=[pl.BlockSpec((tm, D_MODEL), lambda i: (i, 0)),
                  pl.BlockSpec((1, D_MODEL), lambda i: (0, 0)),
                  pl.BlockSpec((D_MODEL, D_IN), lambda i: (0, 0))],
        out_specs=[pl.BlockSpec((tm, w), lambda i: (i, 0)) for w in _IN_SPLITS],
        out_shape=[jax.ShapeDtypeStruct((m, w), F32) for w in _IN_SPLITS],
        compiler_params=pltpu.CompilerParams(dimension_semantics=("arbitrary",), vmem_limit_bytes=VMEM_LIMIT),
        name="in_proj",
    )(x, g_norm, w_in_bf16)


def _stack(z, half0):
    return jnp.concatenate([jnp.where(half0, z, 0.0), jnp.where(half0, 0.0, z)], axis=0).astype(BF16)


def _rwkv_recurrence(at, rt, bt, kt, v, e_cum, s_scr, cm, seg, segs_per_state, fill):
    n_rows = at.shape[0]
    n_blk = n_rows // cm
    lane = lax.broadcasted_iota(jnp.int32, (1, LANES), 1)
    half0 = lane < HEAD_DIM
    ri = lax.broadcasted_iota(jnp.int32, (cm, LANES), 0)
    ci = lax.broadcasted_iota(jnp.int32, (cm, LANES), 1) % cm
    same = (ri // seg) == (ci // seg)
    tri_strict = same & (ci < ri)
    tri_incl = same & (ci <= ri)
    eye_c = (ri == ci).astype(F32)
    sr = lax.broadcasted_iota(jnp.int32, (LANES, LANES), 0) // HEAD_DIM
    sc = lax.broadcasted_iota(jnp.int32, (LANES, LANES), 1) // HEAD_DIM
    state_mask = sr == sc
    n_levels = max(int(math.log2(seg)) - 1, 0)
    blocks = [(rb, p) for rb in range(n_blk) for p in range(N_PAIRS)]

    def tile(x, rb, p):
        return x[rb * cm:(rb + 1) * cm, p * LANES:(p + 1) * LANES]

    ops = {}
    for rb, p in blocks:
        at_p, rt_p, bt_p, kt_p, v_p = (tile(x, rb, p) for x in (at, rt, bt, kt, v))
        ops[rb, p] = dict(at=at_p, rt=rt_p, bt=bt_p, kt=kt_p, v=v_p, v_s=_stack(v_p, half0))
    for blk in blocks:
        o = ops[blk]
        g = _dot_nt(jnp.concatenate([o["at"], o["rt"]], axis=0),
                    jnp.concatenate([_stack(o["bt"], half0), _stack(o["kt"], half0)], axis=0))
        o["g_ab"] = jnp.where(tri_strict, g[:cm, :LANES], 0.0)
        o["g_ak"] = jnp.where(tri_strict, g[:cm, LANES:], 0.0)
        o["g_r"] = jnp.concatenate([jnp.where(tri_incl, g[cm:, :LANES], 0.0),
                                    jnp.where(tri_incl, g[cm:, LANES:], 0.0)], axis=1).astype(BF16)
    fill()

    for blk in blocks:
        ops[blk]["t_inv"] = eye_c + ops[blk]["g_ab"]
    if n_levels > 0:
        for blk in blocks:
            ops[blk]["a_pow"] = _dot(ops[blk]["g_ab"], _stack(ops[blk]["g_ab"], half0))
        fill()
        for lvl in range(n_levels):
            last = lvl == n_levels - 1
            for blk in blocks:
                o = ops[blk]
                if last:
                    o["t_inv"] = o["t_inv"] + _dot(o["a_pow"], _stack(o["t_inv"], half0))
                else:
                    m = _dot(o["a_pow"], jnp.concatenate([_stack(o["a_pow"], half0), _stack(o["t_inv"], half0)], axis=1))
                    o["t_inv"] = o["t_inv"] + m[:, LANES:]
                    o["a_pow"] = m[:, :LANES]
            fill()

    for blk in blocks:
        o = ops[blk]
        gakv = _dot(o["g_ak"], o["v_s"])
        z = _dot(o["t_inv"], jnp.concatenate([_stack(o["at"], half0), _stack(gakv, half0)], axis=1))
        o["a_hat"], o["p0"] = z[:, :LANES], z[:, LANES:]
    fill()

    n_seg = n_rows // seg
    n_states = n_seg // segs_per_state
    per_blk = cm // seg
    p_parts = {blk: [None] * per_blk for blk in blocks}
    y_parts = {blk: [None] * per_blk for blk in blocks}
    for step in range(segs_per_state):
        segs = [st * segs_per_state + step for st in range(n_states)]
        proj = {}
        for g_i in segs:
            rb, off = (g_i * seg) // cm, (g_i * seg) % cm
            for p in range(N_PAIRS):
                o = ops[rb, p]
                lhs = jnp.concatenate([o["a_hat"][off:off + seg], o["rt"][off:off + seg]], axis=0)
                proj[g_i, p] = _dot_nt(lhs, s_scr[(g_i // segs_per_state) * N_PAIRS + p])
        for g_i in segs:
            rb, off = (g_i * seg) // cm, (g_i * seg) % cm
            for p in range(N_PAIRS):
                o = ops[rb, p]
                p_seg = proj[g_i, p][:seg] + o["p0"][off:off + seg]
                p_parts[rb, p][off // seg] = p_seg
                y_parts[rb, p][off // seg] = proj[g_i, p][seg:]
                upd = _dot_tn(jnp.concatenate([p_seg, o["v"][off:off + seg]], axis=0),
                              jnp.concatenate([o["bt"][off:off + seg], o["kt"][off:off + seg]], axis=0))
                si = (g_i // segs_per_state) * N_PAIRS + p
                row_end = g_i * seg + seg - 1
                w_end = e_cum[row_end:row_end + 1, p * LANES:(p + 1) * LANES]
                s_scr[si] = w_end * (s_scr[si] + jnp.where(state_mask, upd, 0.0))

    cat = lambda parts: parts[0] if len(parts) == 1 else jnp.concatenate(parts, axis=0)
    rows = []
    for rb in range(n_blk):
        tiles = []
        for p in range(N_PAIRS):
            o = ops[rb, p]
            pv_s = jnp.concatenate([_stack(cat(p_parts[rb, p]), half0), o["v_s"]], axis=0)
            tiles.append(cat(y_parts[rb, p]) + jnp.dot(o["g_r"], pv_s, preferred_element_type=F32))
        rows.append(jnp.concatenate(tiles, axis=1))
    return cat(rows)


def _time_mix(f, prev_rows, tb, seg, cm, vec_refs, s_scr, fill=lambda: None):
    mu_ref, w0_ref, a0_ref, lora_ref, kk_ref, ka_ref, rk_ref, lnw_ref, lnb_ref = vec_refs
    n_rows = f.shape[0]
    row = lax.broadcasted_iota(jnp.int32, (n_rows, 1), 0)
    f_prev = pltpu.roll(f, 1, 0)
    for b, prev in enumerate(prev_rows):
        f_prev = jnp.where(row == b * tb, prev, f_prev)
    fs = f + (f_prev - f) * mu_ref[...]
    r = fs[:, 0:D_RWKV]
    k = fs[:, D_RWKV:2 * D_RWKV]
    v = fs[:, 2 * D_RWKV:3 * D_RWKV]
    wa = fs[:, 3 * D_RWKV:D_SHIFT]
    lane = lax.broadcasted_iota(jnp.int32, (1, LANES), 1)
    lora = _dot(jnp.where(lane < D_LORA, jnp.tanh(wa), wa), lora_ref[...])
    lw = (-math.exp(-0.5)) * jax.nn.sigmoid(w0_ref[...] + lora[:, 0:D_RWKV])
    a_sig = jax.nn.sigmoid(a0_ref[...] + lora[:, D_RWKV:2 * D_RWKV])
    ones = _head_ones()
    kk = k * kk_ref[...]
    kk = kk * lax.rsqrt(jnp.maximum(_head_sum(kk * kk, ones), 1e-24))
    k2 = k * (1.0 + (a_sig - 1.0) * ka_ref[...])

    cum = _segment_cumsum(lw, seg)
    e_cum = jnp.exp(cum)
    e_inv = jnp.exp(-cum)
    y = _rwkv_recurrence(-kk * jnp.exp(cum - lw), r * e_cum, kk * a_sig * e_inv, k2 * e_inv, v, e_cum,
                         s_scr, cm, seg, tb // seg, fill)

    inv_n = 1.0 / HEAD_DIM
    yc = y - _head_sum(y, ones) * inv_n
    var = _head_sum(yc * yc, ones) * inv_n
    yn = yc * lax.rsqrt(var + LNX_EPS) * lnw_ref[...] + lnb_ref[...]
    bonus = _head_sum(r * k2 * rk_ref[...], ones) * v
    return yn + bonus


def _load_states(s0_ref, s_scr, bb):
    zero = jnp.zeros((HEAD_DIM, HEAD_DIM), F32)
    for b in range(bb):
        for p in range(N_PAIRS):
            top = jnp.concatenate([s0_ref[b, 2 * p], zero], axis=1)
            bot = jnp.concatenate([zero, s0_ref[b, 2 * p + 1]], axis=1)
            s_scr[b * N_PAIRS + p] = jnp.concatenate([top, bot], axis=0)


def _store_states(s_scr, s_out_ref, bb):
    for b in range(bb):
        for p in range(N_PAIRS):
            s = s_scr[b * N_PAIRS + p]
            s_out_ref[b, 2 * p] = s[:HEAD_DIM, :HEAD_DIM]
            s_out_ref[b, 2 * p + 1] = s[HEAD_DIM:, HEAD_DIM:]


def _vec_specs(index_map):
    vec = lambda n: pl.BlockSpec((1, n), index_map)
    return [vec(D_SHIFT), vec(D_RWKV), vec(D_RWKV), pl.BlockSpec((LANES, 2 * D_RWKV), index_map),
            vec(D_RWKV), vec(D_RWKV), vec(D_RWKV), vec(D_RWKV), vec(D_RWKV)]


def _qk_norm(q, k, qw, kw, ones):
    inv_n = 1.0 / HEAD_DIM
    qn = q * lax.rsqrt(_head_sum(q * q, ones) * inv_n + NORM_EPS) * (qw * (HEAD_DIM ** -0.5))
    kn = k * lax.rsqrt(_head_sum(k * k, ones) * inv_n + NORM_EPS) * kw
    return qn, kn


def _swa_prompt_steps(q, k, v, qw, kw, sinks_ref, kprev, kprev_rot, vprev_t, has_prev, out):
    ones = _head_ones()
    half0 = lax.broadcasted_iota(jnp.int32, (1, LANES), 1) < HEAD_DIM
    qn, kn = _qk_norm(q, k, qw, kw, ones)
    kn_b = kn.astype(BF16)
    kn_rot = pltpu.roll(kn, HEAD_DIM, 1).astype(BF16)
    v_t = v.T.astype(BF16)
    keys = jnp.concatenate([kprev[...], kn_b], axis=0)
    keys_rot = jnp.concatenate([kprev_rot[...], kn_rot], axis=0)
    vals_t = jnp.concatenate([vprev_t[...], v_t], axis=1)
    kprev[...] = kn_b
    kprev_rot[...] = kn_rot
    vprev_t[...] = v_t
    kj = lax.broadcasted_iota(jnp.int32, (2 * WINDOW, WINDOW), 0)
    qi = lax.broadcasted_iota(jnp.int32, (2 * WINDOW, WINDOW), 1)
    valid_t = (kj > qi) & (kj <= qi + WINDOW) & ((kj >= WINDOW) | has_prev)
    heads = range(N_Q_HEADS)
    yield
    qm = [jnp.where(half0 if h % 2 == 0 else jnp.logical_not(half0), qn[:, (h // 2) * LANES:(h // 2 + 1) * LANES],
                    0.0).astype(BF16) for h in heads]
    straight = [h for h in heads if h // Q_PER_KV == h % 2]
    swapped = [h for h in heads if h // Q_PER_KV != h % 2]
    scores = [None] * N_Q_HEADS
    for group, kmat in ((straight, keys), (swapped, keys_rot)):
        for h0, h1 in zip(group[0::2], group[1::2]):
            s2 = _dot_nt(kmat, jnp.concatenate([qm[h0], qm[h1]], axis=0))
            scores[h0] = jnp.where(valid_t, s2[:, :WINDOW], NEG_INF)
            scores[h1] = jnp.where(valid_t, s2[:, WINDOW:], NEG_INF)
    yield
    probs = []
    for h in heads:
        sink = sinks_ref[h:h + 1, 0:1]
        m = jnp.maximum(jnp.max(scores[h], axis=0, keepdims=True), sink)
        pr = jnp.exp(scores[h] - m)
        denom = jnp.sum(pr, axis=0, keepdims=True) + jnp.exp(sink - m)
        probs.append((pr * (1.0 / denom)).astype(BF16))
    yield
    tiles = []
    for t in range(D_ATTN // LANES):
        g = (2 * t) // Q_PER_KV
        o2 = jnp.dot(vals_t[g * HEAD_DIM:(g + 1) * HEAD_DIM, :], jnp.concatenate(probs[2 * t:2 * t + 2], axis=1),
                     preferred_element_type=F32)
        tiles.append(jnp.concatenate([o2[:, :WINDOW], o2[:, WINDOW:]], axis=0).T)
    out["o_a"] = jnp.concatenate(tiles, axis=1)
    out["kn"] = kn


def _attn_sample_steps(t_new, q_ref, k_ref, v_ref, ckt_ref, cvt_ref, qw_ref, kw_ref, sinks_ref,
                       o_ref, kot_ref, vot_ref):
    n_seq = q_ref.shape[0]
    wb = ckt_ref.shape[2]
    rows = N_Q_HEADS * t_new
    ones = _head_ones()
    half0 = lax.broadcasted_iota(jnp.int32, (1, LANES), 1) < HEAD_DIM
    qi_c = lax.broadcasted_iota(jnp.int32, (rows, wb), 0) % t_new
    kj_c = lax.broadcasted_iota(jnp.int32, (rows, wb), 1)
    valid_c = (qi_c + wb - kj_c) < WINDOW
    qi_n = lax.broadcasted_iota(jnp.int32, (rows, t_new), 0) % t_new
    kj_n = lax.broadcasted_iota(jnp.int32, (rows, t_new), 1)
    valid_n = kj_n <= qi_n
    sink = sinks_ref[:, 0:1]
    heads = range(N_Q_HEADS)
    swap = [h // Q_PER_KV != h % 2 for h in heads]

    keep = lax.broadcasted_iota(jnp.int32, (1, wb), 1) < wb - t_new
    sel_t = lax.broadcasted_iota(jnp.int32, (3 * t_new, wb), 0) % t_new
    sel_l = lax.broadcasted_iota(jnp.int32, (3 * t_new, wb), 1)
    sel3 = (sel_l == sel_t + (wb - t_new)).astype(BF16)

    def _place_new(x):
        hi = x.astype(BF16)
        r1 = x - hi.astype(F32)
        mid = r1.astype(BF16)
        lo = (r1 - mid.astype(F32)).astype(BF16)
        return lax.dot_general(jnp.concatenate([hi, mid, lo], axis=0), sel3, (((0,), (0,)), ((), ())),
                               preferred_element_type=F32)

    seqs = []
    for i in range(n_seq):
        qn, kn = _qk_norm(q_ref[i], k_ref[i], qw_ref[...], kw_ref[...], ones)
        pieces = []
        for h in heads:
            qm = jnp.where(half0 if h % 2 == 0 else jnp.logical_not(half0), qn[:, (h // 2) * LANES:(h // 2 + 1) * LANES],
                           0.0)
            pieces.append(pltpu.roll(qm, HEAD_DIM, 1) if swap[h] else qm)
        seqs.append(dict(q=jnp.concatenate(pieces, axis=0).astype(BF16), kn=kn, v=v_ref[i],
                         ckt=ckt_ref[i], cvt=cvt_ref[i]))
    yield
    for s in seqs:
        s["s_c"] = jnp.where(valid_c, _dot(s["q"], s["ckt"]), NEG_INF)
        s["s_n"] = jnp.where(valid_n, _dot_nt(s["q"], s["kn"]), NEG_INF)
    yield
    for s in seqs:
        m = jnp.maximum(jnp.maximum(jnp.max(s["s_c"], axis=-1, keepdims=True),
                                    jnp.max(s["s_n"], axis=-1, keepdims=True)), sink)
        p_c = jnp.exp(s["s_c"] - m)
        p_n = jnp.exp(s["s_n"] - m)
        denom = jnp.sum(p_c, axis=-1, keepdims=True) + jnp.sum(p_n, axis=-1, keepdims=True) + jnp.exp(sink - m)
        s["p_c"], s["p_n"], s["inv"] = p_c, p_n, 1.0 / denom
    yield
    for i, s in enumerate(seqs):
        o = (_dot_nt(s["p_c"], s["cvt"]) + _dot(s["p_n"], s["v"])) * s["inv"]
        tiles = []
        for t in range(D_ATTN // LANES):
            pair = []
            for h in (2 * t, 2 * t + 1):
                o_h = o[h * t_new:(h + 1) * t_new]
                pair.append(pltpu.roll(o_h, HEAD_DIM, 1) if swap[h] else o_h)
            tiles.append(jnp.where(half0, pair[0], pair[1]))
        o_ref[i] = jnp.concatenate(tiles, axis=1)
        kot_ref[i] = jnp.where(keep, pltpu.roll(s["ckt"], wb - t_new, 1), _place_new(s["kn"]))
        vot_ref[i] = jnp.where(keep, pltpu.roll(s["cvt"], wb - t_new, 1), _place_new(s["v"]))


def _sample_mix_kernel(bb, tb, f_ref, prev0_ref, *rest):
    vec_refs = rest[:9]
    (s0_ref, q_ref, k_ref, v_ref, ckt_ref, cvt_ref, qw_ref, kw_ref, sinks_ref,
     or_ref, s_out_ref, oa_ref, kot_ref, vot_ref, s_scr) = rest[9:]
    attn = _attn_sample_steps(tb, q_ref, k_ref, v_ref, ckt_ref, cvt_ref, qw_ref, kw_ref, sinks_ref,
                              oa_ref, kot_ref, vot_ref)
    _load_states(s0_ref, s_scr, bb)
    or_ref[...] = _time_mix(f_ref[...], [prev0_ref[b] for b in range(bb)], tb, tb, HEAD_DIM, vec_refs, s_scr,
                            functools.partial(next, attn, None))
    for _ in attn:
        pass
    _store_states(s_scr, s_out_ref, bb)


def _sample_mix(f, prev0, s0, vecs, q, k, v, ckt, cvt, qw, kw, sinks_rows):
    b, t_new, _ = q.shape
    wb = ckt.shape[2]
    gb = SAMPLE_GROUP
    n_rows = gb * t_new
    assert n_rows % HEAD_DIM == 0 and HEAD_DIM % t_new == 0
    spec = lambda r, w: pl.BlockSpec((gb, r, w), lambda i: (i, 0, 0))
    const = lambda r, w: pl.BlockSpec((r, w), lambda i: (0, 0))
    state_spec = pl.BlockSpec((gb, 2 * N_PAIRS, HEAD_DIM, HEAD_DIM), lambda i: (i, 0, 0, 0))
    return pl.pallas_call(
        functools.partial(_sample_mix_kernel, gb, t_new),
        grid=(b // gb,),
        in_specs=([pl.BlockSpec((n_rows, D_SHIFT), lambda i: (i, 0)), spec(1, D_SHIFT)] + _vec_specs(lambda i: (0, 0))
                  + [state_spec, spec(t_new, D_ATTN), spec(t_new, D_KV), spec(t_new, D_KV), spec(D_KV, wb),
                     spec(D_KV, wb), const(1, D_ATTN), const(1, D_KV), const(N_Q_HEADS * t_new, LANES)]),
        out_specs=[pl.BlockSpec((n_rows, D_RWKV), lambda i: (i, 0)), state_spec,
                   spec(t_new, D_ATTN), spec(D_KV, wb), spec(D_KV, wb)],
        out_shape=[jax.ShapeDtypeStruct((b * t_new, D_RWKV), F32),
                   jax.ShapeDtypeStruct((b, 2 * N_PAIRS, HEAD_DIM, HEAD_DIM), F32),
                   jax.ShapeDtypeStruct((b, t_new, D_ATTN), F32),
                   jax.ShapeDtypeStruct((b, D_KV, wb), F32),
                   jax.ShapeDtypeStruct((b, D_KV, wb), F32)],
        scratch_shapes=[pltpu.VMEM((gb * N_PAIRS, LANES, LANES), F32)],
        compiler_params=pltpu.CompilerParams(dimension_semantics=("arbitrary",), vmem_limit_bytes=VMEM_LIMIT),
        name="sample_mix",
    )(f, prev0, *vecs, s0, q, k, v, ckt, cvt, qw, kw, sinks_rows)


def _merge_rows(x, o_r, z_r, o_a, z_a, p, wout_ref, g_ref, wgate_ref, wproj_ref, fill=lambda: None):
    gr = o_r * jax.nn.silu(z_r)
    ga = o_a * jax.nn.silu(z_a)
    ple = _dot(p, wproj_ref[...])
    h = x + _dot(gr, wout_ref[0:D_RWKV, :]) + _dot(ga, wout_ref[D_RWKV:D_MODEL, :])
    fill()
    gate = jax.nn.sigmoid(_dot(_rms_rows(h, g_ref[...]), wgate_ref[...]))
    fill()
    return h + gate * ple


def _merge_kernel(x_ref, or_ref, zr_ref, oa_ref, za_ref, p_ref, wout_ref, g_ref, wgate_ref, wproj_ref, y_ref):
    y_ref[...] = _merge_rows(x_ref[...], or_ref[...], zr_ref[...], oa_ref[...], za_ref[...], p_ref[...],
                             wout_ref, g_ref, wgate_ref, wproj_ref)


def _merge(x, o_r, z_r, o_a, z_a, p, w_out, g_ple, w_gate, w_proj, tm):
    m = x.shape[0]
    tok = lambda w: pl.BlockSpec((tm, w), lambda i: (i, 0))
    const = lambda r, w: pl.BlockSpec((r, w), lambda i: (0, 0))
    return pl.pallas_call(
        _merge_kernel,
        grid=(m // tm,),
        in_specs=[tok(D_MODEL), tok(D_RWKV), tok(D_RWKV), tok(D_ATTN), tok(D_ATTN), tok(D_PLE),
                  const(D_MODEL, D_MODEL), const(1, D_MODEL), const(D_MODEL, D_MODEL), const(D_PLE, D_MODEL)],
        out_specs=tok(D_MODEL),
        out_shape=jax.ShapeDtypeStruct((m, D_MODEL), F32),
        compiler_params=pltpu.CompilerParams(dimension_semantics=("arbitrary",), vmem_limit_bytes=VMEM_LIMIT),
        name="merge",
    )(x, o_r, z_r, o_a, z_a, p, w_out, g_ple, w_gate, w_proj)


_IN_CHUNK = 512
PROMPT_ROWS = 2
_FILL_HEAD = 3
_FILL_TAIL = 2


def _fill_order(n_proj, n_attn):
    mid = n_proj - _FILL_HEAD - _FILL_TAIL
    order = ["p"] * _FILL_HEAD
    done = 0
    for i in range(n_attn):
        while done < mid and done * n_attn <= i * mid:
            order.append("p")
            done += 1
        order.append("a")
    return order + ["p"] * (mid - done + _FILL_TAIL)


def _prompt_layer_kernel(nblk, x_ref, p_ref, gn_ref, win_ref, *rest):
    vec_refs = rest[:9]
    (qw_ref, kw_ref, sinks_ref, wout_ref, gp_ref, wgate_ref, wproj_ref,
     y_ref, s_out_ref, shift_ref, kc_ref, vc_ref,
     hbuf, xbuf, s_scr, prev_scr, kprev, kprev_rot, vprev_t) = rest[9:]
    bb = x_ref.shape[0]
    n_rows = bb * WINDOW
    s = pl.program_id(0)
    slot = s % 2
    j = jnp.maximum(s - 1, 0) % nblk

    @pl.when(s == 0)
    def _():
        hbuf[1] = jnp.zeros(hbuf.shape[1:], F32)
        xbuf[1] = jnp.zeros(xbuf.shape[1:], F32)

    @pl.when(j == 0)
    def _():
        s_scr[...] = jnp.zeros(s_scr.shape, F32)
        prev_scr[...] = jnp.zeros(prev_scr.shape, F32)
        kprev[...] = jnp.zeros(kprev.shape, BF16)
        kprev_rot[...] = jnp.zeros(kprev_rot.shape, BF16)
        vprev_t[...] = jnp.zeros(vprev_t.shape, BF16)

    h_cur = hbuf.at[1 - slot]
    x_new = x_ref[...].reshape(n_rows, D_MODEL)
    xbuf[slot] = x_new
    xn = _rms_rows(x_new, gn_ref[...]).astype(BF16)
    h_new = hbuf.at[slot]

    def project(c0):
        c1 = min(c0 + _IN_CHUNK, D_IN)
        h_new[:, c0:c1] = jnp.dot(xn, win_ref[:, c0:c1], preferred_element_type=F32)

    offs = [0]
    for w in _IN_SPLITS:
        offs.append(offs[-1] + w)
    part = lambda i: h_cur[:, offs[i]:offs[i + 1]]
    q, k, v = part(2), part(3), part(4)
    attn = [{} for _ in range(bb)]
    attn_steps = []
    for b in range(bb):
        rows = slice(b * WINDOW, (b + 1) * WINDOW)
        attn_steps.append(_swa_prompt_steps(q[rows], k[rows], v[rows], qw_ref[...], kw_ref[...], sinks_ref,
                                            kprev.at[b], kprev_rot.at[b], vprev_t.at[b], j > 0, attn[b]))
    proj_items = [functools.partial(project, c0) for c0 in range(0, D_IN, _IN_CHUNK)]
    attn_items = [functools.partial(next, attn_steps[b], None) for _ in range(4) for b in range(bb)]
    queue = [proj_items.pop(0) if kind == "p" else attn_items.pop(0)
             for kind in _fill_order(len(proj_items), len(attn_items))]
    assert not proj_items and not attn_items

    def fill():
        if queue:
            queue.pop(0)()

    for _ in range(_FILL_HEAD):
        fill()
    f = part(0)
    o_r = _time_mix(f, [prev_scr[b] for b in range(bb)], WINDOW, PROMPT_CHUNK, HEAD_DIM, vec_refs, s_scr, fill)
    for b in range(bb):
        prev_scr[b] = f[(b + 1) * WINDOW - 1:(b + 1) * WINDOW, :]
    while len(queue) > _FILL_TAIL:
        fill()
    o_a = jnp.concatenate([attn[b]["o_a"] for b in range(bb)], axis=0)
    y = _merge_rows(xbuf[1 - slot], o_r, part(1), o_a, part(5), p_ref[...].reshape(n_rows, D_PLE), wout_ref, gp_ref,
                    wgate_ref, wproj_ref, fill)
    y_ref[...] = y.reshape(y_ref.shape)
    while queue:
        fill()

    @pl.when((s > 0) & (j == nblk - 1))
    def _():
        _store_states(s_scr, s_out_ref, bb)
        shift_ref[...] = prev_scr[...]
        for b in range(bb):
            kc_ref[b] = attn[b]["kn"].T
            vc_ref[b] = v[b * WINDOW:(b + 1) * WINDOW].T


def _prompt_layer(x, p, gn, w_in_b, vecs, qw, kw, sinks_b, w_out_b, gp, w_gate_b, w_proj_b):
    batch, seq, _ = x.shape
    bb = PROMPT_ROWS
    nblk = seq // WINDOW
    n_steps = (batch // bb) * nblk
    const = lambda r, w: pl.BlockSpec((r, w), lambda s: (0, 0))
    cur = lambda s: jnp.minimum(s, n_steps - 1)
    prv = lambda s: jnp.maximum(s - 1, 0)
    per_row = lambda *shape: pl.BlockSpec((bb,) + shape, lambda s: (prv(s) // nblk,) + (0,) * len(shape))
    in_specs = ([pl.BlockSpec((bb, WINDOW, D_MODEL), lambda s: (cur(s) // nblk, cur(s) % nblk, 0)),
                 pl.BlockSpec((bb, WINDOW, D_PLE), lambda s: (prv(s) // nblk, prv(s) % nblk, 0)),
                 const(1, D_MODEL), const(D_MODEL, D_IN)]
                + _vec_specs(lambda s: (0, 0))
                + [const(1, D_ATTN), const(1, D_KV), const(N_Q_HEADS, LANES), const(D_MODEL, D_MODEL),
                   const(1, D_MODEL), const(D_MODEL, D_MODEL), const(D_PLE, D_MODEL)])
    return pl.pallas_call(
        functools.partial(_prompt_layer_kernel, nblk),
        grid=(n_steps + 1,),
        in_specs=in_specs,
        out_specs=[pl.BlockSpec((bb, WINDOW, D_MODEL), lambda s: (prv(s) // nblk, prv(s) % nblk, 0)),
                   per_row(2 * N_PAIRS, HEAD_DIM, HEAD_DIM), per_row(1, D_SHIFT),
                   per_row(D_KV, WINDOW), per_row(D_KV, WINDOW)],
        out_shape=[jax.ShapeDtypeStruct((batch, seq, D_MODEL), F32),
                   jax.ShapeDtypeStruct((batch, 2 * N_PAIRS, HEAD_DIM, HEAD_DIM), F32),
                   jax.ShapeDtypeStruct((batch, 1, D_SHIFT), F32),
                   jax.ShapeDtypeStruct((batch, D_KV, WINDOW), F32),
                   jax.ShapeDtypeStruct((batch, D_KV, WINDOW), F32)],
        scratch_shapes=[pltpu.VMEM((2, bb * WINDOW, D_IN), F32), pltpu.VMEM((2, bb * WINDOW, D_MODEL), F32),
                        pltpu.VMEM((bb * N_PAIRS, LANES, LANES), F32), pltpu.VMEM((bb, 1, D_SHIFT), F32),
                        pltpu.VMEM((bb, WINDOW, D_KV), BF16), pltpu.VMEM((bb, WINDOW, D_KV), BF16),
                        pltpu.VMEM((bb, D_KV, WINDOW), BF16)],
        compiler_params=pltpu.CompilerParams(dimension_semantics=("arbitrary",), vmem_limit_bytes=VMEM_LIMIT),
        name="prompt_layer",
    )(x, p, gn, w_in_b, *vecs, qw, kw, sinks_b, w_out_b, gp, w_gate_b, w_proj_b)


def kernel(x_prompt, x_sample, state_rwkv, state_shift, cache_k, cache_v, p_prompt, p_sample, g_norm, w_in, mu_shift, w0, w_dec2, a0, w_a2, k_k, k_a, r_k, lnx_w, lnx_b, q_norm_w, k_norm_w, sinks, w_out, g_ple, w_ple_gate, w_ple_proj):
    depth = w_in.shape[0]
    bp, seq, _ = x_prompt.shape
    bs, dec, _ = x_sample.shape
    xp = x_prompt
    xs = x_sample.reshape(bs * dec, D_MODEL)
    outs = [[] for _ in range(8)]
    for i in range(depth):
        w_in_b = w_in[i].astype(BF16)
        w_out_b = w_out[i].astype(BF16)
        w_gate_b = w_ple_gate[i].astype(BF16)
        w_proj_b = w_ple_proj[i].astype(BF16)
        zl = jnp.zeros((D_LORA, D_RWKV), F32)
        lora_w = jnp.concatenate([jnp.concatenate([w_dec2[i], zl], axis=1),
                                  jnp.concatenate([zl, w_a2[i]], axis=1)], axis=0).astype(BF16)
        row = lambda t: t.reshape(1, -1)
        vecs = (row(mu_shift[i]), row(w0[i]), row(a0[i]), lora_w, row(k_k[i]), row(k_a[i]), row(r_k[i]),
                row(lnx_w[i]), row(lnx_b[i]))
        qw = row(jnp.tile(q_norm_w[i], N_Q_HEADS))
        kw = row(jnp.tile(k_norm_w[i], N_KV_HEADS))
        sinks_b = jnp.broadcast_to(sinks[i][:, None], (N_Q_HEADS, LANES))
        gn, gp = row(g_norm[i]), row(g_ple[i])

        to_t = lambda c: jnp.transpose(c, (0, 2, 3, 1)).reshape(c.shape[0], D_KV, c.shape[1])
        from_t = lambda c: jnp.transpose(c.reshape(c.shape[0], N_KV_HEADS, HEAD_DIM, c.shape[2]), (0, 3, 1, 2))

        xp, s_p, sh_p, kc, vc = _prompt_layer(xp, p_prompt[i], gn, w_in_b, vecs, qw, kw, sinks_b, w_out_b, gp,
                                              w_gate_b, w_proj_b)
        outs[0].append(s_p)
        outs[2].append(sh_p)
        outs[4].append(from_t(kc))
        outs[6].append(from_t(vc))

        f, z_r, q, k, v, z_a = _in_proj(xs, gn, w_in_b, 512)
        o_r, s_s, o_a, k_buf, v_buf = _sample_mix(
            f, state_shift[i], state_rwkv[i], vecs, q.reshape(bs, dec, D_ATTN), k.reshape(bs, dec, D_KV),
            v.reshape(bs, dec, D_KV), to_t(cache_k[i]), to_t(cache_v[i]), qw, kw, jnp.repeat(sinks_b, dec, axis=0))
        outs[1].append(s_s)
        outs[3].append(f.reshape(bs, dec, D_SHIFT)[:, -1:])
        outs[5].append(from_t(k_buf))
        outs[7].append(from_t(v_buf))
        xs = _merge(xs, o_r, z_r, o_a.reshape(bs * dec, D_ATTN), z_a, p_sample[i].reshape(bs * dec, D_PLE),
                    w_out_b, gp, w_gate_b, w_proj_b, 512)
    st = lambda l: jnp.stack(l)
    return (xp, xs.reshape(bs, dec, D_MODEL),
            st(outs[0]), st(outs[1]), st(outs[2]), st(outs[3]), st(outs[4]), st(outs[5]), st(outs[6]), st(outs[7]))
```

```python
import functools
import math

import jax
import jax.numpy as jnp
from jax import lax
from jax.experimental import pallas as pl
from jax.experimental.pallas import tpu as pltpu

F32 = jnp.float32
BF16 = jnp.bfloat16

D_MODEL = 1024
HEAD_DIM = 64
D_RWKV = 512
D_ATTN = 512
N_KV_HEADS = 2
N_Q_HEADS = 8
Q_PER_KV = N_Q_HEADS // N_KV_HEADS
D_KV = N_KV_HEADS * HEAD_DIM
WINDOW = 128
D_LORA = 64
D_SHIFT = 3 * D_RWKV + 2 * D_LORA
D_PLE = 256
D_IN = D_SHIFT + D_RWKV + D_ATTN + 2 * D_KV + D_ATTN
NORM_EPS = 1e-6
LNX_EPS = 64e-5
NEG_INF = -1e30

LANES = 128
N_PAIRS = D_RWKV // LANES
PROMPT_CHUNK = 64
SAMPLE_GROUP = 16
VMEM_LIMIT = 56 * 1024 * 1024


def _dot(a, b):
    return jnp.dot(a.astype(BF16), b.astype(BF16), preferred_element_type=F32)


def _dot_nt(a, b):
    return lax.dot_general(a.astype(BF16), b.astype(BF16), (((1,), (1,)), ((), ())), preferred_element_type=F32)


def _dot_tn(a, b):
    return lax.dot_general(a.astype(BF16), b.astype(BF16), (((0,), (0,)), ((), ())), preferred_element_type=F32)


def _segment_cumsum(x, seg):
    blk = min(max(seg, HEAD_DIM), x.shape[0])
    ti = lax.broadcasted_iota(jnp.int32, (blk, 3 * blk), 0)
    tj = lax.broadcasted_iota(jnp.int32, (blk, 3 * blk), 1) % blk
    tri3 = ((ti // seg == tj // seg) & (tj <= ti)).astype(BF16)
    hi = x.astype(BF16)
    r1 = x - hi.astype(F32)
    mid = r1.astype(BF16)
    lo = (r1 - mid.astype(F32)).astype(BF16)
    parts = []
    for r0 in range(0, x.shape[0], blk):
        rows = slice(r0, r0 + blk)
        parts.append(jnp.dot(tri3, jnp.concatenate([hi[rows], mid[rows], lo[rows]], axis=0),
                             preferred_element_type=F32))
    return parts[0] if len(parts) == 1 else jnp.concatenate(parts, axis=0)


def _head_ones():
    r = (lax.broadcasted_iota(jnp.int32, (2 * LANES, LANES), 0) % LANES) // HEAD_DIM
    c = lax.broadcasted_iota(jnp.int32, (2 * LANES, LANES), 1) // HEAD_DIM
    return (r == c).astype(BF16)


def _head_sum(x, ones):
    tiles = []
    for i in range(0, x.shape[1], LANES):
        xt = x[:, i:i + LANES]
        hi = xt.astype(BF16)
        lo = (xt - hi.astype(F32)).astype(BF16)
        tiles.append(jnp.dot(jnp.concatenate([hi, lo], axis=1), ones, preferred_element_type=F32))
    return tiles[0] if len(tiles) == 1 else jnp.concatenate(tiles, axis=1)


def _rms_rows(x, g):
    return x * lax.rsqrt(jnp.mean(x * x, axis=-1, keepdims=True) + NORM_EPS) * g


_IN_SPLITS = (D_SHIFT, D_RWKV, D_ATTN, D_KV, D_KV, D_ATTN)


def _in_proj_kernel(x_ref, g_ref, w_ref, *out_refs):
    h = _dot(_rms_rows(x_ref[...], g_ref[...]), w_ref[...])
    off = 0
    for o_ref, width in zip(out_refs, _IN_SPLITS):
        o_ref[...] = h[:, off:off + width]
        off += width


def _in_proj(x, g_norm, w_in_bf16, tm):
    m = x.shape[0]
    return pl.pallas_call(
        _in_proj_kernel,
        grid=(m // tm,),
        in_specs=[pl.BlockSpec((tm, D_MODEL), lambda i: (i, 0)),
                  pl.BlockSpec((1, D_MODEL), lambda i: (0, 0)),
                  pl.BlockSpec((D_MODEL, D_IN), lambda i: (0, 0))],
        out_specs=[pl.BlockSpec((tm, w), lambda i: (i, 0)) for w in _IN_SPLITS],
        out_shape=[jax.ShapeDtypeStruct((m, w), F32) for w in _IN_SPLITS],
        compiler_params=pltpu.CompilerParams(dimension_semantics=("arbitrary",), vmem_limit_bytes=VMEM_LIMIT),
        name="in_proj",
    )(x, g_norm, w_in_bf16)


def _stack(z, half0):
    return jnp.concatenate([jnp.where(half0, z, 0.0), jnp.where(half0, 0.0, z)], axis=0).astype(BF16)


def _rwkv_recurrence(at, rt, bt, kt, v, e_cum, s_scr, cm, seg, segs_per_state, fill):
    n_rows = at.shape[0]
    n_blk = n_rows // cm
    lane = lax.broadcasted_iota(jnp.int32, (1, LANES), 1)
    half0 = lane < HEAD_DIM
    ri = lax.broadcasted_iota(jnp.int32, (cm, LANES), 0)
    ci = lax.broadcasted_iota(jnp.int32, (cm, LANES), 1) % cm
    same = (ri // seg) == (ci // seg)
    tri_strict = same & (ci < ri)
    tri_incl = same & (ci <= ri)
    eye_c = (ri == ci).astype(F32)
    sr = lax.broadcasted_iota(jnp.int32, (LANES, LANES), 0) // HEAD_DIM
    sc = lax.broadcasted_iota(jnp.int32, (LANES, LANES), 1) // HEAD_DIM
    state_mask = sr == sc
    n_levels = max(int(math.log2(seg)) - 1, 0)
    blocks = [(rb, p) for rb in range(n_blk) for p in range(N_PAIRS)]

    def tile(x, rb, p):
        return x[rb * cm:(rb + 1) * cm, p * LANES:(p + 1) * LANES]

    ops = {}
    for rb, p in blocks:
        at_p, rt_p, bt_p, kt_p, v_p = (tile(x, rb, p) for x in (at, rt, bt, kt, v))
        ops[rb, p] = dict(at=at_p, rt=rt_p, bt=bt_p, kt=kt_p, v=v_p, v_s=_stack(v_p, half0))
    for blk in blocks:
        o = ops[blk]
        g = _dot_nt(jnp.concatenate([o["at"], o["rt"]], axis=0),
                    jnp.concatenate([_stack(o["bt"], half0), _stack(o["kt"], half0)], axis=0))
        o["g_ab"] = jnp.where(tri_strict, g[:cm, :LANES], 0.0)
        o["g_ak"] = jnp.where(tri_strict, g[:cm, LANES:], 0.0)
        o["g_r"] = jnp.concatenate([jnp.where(tri_incl, g[cm:, :LANES], 0.0),
                                    jnp.where(tri_incl, g[cm:, LANES:], 0.0)], axis=1).astype(BF16)
    fill()

    for blk in blocks:
        ops[blk]["t_inv"] = eye_c + ops[blk]["g_ab"]
    if n_levels > 0:
        for blk in blocks:
            ops[blk]["a_pow"] = _dot(ops[blk]["g_ab"], _stack(ops[blk]["g_ab"], half0))
        fill()
        for lvl in range(n_levels):
            last = lvl == n_levels - 1
            for blk in blocks:
                o = ops[blk]
                if last:
                    o["t_inv"] = o["t_inv"] + _dot(o["a_pow"], _stack(o["t_inv"], half0))
                else:
                    m = _dot(o["a_pow"], jnp.concatenate([_stack(o["a_pow"], half0), _stack(o["t_inv"], half0)], axis=1))
                    o["t_inv"] = o["t_inv"] + m[:, LANES:]
                    o["a_pow"] = m[:, :LANES]
            fill()

    for blk in blocks:
        o = ops[blk]
        gakv = _dot(o["g_ak"], o["v_s"])
        z = _dot(o["t_inv"], jnp.concatenate([_stack(o["at"], half0), _stack(gakv, half0)], axis=1))
        o["a_hat"], o["p0"] = z[:, :LANES], z[:, LANES:]
    fill()

    n_seg = n_rows // seg
    n_states = n_seg // segs_per_state
    per_blk = cm // seg
    p_parts = {blk: [None] * per_blk for blk in blocks}
    y_parts = {blk: [None] * per_blk for blk in blocks}
    for step in range(segs_per_state):
        segs = [st * segs_per_state + step for st in range(n_states)]
        proj = {}
        for g_i in segs:
            rb, off = (g_i * seg) // cm, (g_i * seg) % cm
            for p in range(N_PAIRS):
                o = ops[rb, p]
                lhs = jnp.concatenate([o["a_hat"][off:off + seg], o["rt"][off:off + seg]], axis=0)
                proj[g_i, p] = _dot_nt(lhs, s_scr[(g_i // segs_per_state) * N_PAIRS + p])
        for g_i in segs:
            rb, off = (g_i * seg) // cm, (g_i * seg) % cm
            for p in range(N_PAIRS):
                o = ops[rb, p]
                p_seg = proj[g_i, p][:seg] + o["p0"][off:off + seg]
                p_parts[rb, p][off // seg] = p_seg
                y_parts[rb, p][off // seg] = proj[g_i, p][seg:]
                upd = _dot_tn(jnp.concatenate([p_seg, o["v"][off:off + seg]], axis=0),
                              jnp.concatenate([o["bt"][off:off + seg], o["kt"][off:off + seg]], axis=0))
                si = (g_i // segs_per_state) * N_PAIRS + p
                row_end = g_i * seg + seg - 1
                w_end = e_cum[row_end:row_end + 1, p * LANES:(p + 1) * LANES]
                s_scr[si] = w_end * (s_scr[si] + jnp.where(state_mask, upd, 0.0))

    cat = lambda parts: parts[0] if len(parts) == 1 else jnp.concatenate(parts, axis=0)
    rows = []
    for rb in range(n_blk):
        tiles = []
        for p in range(N_PAIRS):
            o = ops[rb, p]
            pv_s = jnp.concatenate([_stack(cat(p_parts[rb, p]), half0), o["v_s"]], axis=0)
            tiles.append(cat(y_parts[rb, p]) + jnp.dot(o["g_r"], pv_s, preferred_element_type=F32))
        rows.append(jnp.concatenate(tiles, axis=1))
    return cat(rows)


def _time_mix(f, prev_rows, tb, seg, cm, vec_refs, s_scr, fill=lambda: None):
    mu_ref, w0_ref, a0_ref, lora_ref, kk_ref, ka_ref, rk_ref, lnw_ref, lnb_ref = vec_refs
    n_rows = f.shape[0]
    row = lax.broadcasted_iota(jnp.int32, (n_rows, 1), 0)
    f_prev = pltpu.roll(f, 1, 0)
    for b, prev in enumerate(prev_rows):
        f_prev = jnp.where(row == b * tb, prev, f_prev)
    fs = f + (f_prev - f) * mu_ref[...]
    r = fs[:, 0:D_RWKV]
    k = fs[:, D_RWKV:2 * D_RWKV]
    v = fs[:, 2 * D_RWKV:3 * D_RWKV]
    wa = fs[:, 3 * D_RWKV:D_SHIFT]
    lane = lax.broadcasted_iota(jnp.int32, (1, LANES), 1)
    lora = _dot(jnp.where(lane < D_LORA, jnp.tanh(wa), wa), lora_ref[...])
    lw = (-math.exp(-0.5)) * jax.nn.sigmoid(w0_ref[...] + lora[:, 0:D_RWKV])
    a_sig = jax.nn.sigmoid(a0_ref[...] + lora[:, D_RWKV:2 * D_RWKV])
    ones = _head_ones()
    kk = k * kk_ref[...]
    kk = kk * lax.rsqrt(jnp.maximum(_head_sum(kk * kk, ones), 1e-24))
    k2 = k * (1.0 + (a_sig - 1.0) * ka_ref[...])

    cum = _segment_cumsum(lw, seg)
    e_cum = jnp.exp(cum)
    e_inv = jnp.exp(-cum)
    y = _rwkv_recurrence(-kk * jnp.exp(cum - lw), r * e_cum, kk * a_sig * e_inv, k2 * e_inv, v, e_cum,
                         s_scr, cm, seg, tb // seg, fill)

    inv_n = 1.0 / HEAD_DIM
    yc = y - _head_sum(y, ones) * inv_n
    var = _head_sum(yc * yc, ones) * inv_n
    yn = yc * lax.rsqrt(var + LNX_EPS) * lnw_ref[...] + lnb_ref[...]
    bonus = _head_sum(r * k2 * rk_ref[...], ones) * v
    return yn + bonus


def _load_states(s0_ref, s_scr, bb):
    zero = jnp.zeros((HEAD_DIM, HEAD_DIM), F32)
    for b in range(bb):
        for p in range(N_PAIRS):
            top = jnp.concatenate([s0_ref[b, 2 * p], zero], axis=1)
            bot = jnp.concatenate([zero, s0_ref[b, 2 * p + 1]], axis=1)
            s_scr[b * N_PAIRS + p] = jnp.concatenate([top, bot], axis=0)


def _store_states(s_scr, s_out_ref, bb):
    for b in range(bb):
        for p in range(N_PAIRS):
            s = s_scr[b * N_PAIRS + p]
            s_out_ref[b, 2 * p] = s[:HEAD_DIM, :HEAD_DIM]
            s_out_ref[b, 2 * p + 1] = s[HEAD_DIM:, HEAD_DIM:]


def _vec_specs(index_map):
    vec = lambda n: pl.BlockSpec((1, n), index_map)
    return [vec(D_SHIFT), vec(D_RWKV), vec(D_RWKV), pl.BlockSpec((LANES, 2 * D_RWKV), index_map),
            vec(D_RWKV), vec(D_RWKV), vec(D_RWKV), vec(D_RWKV), vec(D_RWKV)]


def _qk_norm(q, k, qw, kw, ones):
    inv_n = 1.0 / HEAD_DIM
    qn = q * lax.rsqrt(_head_sum(q * q, ones) * inv_n + NORM_EPS) * (qw * (HEAD_DIM ** -0.5))
    kn = k * lax.rsqrt(_head_sum(k * k, ones) * inv_n + NORM_EPS) * kw
    return qn, kn


def _swa_prompt_steps(q, k, v, qw, kw, sinks_ref, kprev, kprev_rot, vprev_t, has_prev, out):
    ones = _head_ones()
    half0 = lax.broadcasted_iota(jnp.int32, (1, LANES), 1) < HEAD_DIM
    qn, kn = _qk_norm(q, k, qw, kw, ones)
    kn_b = kn.astype(BF16)
    kn_rot = pltpu.roll(kn, HEAD_DIM, 1).astype(BF16)
    v_t = v.T.astype(BF16)
    keys = jnp.concatenate([kprev[...], kn_b], axis=0)
    keys_rot = jnp.concatenate([kprev_rot[...], kn_rot], axis=0)
    vals_t = jnp.concatenate([vprev_t[...], v_t], axis=1)
    kprev[...] = kn_b
    kprev_rot[...] = kn_rot
    vprev_t[...] = v_t
    kj = lax.broadcasted_iota(jnp.int32, (2 * WINDOW, WINDOW), 0)
    qi = lax.broadcasted_iota(jnp.int32, (2 * WINDOW, WINDOW), 1)
    valid_t = (kj > qi) & (kj <= qi + WINDOW) & ((kj >= WINDOW) | has_prev)
    heads = range(N_Q_HEADS)
    yield
    qm = [jnp.where(half0 if h % 2 == 0 else jnp.logical_not(half0), qn[:, (h // 2) * LANES:(h // 2 + 1) * LANES],
                    0.0).astype(BF16) for h in heads]
    straight = [h for h in heads if h // Q_PER_KV == h % 2]
    swapped = [h for h in heads if h // Q_PER_KV != h % 2]
    scores = [None] * N_Q_HEADS
    for group, kmat in ((straight, keys), (swapped, keys_rot)):
        for h0, h1 in zip(group[0::2], group[1::2]):
            s2 = _dot_nt(kmat, jnp.concatenate([qm[h0], qm[h1]], axis=0))
            scores[h0] = jnp.where(valid_t, s2[:, :WINDOW], NEG_INF)
            scores[h1] = jnp.where(valid_t, s2[:, WINDOW:], NEG_INF)
    yield
    probs = []
    for h in heads:
        sink = sinks_ref[h:h + 1, 0:1]
        m = jnp.maximum(jnp.max(scores[h], axis=0, keepdims=True), sink)
        pr = jnp.exp(scores[h] - m)
        denom = jnp.sum(pr, axis=0, keepdims=True) + jnp.exp(sink - m)
        probs.append((pr * (1.0 / denom)).astype(BF16))
    yield
    tiles = []
    for t in range(D_ATTN // LANES):
        g = (2 * t) // Q_PER_KV
        o2 = jnp.dot(vals_t[g * HEAD_DIM:(g + 1) * HEAD_DIM, :], jnp.concatenate(probs[2 * t:2 * t + 2], axis=1),
                     preferred_element_type=F32)
        tiles.append(jnp.concatenate([o2[:, :WINDOW], o2[:, WINDOW:]], axis=0).T)
    out["o_a"] = jnp.concatenate(tiles, axis=1)
    out["kn"] = kn


def _attn_sample_steps(t_new, q_ref, k_ref, v_ref, ckt_ref, cvt_ref, qw_ref, kw_ref, sinks_ref,
                       o_ref, kot_ref, vot_ref):
    n_seq = q_ref.shape[0]
    wb = ckt_ref.shape[2]
    rows = N_Q_HEADS * t_new
    ones = _head_ones()
    half0 = lax.broadcasted_iota(jnp.int32, (1, LANES), 1) < HEAD_DIM
    qi_c = lax.broadcasted_iota(jnp.int32, (rows, wb), 0) % t_new
    kj_c = lax.broadcasted_iota(jnp.int32, (rows, wb), 1)
    valid_c = (qi_c + wb - kj_c) < WINDOW
    qi_n = lax.broadcasted_iota(jnp.int32, (rows, t_new), 0) % t_new
    kj_n = lax.broadcasted_iota(jnp.int32, (rows, t_new), 1)
    valid_n = kj_n <= qi_n
    sink = sinks_ref[:, 0:1]
    heads = range(N_Q_HEADS)
    swap = [h // Q_PER_KV != h % 2 for h in heads]

    keep = lax.broadcasted_iota(jnp.int32, (1, wb), 1) < wb - t_new
    sel_t = lax.broadcasted_iota(jnp.int32, (3 * t_new, wb), 0) % t_new
    sel_l = lax.broadcasted_iota(jnp.int32, (3 * t_new, wb), 1)
    sel3 = (sel_l == sel_t + (wb - t_new)).astype(BF16)

    def _place_new(x):
        hi = x.astype(BF16)
        r1 = x - hi.astype(F32)
        mid = r1.astype(BF16)
        lo = (r1 - mid.astype(F32)).astype(BF16)
        return lax.dot_general(jnp.concatenate([hi, mid, lo], axis=0), sel3, (((0,), (0,)), ((), ())),
                               preferred_element_type=F32)

    seqs = []
    for i in range(n_seq):
        qn, kn = _qk_norm(q_ref[i], k_ref[i], qw_ref[...], kw_ref[...], ones)
        pieces = []
        for h in heads:
            qm = jnp.where(half0 if h % 2 == 0 else jnp.logical_not(half0), qn[:, (h // 2) * LANES:(h // 2 + 1) * LANES],
                           0.0)
            pieces.append(pltpu.roll(qm, HEAD_DIM, 1) if swap[h] else qm)
        seqs.append(dict(q=jnp.concatenate(pieces, axis=0).astype(BF16), kn=kn, v=v_ref[i],
                         ckt=ckt_ref[i], cvt=cvt_ref[i]))
    yield
    for s in seqs:
        s["s_c"] = jnp.where(valid_c, _dot(s["q"], s["ckt"]), NEG_INF)
        s["s_n"] = jnp.where(valid_n, _dot_nt(s["q"], s["kn"]), NEG_INF)
    yield
    for s in seqs:
        m = jnp.maximum(jnp.maximum(jnp.max(s["s_c"], axis=-1, keepdims=True),
                                    jnp.max(s["s_n"], axis=-1, keepdims=True)), sink)
        p_c = jnp.exp(s["s_c"] - m)
        p_n = jnp.exp(s["s_n"] - m)
        denom = jnp.sum(p_c, axis=-1, keepdims=True) + jnp.sum(p_n, axis=-1, keepdims=True) + jnp.exp(sink - m)
        s["p_c"], s["p_n"], s["inv"] = p_c, p_n, 1.0 / denom
    yield
    for i, s in enumerate(seqs):
        o = (_dot_nt(s["p_c"], s["cvt"]) + _dot(s["p_n"], s["v"])) * s["inv"]
        tiles = []
        for t in range(D_ATTN // LANES):
            pair = []
            for h in (2 * t, 2 * t + 1):
                o_h = o[h * t_new:(h + 1) * t_new]
                pair.append(pltpu.roll(o_h, HEAD_DIM, 1) if swap[h] else o_h)
            tiles.append(jnp.where(half0, pair[0], pair[1]))
        o_ref[i] = jnp.concatenate(tiles, axis=1)
        kot_ref[i] = jnp.where(keep, pltpu.roll(s["ckt"], wb - t_new, 1), _place_new(s["kn"]))
        vot_ref[i] = jnp.where(keep, pltpu.roll(s["cvt"], wb - t_new, 1), _place_new(s["v"]))


def _sample_layer_kernel(bb, tb, f_ref, prev0_ref, *rest):
    vec_refs = rest[:9]
    (s0_ref, q_ref, k_ref, v_ref, ckt_ref, cvt_ref, qw_ref, kw_ref, sinks_ref,
     x_ref, zr_ref, za_ref, p_ref, wout_ref, gp_ref, wgate_ref, wproj_ref,
     y_ref, s_out_ref, kot_ref, vot_ref, s_scr, oa_scr) = rest[9:]
    attn = _attn_sample_steps(tb, q_ref, k_ref, v_ref, ckt_ref, cvt_ref, qw_ref, kw_ref, sinks_ref,
                              oa_scr, kot_ref, vot_ref)
    _load_states(s0_ref, s_scr, bb)
    o_r = _time_mix(f_ref[...], [prev0_ref[b] for b in range(bb)], tb, tb, HEAD_DIM, vec_refs, s_scr,
                    functools.partial(next, attn, None))
    for _ in attn:
        pass
    _store_states(s_scr, s_out_ref, bb)
    y_ref[...] = _merge_rows(x_ref[...], o_r, zr_ref[...], oa_scr[...].reshape(bb * tb, D_ATTN), za_ref[...],
                             p_ref[...], wout_ref, gp_ref, wgate_ref, wproj_ref)


def _sample_layer(f, z_r, q, k, v, z_a, prev0, s0, vecs, ckt, cvt, qw, kw, sinks_rows, x, p, w_out, g_ple, w_gate,
                  w_proj):
    b, t_new, _ = q.shape
    wb = ckt.shape[2]
    gb = SAMPLE_GROUP
    n_rows = gb * t_new
    assert n_rows % HEAD_DIM == 0 and HEAD_DIM % t_new == 0
    spec = lambda r, w: pl.BlockSpec((gb, r, w), lambda i: (i, 0, 0))
    rows = lambda w: pl.BlockSpec((n_rows, w), lambda i: (i, 0))
    const = lambda r, w: pl.BlockSpec((r, w), lambda i: (0, 0))
    state_spec = pl.BlockSpec((gb, 2 * N_PAIRS, HEAD_DIM, HEAD_DIM), lambda i: (i, 0, 0, 0))
    return pl.pallas_call(
        functools.partial(_sample_layer_kernel, gb, t_new),
        grid=(b // gb,),
        in_specs=([rows(D_SHIFT), spec(1, D_SHIFT)] + _vec_specs(lambda i: (0, 0))
                  + [state_spec, spec(t_new, D_ATTN), spec(t_new, D_KV), spec(t_new, D_KV), spec(D_KV, wb),
                     spec(D_KV, wb), const(1, D_ATTN), const(1, D_KV), const(N_Q_HEADS * t_new, LANES),
                     rows(D_MODEL), rows(D_RWKV), rows(D_ATTN), rows(D_PLE), const(D_MODEL, D_MODEL),
                     const(1, D_MODEL), const(D_MODEL, D_MODEL), const(D_PLE, D_MODEL)]),
        out_specs=[rows(D_MODEL), state_spec, spec(D_KV, wb), spec(D_KV, wb)],
        out_shape=[jax.ShapeDtypeStruct((b * t_new, D_MODEL), F32),
                   jax.ShapeDtypeStruct((b, 2 * N_PAIRS, HEAD_DIM, HEAD_DIM), F32),
                   jax.ShapeDtypeStruct((b, D_KV, wb), F32),
                   jax.ShapeDtypeStruct((b, D_KV, wb), F32)],
        scratch_shapes=[pltpu.VMEM((gb * N_PAIRS, LANES, LANES), F32), pltpu.VMEM((gb, t_new, D_ATTN), F32)],
        compiler_params=pltpu.CompilerParams(dimension_semantics=("arbitrary",), vmem_limit_bytes=VMEM_LIMIT),
        name="sample_layer",
    )(f, prev0, *vecs, s0, q, k, v, ckt, cvt, qw, kw, sinks_rows, x, z_r, z_a, p, w_out, g_ple, w_gate, w_proj)


def _merge_rows(x, o_r, z_r, o_a, z_a, p, wout_ref, g_ref, wgate_ref, wproj_ref, fill=lambda: None):
    gr = o_r * jax.nn.silu(z_r)
    ga = o_a * jax.nn.silu(z_a)
    ple = _dot(p, wproj_ref[...])
    h = x + _dot(gr, wout_ref[0:D_RWKV, :]) + _dot(ga, wout_ref[D_RWKV:D_MODEL, :])
    fill()
    gate = jax.nn.sigmoid(_dot(_rms_rows(h, g_ref[...]), wgate_ref[...]))
    fill()
    return h + gate * ple


_IN_CHUNK = 512
PROMPT_ROWS = 2
_FILL_HEAD = 3
_FILL_TAIL = 2


def _fill_order(n_proj, n_attn):
    mid = n_proj - _FILL_HEAD - _FILL_TAIL
    order = ["p"] * _FILL_HEAD
    done = 0
    for i in range(n_attn):
        while done < mid and done * n_attn <= i * mid:
            order.append("p")
            done += 1
        order.append("a")
    return order + ["p"] * (mid - done + _FILL_TAIL)


def _prompt_layer_kernel(nblk, x_ref, p_ref, gn_ref, win_ref, *rest):
    vec_refs = rest[:9]
    (qw_ref, kw_ref, sinks_ref, wout_ref, gp_ref, wgate_ref, wproj_ref,
     y_ref, s_out_ref, shift_ref, kc_ref, vc_ref,
     hbuf, xbuf, s_scr, prev_scr, kprev, kprev_rot, vprev_t) = rest[9:]
    bb = x_ref.shape[0]
    n_rows = bb * WINDOW
    s = pl.program_id(0)
    slot = s % 2
    j = jnp.maximum(s - 1, 0) % nblk

    @pl.when(s == 0)
    def _():
        hbuf[1] = jnp.zeros(hbuf.shape[1:], F32)
        xbuf[1] = jnp.zeros(xbuf.shape[1:], F32)

    @pl.when(j == 0)
    def _():
        s_scr[...] = jnp.zeros(s_scr.shape, F32)
        prev_scr[...] = jnp.zeros(prev_scr.shape, F32)
        kprev[...] = jnp.zeros(kprev.shape, BF16)
        kprev_rot[...] = jnp.zeros(kprev_rot.shape, BF16)
        vprev_t[...] = jnp.zeros(vprev_t.shape, BF16)

    h_cur = hbuf.at[1 - slot]
    x_new = x_ref[...].reshape(n_rows, D_MODEL)
    xbuf[slot] = x_new
    xn = _rms_rows(x_new, gn_ref[...]).astype(BF16)
    h_new = hbuf.at[slot]

    def project(c0):
        c1 = min(c0 + _IN_CHUNK, D_IN)
        h_new[:, c0:c1] = jnp.dot(xn, win_ref[:, c0:c1], preferred_element_type=F32)

    offs = [0]
    for w in _IN_SPLITS:
        offs.append(offs[-1] + w)
    part = lambda i: h_cur[:, offs[i]:offs[i + 1]]
    q, k, v = part(2), part(3), part(4)
    attn = [{} for _ in range(bb)]
    attn_steps = []
    for b in range(bb):
        rows = slice(b * WINDOW, (b + 1) * WINDOW)
        attn_steps.append(_swa_prompt_steps(q[rows], k[rows], v[rows], qw_ref[...], kw_ref[...], sinks_ref,
                                            kprev.at[b], kprev_rot.at[b], vprev_t.at[b], j > 0, attn[b]))
    proj_items = [functools.partial(project, c0) for c0 in range(0, D_IN, _IN_CHUNK)]
    attn_items = [functools.partial(next, attn_steps[b], None) for _ in range(4) for b in range(bb)]
    queue = [proj_items.pop(0) if kind == "p" else attn_items.pop(0)
             for kind in _fill_order(len(proj_items), len(attn_items))]
    assert not proj_items and not attn_items

    def fill():
        if queue:
            queue.pop(0)()

    for _ in range(_FILL_HEAD):
        fill()
    f = part(0)
    o_r = _time_mix(f, [prev_scr[b] for b in range(bb)], WINDOW, PROMPT_CHUNK, HEAD_DIM, vec_refs, s_scr, fill)
    for b in range(bb):
        prev_scr[b] = f[(b + 1) * WINDOW - 1:(b + 1) * WINDOW, :]
    while len(queue) > _FILL_TAIL:
        fill()
    o_a = jnp.concatenate([attn[b]["o_a"] for b in range(bb)], axis=0)
    y = _merge_rows(xbuf[1 - slot], o_r, part(1), o_a, part(5), p_ref[...].reshape(n_rows, D_PLE), wout_ref, gp_ref,
                    wgate_ref, wproj_ref, fill)
    y_ref[...] = y.reshape(y_ref.shape)
    while queue:
        fill()

    @pl.when((s > 0) & (j == nblk - 1))
    def _():
        _store_states(s_scr, s_out_ref, bb)
        shift_ref[...] = prev_scr[...]
        for b in range(bb):
            kc_ref[b] = attn[b]["kn"].T
            vc_ref[b] = v[b * WINDOW:(b + 1) * WINDOW].T


def _prompt_layer(x, p, gn, w_in_b, vecs, qw, kw, sinks_b, w_out_b, gp, w_gate_b, w_proj_b):
    batch, seq, _ = x.shape
    bb = PROMPT_ROWS
    nblk = seq // WINDOW
    n_steps = (batch // bb) * nblk
    const = lambda r, w: pl.BlockSpec((r, w), lambda s: (0, 0))
    cur = lambda s: jnp.minimum(s, n_steps - 1)
    prv = lambda s: jnp.maximum(s - 1, 0)
    per_row = lambda *shape: pl.BlockSpec((bb,) + shape, lambda s: (prv(s) // nblk,) + (0,) * len(shape))
    in_specs = ([pl.BlockSpec((bb, WINDOW, D_MODEL), lambda s: (cur(s) // nblk, cur(s) % nblk, 0)),
                 pl.BlockSpec((bb, WINDOW, D_PLE), lambda s: (prv(s) // nblk, prv(s) % nblk, 0)),
                 const(1, D_MODEL), const(D_MODEL, D_IN)]
                + _vec_specs(lambda s: (0, 0))
                + [const(1, D_ATTN), const(1, D_KV), const(N_Q_HEADS, LANES), const(D_MODEL, D_MODEL),
                   const(1, D_MODEL), const(D_MODEL, D_MODEL), const(D_PLE, D_MODEL)])
    return pl.pallas_call(
        functools.partial(_prompt_layer_kernel, nblk),
        grid=(n_steps + 1,),
        in_specs=in_specs,
        out_specs=[pl.BlockSpec((bb, WINDOW, D_MODEL), lambda s: (prv(s) // nblk, prv(s) % nblk, 0)),
                   per_row(2 * N_PAIRS, HEAD_DIM, HEAD_DIM), per_row(1, D_SHIFT),
                   per_row(D_KV, WINDOW), per_row(D_KV, WINDOW)],
        out_shape=[jax.ShapeDtypeStruct((batch, seq, D_MODEL), F32),
                   jax.ShapeDtypeStruct((batch, 2 * N_PAIRS, HEAD_DIM, HEAD_DIM), F32),
                   jax.ShapeDtypeStruct((batch, 1, D_SHIFT), F32),
                   jax.ShapeDtypeStruct((batch, D_KV, WINDOW), F32),
                   jax.ShapeDtypeStruct((batch, D_KV, WINDOW), F32)],
        scratch_shapes=[pltpu.VMEM((2, bb * WINDOW, D_IN), F32), pltpu.VMEM((2, bb * WINDOW, D_MODEL), F32),
                        pltpu.VMEM((bb * N_PAIRS, LANES, LANES), F32), pltpu.VMEM((bb, 1, D_SHIFT), F32),
                        pltpu.VMEM((bb, WINDOW, D_KV), BF16), pltpu.VMEM((bb, WINDOW, D_KV), BF16),
                        pltpu.VMEM((bb, D_KV, WINDOW), BF16)],
        compiler_params=pltpu.CompilerParams(dimension_semantics=("arbitrary",), vmem_limit_bytes=VMEM_LIMIT),
        name="prompt_layer",
    )(x, p, gn, w_in_b, *vecs, qw, kw, sinks_b, w_out_b, gp, w_gate_b, w_proj_b)


def kernel(x_prompt, x_sample, state_rwkv, state_shift, cache_k, cache_v, p_prompt, p_sample, g_norm, w_in, mu_shift, w0, w_dec2, a0, w_a2, k_k, k_a, r_k, lnx_w, lnx_b, q_norm_w, k_norm_w, sinks, w_out, g_ple, w_ple_gate, w_ple_proj):
    depth = w_in.shape[0]
    bp, seq, _ = x_prompt.shape
    bs, dec, _ = x_sample.shape
    xp = x_prompt
    xs = x_sample.reshape(bs * dec, D_MODEL)
    outs = [[] for _ in range(8)]
    for i in range(depth):
        w_in_b = w_in[i].astype(BF16)
        w_out_b = w_out[i].astype(BF16)
        w_gate_b = w_ple_gate[i].astype(BF16)
        w_proj_b = w_ple_proj[i].astype(BF16)
        zl = jnp.zeros((D_LORA, D_RWKV), F32)
        lora_w = jnp.concatenate([jnp.concatenate([w_dec2[i], zl], axis=1),
                                  jnp.concatenate([zl, w_a2[i]], axis=1)], axis=0).astype(BF16)
        row = lambda t: t.reshape(1, -1)
        vecs = (row(mu_shift[i]), row(w0[i]), row(a0[i]), lora_w, row(k_k[i]), row(k_a[i]), row(r_k[i]),
                row(lnx_w[i]), row(lnx_b[i]))
        qw = row(jnp.tile(q_norm_w[i], N_Q_HEADS))
        kw = row(jnp.tile(k_norm_w[i], N_KV_HEADS))
        sinks_b = jnp.broadcast_to(sinks[i][:, None], (N_Q_HEADS, LANES))
        gn, gp = row(g_norm[i]), row(g_ple[i])

        to_t = lambda c: jnp.transpose(c, (0, 2, 3, 1)).reshape(c.shape[0], D_KV, c.shape[1])
        from_t = lambda c: jnp.transpose(c.reshape(c.shape[0], N_KV_HEADS, HEAD_DIM, c.shape[2]), (0, 3, 1, 2))

        xp, s_p, sh_p, kc, vc = _prompt_layer(xp, p_prompt[i], gn, w_in_b, vecs, qw, kw, sinks_b, w_out_b, gp,
                                              w_gate_b, w_proj_b)
        outs[0].append(s_p)
        outs[2].append(sh_p)
        outs[4].append(from_t(kc))
        outs[6].append(from_t(vc))

        f, z_r, q, k, v, z_a = _in_proj(xs, gn, w_in_b, 512)
        xs, s_s, k_buf, v_buf = _sample_layer(
            f, z_r, q.reshape(bs, dec, D_ATTN), k.reshape(bs, dec, D_KV), v.reshape(bs, dec, D_KV), z_a,
            state_shift[i], state_rwkv[i], vecs, to_t(cache_k[i]), to_t(cache_v[i]), qw, kw,
            jnp.repeat(sinks_b, dec, axis=0), xs, p_sample[i].reshape(bs * dec, D_PLE), w_out_b, gp, w_gate_b, w_proj_b)
        outs[1].append(s_s)
        outs[3].append(f.reshape(bs, dec, D_SHIFT)[:, -1:])
        outs[5].append(from_t(k_buf))
        outs[7].append(from_t(v_buf))
    st = lambda l: jnp.stack(l)
    return (xp, xs.reshape(bs, dec, D_MODEL),
            st(outs[0]), st(outs[1]), st(outs[2]), st(outs[3]), st(outs[4]), st(outs[5]), st(outs[6]), st(outs[7]))
```

```python
import functools
import math

import jax
import jax.numpy as jnp
from jax import lax
from jax.experimental import pallas as pl
from jax.experimental.pallas import tpu as pltpu

F32 = jnp.float32
BF16 = jnp.bfloat16

D_MODEL = 1024
HEAD_DIM = 64
D_RWKV = 512
D_ATTN = 512
N_KV_HEADS = 2
N_Q_HEADS = 8
Q_PER_KV = N_Q_HEADS // N_KV_HEADS
D_KV = N_KV_HEADS * HEAD_DIM
WINDOW = 128
D_LORA = 64
D_SHIFT = 3 * D_RWKV + 2 * D_LORA
D_PLE = 256
D_IN = D_SHIFT + D_RWKV + D_ATTN + 2 * D_KV + D_ATTN
NORM_EPS = 1e-6
LNX_EPS = 64e-5
NEG_INF = -1e30

LANES = 128
N_PAIRS = D_RWKV // LANES
PROMPT_CHUNK = 64
SAMPLE_GROUP = 16
VMEM_LIMIT = 62 * 1024 * 1024


def _dot(a, b):
    return jnp.dot(a.astype(BF16), b.astype(BF16), preferred_element_type=F32)


def _dot_nt(a, b):
    return lax.dot_general(a.astype(BF16), b.astype(BF16), (((1,), (1,)), ((), ())), preferred_element_type=F32)


def _dot_tn(a, b):
    return lax.dot_general(a.astype(BF16), b.astype(BF16), (((0,), (0,)), ((), ())), preferred_element_type=F32)


def _segment_cumsum(x, seg):
    blk = min(max(seg, HEAD_DIM), x.shape[0])
    ti = lax.broadcasted_iota(jnp.int32, (blk, 3 * blk), 0)
    tj = lax.broadcasted_iota(jnp.int32, (blk, 3 * blk), 1) % blk
    tri3 = ((ti // seg == tj // seg) & (tj <= ti)).astype(BF16)
    hi = x.astype(BF16)
    r1 = x - hi.astype(F32)
    mid = r1.astype(BF16)
    lo = (r1 - mid.astype(F32)).astype(BF16)
    parts = []
    for r0 in range(0, x.shape[0], blk):
        rows = slice(r0, r0 + blk)
        parts.append(jnp.dot(tri3, jnp.concatenate([hi[rows], mid[rows], lo[rows]], axis=0),
                             preferred_element_type=F32))
    return parts[0] if len(parts) == 1 else jnp.concatenate(parts, axis=0)


def _head_ones():
    r = (lax.broadcasted_iota(jnp.int32, (2 * LANES, LANES), 0) % LANES) // HEAD_DIM
    c = lax.broadcasted_iota(jnp.int32, (2 * LANES, LANES), 1) // HEAD_DIM
    return (r == c).astype(BF16)


def _head_sum(x, ones):
    tiles = []
    for i in range(0, x.shape[1], LANES):
        xt = x[:, i:i + LANES]
        hi = xt.astype(BF16)
        lo = (xt - hi.astype(F32)).astype(BF16)
        tiles.append(jnp.dot(jnp.concatenate([hi, lo], axis=1), ones, preferred_element_type=F32))
    return tiles[0] if len(tiles) == 1 else jnp.concatenate(tiles, axis=1)


def _rms_rows(x, g):
    return x * lax.rsqrt(jnp.mean(x * x, axis=-1, keepdims=True) + NORM_EPS) * g


_IN_SPLITS = (D_SHIFT, D_RWKV, D_ATTN, D_KV, D_KV, D_ATTN)


def _in_proj_kernel(x_ref, g_ref, w_ref, *out_refs):
    h = _dot(_rms_rows(x_ref[...], g_ref[...]), w_ref[...])
    off = 0
    for o_ref, width in zip(out_refs, _IN_SPLITS):
        o_ref[...] = h[:, off:off + width]
        off += width


def _in_proj(x, g_norm, w_in_bf16, tm):
    m = x.shape[0]
    return pl.pallas_call(
        _in_proj_kernel,
        grid=(m // tm,),
        in_specs=[pl.BlockSpec((tm, D_MODEL), lambda i: (i, 0)),
                  pl.BlockSpec((1, D_MODEL), lambda i: (0, 0)),
                  pl.BlockSpec((D_MODEL, D_IN), lambda i: (0, 0))],
        out_specs=[pl.BlockSpec((tm, w), lambda i: (i, 0)) for w in _IN_SPLITS],
        out_shape=[jax.ShapeDtypeStruct((m, w), F32) for w in _IN_SPLITS],
        compiler_params=pltpu.CompilerParams(dimension_semantics=("arbitrary",), vmem_limit_bytes=VMEM_LIMIT),
        name="in_proj",
    )(x, g_norm, w_in_bf16)


def _stack(z, half0):
    return jnp.concatenate([jnp.where(half0, z, 0.0), jnp.where(half0, 0.0, z)], axis=0).astype(BF16)


def _rwkv_recurrence(at, rt, bt, kt, v, e_cum, s_scr, cm, seg, segs_per_state, fill):
    n_rows = at.shape[0]
    n_blk = n_rows // cm
    lane = lax.broadcasted_iota(jnp.int32, (1, LANES), 1)
    half0 = lane < HEAD_DIM
    ri = lax.broadcasted_iota(jnp.int32, (cm, LANES), 0)
    ci = lax.broadcasted_iota(jnp.int32, (cm, LANES), 1) % cm
    same = (ri // seg) == (ci // seg)
    tri_strict = same & (ci < ri)
    tri_incl = same & (ci <= ri)
    eye_c = (ri == ci).astype(F32)
    sr = lax.broadcasted_iota(jnp.int32, (LANES, LANES), 0) // HEAD_DIM
    sc = lax.broadcasted_iota(jnp.int32, (LANES, LANES), 1) // HEAD_DIM
    state_mask = sr == sc
    n_levels = max(int(math.log2(seg)) - 1, 0)
    blocks = [(rb, p) for rb in range(n_blk) for p in range(N_PAIRS)]

    def tile(x, rb, p):
        return x[rb * cm:(rb + 1) * cm, p * LANES:(p + 1) * LANES]

    ops = {}
    for rb, p in blocks:
        at_p, rt_p, bt_p, kt_p, v_p = (tile(x, rb, p) for x in (at, rt, bt, kt, v))
        ops[rb, p] = dict(at=at_p, rt=rt_p, bt=bt_p, kt=kt_p, v=v_p, v_s=_stack(v_p, half0))
    for blk in blocks:
        o = ops[blk]
        g = _dot_nt(jnp.concatenate([o["at"], o["rt"]], axis=0),
                    jnp.concatenate([_stack(o["bt"], half0), _stack(o["kt"], half0)], axis=0))
        o["g_ab"] = jnp.where(tri_strict, g[:cm, :LANES], 0.0)
        o["g_ak"] = jnp.where(tri_strict, g[:cm, LANES:], 0.0)
        o["g_r"] = jnp.concatenate([jnp.where(tri_incl, g[cm:, :LANES], 0.0),
                                    jnp.where(tri_incl, g[cm:, LANES:], 0.0)], axis=1).astype(BF16)
    fill()

    for blk in blocks:
        ops[blk]["t_inv"] = eye_c + ops[blk]["g_ab"]
    if n_levels > 0:
        for blk in blocks:
            ops[blk]["a_pow"] = _dot(ops[blk]["g_ab"], _stack(ops[blk]["g_ab"], half0))
        fill()
        for lvl in range(n_levels):
            last = lvl == n_levels - 1
            for blk in blocks:
                o = ops[blk]
                if last:
                    o["t_inv"] = o["t_inv"] + _dot(o["a_pow"], _stack(o["t_inv"], half0))
                else:
                    m = _dot(o["a_pow"], jnp.concatenate([_stack(o["a_pow"], half0), _stack(o["t_inv"], half0)], axis=1))
                    o["t_inv"] = o["t_inv"] + m[:, LANES:]
                    o["a_pow"] = m[:, :LANES]
            fill()

    for blk in blocks:
        o = ops[blk]
        gakv = _dot(o["g_ak"], o["v_s"])
        z = _dot(o["t_inv"], jnp.concatenate([_stack(o["at"], half0), _stack(gakv, half0)], axis=1))
        o["a_hat"], o["p0"] = z[:, :LANES], z[:, LANES:]
    fill()

    n_seg = n_rows // seg
    n_states = n_seg // segs_per_state
    per_blk = cm // seg
    p_parts = {blk: [None] * per_blk for blk in blocks}
    y_parts = {blk: [None] * per_blk for blk in blocks}
    for step in range(segs_per_state):
        segs = [st * segs_per_state + step for st in range(n_states)]
        proj = {}
        for g_i in segs:
            rb, off = (g_i * seg) // cm, (g_i * seg) % cm
            for p in range(N_PAIRS):
                o = ops[rb, p]
                lhs = jnp.concatenate([o["a_hat"][off:off + seg], o["rt"][off:off + seg]], axis=0)
                proj[g_i, p] = _dot_nt(lhs, s_scr[(g_i // segs_per_state) * N_PAIRS + p])
        for g_i in segs:
            rb, off = (g_i * seg) // cm, (g_i * seg) % cm
            for p in range(N_PAIRS):
                o = ops[rb, p]
                p_seg = proj[g_i, p][:seg] + o["p0"][off:off + seg]
                p_parts[rb, p][off // seg] = p_seg
                y_parts[rb, p][off // seg] = proj[g_i, p][seg:]
                upd = _dot_tn(jnp.concatenate([p_seg, o["v"][off:off + seg]], axis=0),
                              jnp.concatenate([o["bt"][off:off + seg], o["kt"][off:off + seg]], axis=0))
                si = (g_i // segs_per_state) * N_PAIRS + p
                row_end = g_i * seg + seg - 1
                w_end = e_cum[row_end:row_end + 1, p * LANES:(p + 1) * LANES]
                s_scr[si] = w_end * (s_scr[si] + jnp.where(state_mask, upd, 0.0))

    cat = lambda parts: parts[0] if len(parts) == 1 else jnp.concatenate(parts, axis=0)
    rows = []
    for rb in range(n_blk):
        tiles = []
        for p in range(N_PAIRS):
            o = ops[rb, p]
            pv_s = jnp.concatenate([_stack(cat(p_parts[rb, p]), half0), o["v_s"]], axis=0)
            tiles.append(cat(y_parts[rb, p]) + jnp.dot(o["g_r"], pv_s, preferred_element_type=F32))
        rows.append(jnp.concatenate(tiles, axis=1))
    return cat(rows)


def _time_mix(f, prev_rows, tb, seg, cm, vec_refs, s_scr, fill=lambda: None):
    mu_ref, w0_ref, a0_ref, lora_ref, kk_ref, ka_ref, rk_ref, lnw_ref, lnb_ref = vec_refs
    n_rows = f.shape[0]
    row = lax.broadcasted_iota(jnp.int32, (n_rows, 1), 0)
    f_prev = pltpu.roll(f, 1, 0)
    for b, prev in enumerate(prev_rows):
        f_prev = jnp.where(row == b * tb, prev, f_prev)
    fs = f + (f_prev - f) * mu_ref[...]
    r = fs[:, 0:D_RWKV]
    k = fs[:, D_RWKV:2 * D_RWKV]
    v = fs[:, 2 * D_RWKV:3 * D_RWKV]
    wa = fs[:, 3 * D_RWKV:D_SHIFT]
    lane = lax.broadcasted_iota(jnp.int32, (1, LANES), 1)
    lora = _dot(jnp.where(lane < D_LORA, jnp.tanh(wa), wa), lora_ref[...])
    lw = (-math.exp(-0.5)) * jax.nn.sigmoid(w0_ref[...] + lora[:, 0:D_RWKV])
    a_sig = jax.nn.sigmoid(a0_ref[...] + lora[:, D_RWKV:2 * D_RWKV])
    ones = _head_ones()
    kk = k * kk_ref[...]
    kk = kk * lax.rsqrt(jnp.maximum(_head_sum(kk * kk, ones), 1e-24))
    k2 = k * (1.0 + (a_sig - 1.0) * ka_ref[...])

    cum = _segment_cumsum(lw, seg)
    e_cum = jnp.exp(cum)
    e_inv = jnp.exp(-cum)
    y = _rwkv_recurrence(-kk * jnp.exp(cum - lw), r * e_cum, kk * a_sig * e_inv, k2 * e_inv, v, e_cum,
                         s_scr, cm, seg, tb // seg, fill)

    inv_n = 1.0 / HEAD_DIM
    yc = y - _head_sum(y, ones) * inv_n
    var = _head_sum(yc * yc, ones) * inv_n
    yn = yc * lax.rsqrt(var + LNX_EPS) * lnw_ref[...] + lnb_ref[...]
    bonus = _head_sum(r * k2 * rk_ref[...], ones) * v
    return yn + bonus


def _load_states(s0_ref, s_scr, bb):
    zero = jnp.zeros((HEAD_DIM, HEAD_DIM), F32)
    for b in range(bb):
        for p in range(N_PAIRS):
            top = jnp.concatenate([s0_ref[b, 2 * p], zero], axis=1)
            bot = jnp.concatenate([zero, s0_ref[b, 2 * p + 1]], axis=1)
            s_scr[b * N_PAIRS + p] = jnp.concatenate([top, bot], axis=0)


def _store_states(s_scr, s_out_ref, bb):
    for b in range(bb):
        for p in range(N_PAIRS):
            s = s_scr[b * N_PAIRS + p]
            s_out_ref[b, 2 * p] = s[:HEAD_DIM, :HEAD_DIM]
            s_out_ref[b, 2 * p + 1] = s[HEAD_DIM:, HEAD_DIM:]


def _vec_specs(index_map):
    vec = lambda n: pl.BlockSpec((1, n), index_map)
    return [vec(D_SHIFT), vec(D_RWKV), vec(D_RWKV), pl.BlockSpec((LANES, 2 * D_RWKV), index_map),
            vec(D_RWKV), vec(D_RWKV), vec(D_RWKV), vec(D_RWKV), vec(D_RWKV)]


def _qk_norm(q, k, qw, kw, ones):
    inv_n = 1.0 / HEAD_DIM
    qn = q * lax.rsqrt(_head_sum(q * q, ones) * inv_n + NORM_EPS) * (qw * (HEAD_DIM ** -0.5))
    kn = k * lax.rsqrt(_head_sum(k * k, ones) * inv_n + NORM_EPS) * kw
    return qn, kn


def _swa_prompt_steps(q, k, v, qw, kw, sinks_ref, kprev, kprev_rot, vprev_t, has_prev, out):
    ones = _head_ones()
    half0 = lax.broadcasted_iota(jnp.int32, (1, LANES), 1) < HEAD_DIM
    qn, kn = _qk_norm(q, k, qw, kw, ones)
    kn_b = kn.astype(BF16)
    kn_rot = pltpu.roll(kn, HEAD_DIM, 1).astype(BF16)
    v_t = v.T.astype(BF16)
    keys = jnp.concatenate([kprev[...], kn_b], axis=0)
    keys_rot = jnp.concatenate([kprev_rot[...], kn_rot], axis=0)
    vals_t = jnp.concatenate([vprev_t[...], v_t], axis=1)
    kprev[...] = kn_b
    kprev_rot[...] = kn_rot
    vprev_t[...] = v_t
    kj = lax.broadcasted_iota(jnp.int32, (2 * WINDOW, WINDOW), 0)
    qi = lax.broadcasted_iota(jnp.int32, (2 * WINDOW, WINDOW), 1)
    valid_t = (kj > qi) & (kj <= qi + WINDOW) & ((kj >= WINDOW) | has_prev)
    heads = range(N_Q_HEADS)
    yield
    qm = [jnp.where(half0 if h % 2 == 0 else jnp.logical_not(half0), qn[:, (h // 2) * LANES:(h // 2 + 1) * LANES],
                    0.0).astype(BF16) for h in heads]
    straight = [h for h in heads if h // Q_PER_KV == h % 2]
    swapped = [h for h in heads if h // Q_PER_KV != h % 2]
    scores = [None] * N_Q_HEADS
    for group, kmat in ((straight, keys), (swapped, keys_rot)):
        for h0, h1 in zip(group[0::2], group[1::2]):
            s2 = _dot_nt(kmat, jnp.concatenate([qm[h0], qm[h1]], axis=0))
            scores[h0] = jnp.where(valid_t, s2[:, :WINDOW], NEG_INF)
            scores[h1] = jnp.where(valid_t, s2[:, WINDOW:], NEG_INF)
    yield
    probs = []
    for h in heads:
        sink = sinks_ref[h:h + 1, 0:1]
        m = jnp.maximum(jnp.max(scores[h], axis=0, keepdims=True), sink)
        pr = jnp.exp(scores[h] - m)
        denom = jnp.sum(pr, axis=0, keepdims=True) + jnp.exp(sink - m)
        probs.append((pr * (1.0 / denom)).astype(BF16))
    yield
    tiles = []
    for t in range(D_ATTN // LANES):
        g = (2 * t) // Q_PER_KV
        o2 = jnp.dot(vals_t[g * HEAD_DIM:(g + 1) * HEAD_DIM, :], jnp.concatenate(probs[2 * t:2 * t + 2], axis=1),
                     preferred_element_type=F32)
        tiles.append(jnp.concatenate([o2[:, :WINDOW], o2[:, WINDOW:]], axis=0).T)
    out["o_a"] = jnp.concatenate(tiles, axis=1)
    out["kn"] = kn


def _attn_sample_steps(t_new, q_ref, k_ref, v_ref, ckt_ref, cvt_ref, qw_ref, kw_ref, sinks_ref,
                       o_ref, kot_ref, vot_ref):
    n_seq = q_ref.shape[0]
    wb = ckt_ref.shape[2]
    rows = N_Q_HEADS * t_new
    ones = _head_ones()
    half0 = lax.broadcasted_iota(jnp.int32, (1, LANES), 1) < HEAD_DIM
    qi_c = lax.broadcasted_iota(jnp.int32, (rows, wb), 0) % t_new
    kj_c = lax.broadcasted_iota(jnp.int32, (rows, wb), 1)
    valid_c = (qi_c + wb - kj_c) < WINDOW
    qi_n = lax.broadcasted_iota(jnp.int32, (rows, t_new), 0) % t_new
    kj_n = lax.broadcasted_iota(jnp.int32, (rows, t_new), 1)
    valid_n = kj_n <= qi_n
    sink = sinks_ref[:, 0:1]
    heads = range(N_Q_HEADS)
    swap = [h // Q_PER_KV != h % 2 for h in heads]

    keep = lax.broadcasted_iota(jnp.int32, (1, wb), 1) < wb - t_new
    sel_t = lax.broadcasted_iota(jnp.int32, (3 * t_new, wb), 0) % t_new
    sel_l = lax.broadcasted_iota(jnp.int32, (3 * t_new, wb), 1)
    sel3 = (sel_l == sel_t + (wb - t_new)).astype(BF16)

    def _place_new(x):
        hi = x.astype(BF16)
        r1 = x - hi.astype(F32)
        mid = r1.astype(BF16)
        lo = (r1 - mid.astype(F32)).astype(BF16)
        return lax.dot_general(jnp.concatenate([hi, mid, lo], axis=0), sel3, (((0,), (0,)), ((), ())),
                               preferred_element_type=F32)

    seqs = []
    for i in range(n_seq):
        qn, kn = _qk_norm(q_ref[i], k_ref[i], qw_ref[...], kw_ref[...], ones)
        pieces = []
        for h in heads:
            qm = jnp.where(half0 if h % 2 == 0 else jnp.logical_not(half0), qn[:, (h // 2) * LANES:(h // 2 + 1) * LANES],
                           0.0)
            pieces.append(pltpu.roll(qm, HEAD_DIM, 1) if swap[h] else qm)
        seqs.append(dict(q=jnp.concatenate(pieces, axis=0).astype(BF16), kn=kn, v=v_ref[i],
                         ckt=ckt_ref[i], cvt=cvt_ref[i]))
    yield
    for s in seqs:
        s["s_c"] = jnp.where(valid_c, _dot(s["q"], s["ckt"]), NEG_INF)
        s["s_n"] = jnp.where(valid_n, _dot_nt(s["q"], s["kn"]), NEG_INF)
    yield
    for s in seqs:
        m = jnp.maximum(jnp.maximum(jnp.max(s["s_c"], axis=-1, keepdims=True),
                                    jnp.max(s["s_n"], axis=-1, keepdims=True)), sink)
        p_c = jnp.exp(s["s_c"] - m)
        p_n = jnp.exp(s["s_n"] - m)
        denom = jnp.sum(p_c, axis=-1, keepdims=True) + jnp.sum(p_n, axis=-1, keepdims=True) + jnp.exp(sink - m)
        s["p_c"], s["p_n"], s["inv"] = p_c, p_n, 1.0 / denom
    yield
    for i, s in enumerate(seqs):
        o = (_dot_nt(s["p_c"], s["cvt"]) + _dot(s["p_n"], s["v"])) * s["inv"]
        tiles = []
        for t in range(D_ATTN // LANES):
            pair = []
            for h in (2 * t, 2 * t + 1):
                o_h = o[h * t_new:(h + 1) * t_new]
                pair.append(pltpu.roll(o_h, HEAD_DIM, 1) if swap[h] else o_h)
            tiles.append(jnp.where(half0, pair[0], pair[1]))
        o_ref[i] = jnp.concatenate(tiles, axis=1)
        kot_ref[i] = jnp.where(keep, pltpu.roll(s["ckt"], wb - t_new, 1), _place_new(s["kn"]))
        vot_ref[i] = jnp.where(keep, pltpu.roll(s["cvt"], wb - t_new, 1), _place_new(s["v"]))


def _sample_mix_kernel(bb, tb, f_ref, prev0_ref, *rest):
    vec_refs = rest[:9]
    (s0_ref, q_ref, k_ref, v_ref, ckt_ref, cvt_ref, qw_ref, kw_ref, sinks_ref,
     or_ref, s_out_ref, oa_ref, kot_ref, vot_ref, s_scr) = rest[9:]
    attn = _attn_sample_steps(tb, q_ref, k_ref, v_ref, ckt_ref, cvt_ref, qw_ref, kw_ref, sinks_ref,
                              oa_ref, kot_ref, vot_ref)
    _load_states(s0_ref, s_scr, bb)
    or_ref[...] = _time_mix(f_ref[...], [prev0_ref[b] for b in range(bb)], tb, tb, HEAD_DIM, vec_refs, s_scr,
                            functools.partial(next, attn, None))
    for _ in attn:
        pass
    _store_states(s_scr, s_out_ref, bb)


def _sample_mix(f, prev0, s0, vecs, q, k, v, ckt, cvt, qw, kw, sinks_rows):
    b, t_new, _ = q.shape
    wb = ckt.shape[2]
    gb = SAMPLE_GROUP
    n_rows = gb * t_new
    assert n_rows % HEAD_DIM == 0 and HEAD_DIM % t_new == 0
    spec = lambda r, w: pl.BlockSpec((gb, r, w), lambda i: (i, 0, 0))
    const = lambda r, w: pl.BlockSpec((r, w), lambda i: (0, 0))
    state_spec = pl.BlockSpec((gb, 2 * N_PAIRS, HEAD_DIM, HEAD_DIM), lambda i: (i, 0, 0, 0))
    return pl.pallas_call(
        functools.partial(_sample_mix_kernel, gb, t_new),
        grid=(b // gb,),
        in_specs=([pl.BlockSpec((n_rows, D_SHIFT), lambda i: (i, 0)), spec(1, D_SHIFT)] + _vec_specs(lambda i: (0, 0))
                  + [state_spec, spec(t_new, D_ATTN), spec(t_new, D_KV), spec(t_new, D_KV), spec(D_KV, wb),
                     spec(D_KV, wb), const(1, D_ATTN), const(1, D_KV), const(N_Q_HEADS * t_new, LANES)]),
        out_specs=[pl.BlockSpec((n_rows, D_RWKV), lambda i: (i, 0)), state_spec,
                   spec(t_new, D_ATTN), spec(D_KV, wb), spec(D_KV, wb)],
        out_shape=[jax.ShapeDtypeStruct((b * t_new, D_RWKV), F32),
                   jax.ShapeDtypeStruct((b, 2 * N_PAIRS, HEAD_DIM, HEAD_DIM), F32),
                   jax.ShapeDtypeStruct((b, t_new, D_ATTN), F32),
                   jax.ShapeDtypeStruct((b, D_KV, wb), F32),
                   jax.ShapeDtypeStruct((b, D_KV, wb), F32)],
        scratch_shapes=[pltpu.VMEM((gb * N_PAIRS, LANES, LANES), F32)],
        compiler_params=pltpu.CompilerParams(dimension_semantics=("arbitrary",), vmem_limit_bytes=VMEM_LIMIT),
        name="sample_mix",
    )(f, prev0, *vecs, s0, q, k, v, ckt, cvt, qw, kw, sinks_rows)


def _merge_rows(x, o_r, z_r, o_a, z_a, p, wout_ref, g_ref, wgate_ref, wproj_ref, fill=lambda: None):
    gr = o_r * jax.nn.silu(z_r)
    ga = o_a * jax.nn.silu(z_a)
    ple = _dot(p, wproj_ref[...])
    h = x + _dot(gr, wout_ref[0:D_RWKV, :]) + _dot(ga, wout_ref[D_RWKV:D_MODEL, :])
    fill()
    gate = jax.nn.sigmoid(_dot(_rms_rows(h, g_ref[...]), wgate_ref[...]))
    fill()
    return h + gate * ple


def _merge_kernel(x_ref, or_ref, zr_ref, oa_ref, za_ref, p_ref, wout_ref, g_ref, wgate_ref, wproj_ref, y_ref):
    y_ref[...] = _merge_rows(x_ref[...], or_ref[...], zr_ref[...], oa_ref[...], za_ref[...], p_ref[...],
                             wout_ref, g_ref, wgate_ref, wproj_ref)


def _merge(x, o_r, z_r, o_a, z_a, p, w_out, g_ple, w_gate, w_proj, tm):
    m = x.shape[0]
    tok = lambda w: pl.BlockSpec((tm, w), lambda i: (i, 0))
    const = lambda r, w: pl.BlockSpec((r, w), lambda i: (0, 0))
    return pl.pallas_call(
        _merge_kernel,
        grid=(m // tm,),
        in_specs=[tok(D_MODEL), tok(D_RWKV), tok(D_RWKV), tok(D_ATTN), tok(D_ATTN), tok(D_PLE),
                  const(D_MODEL, D_MODEL), const(1, D_MODEL), const(D_MODEL, D_MODEL), const(D_PLE, D_MODEL)],
        out_specs=tok(D_MODEL),
        out_shape=jax.ShapeDtypeStruct((m, D_MODEL), F32),
        compiler_params=pltpu.CompilerParams(dimension_semantics=("arbitrary",), vmem_limit_bytes=VMEM_LIMIT),
        name="merge",
    )(x, o_r, z_r, o_a, z_a, p, w_out, g_ple, w_gate, w_proj)


_IN_CHUNK = 512
PROMPT_ROWS = 4
_FILL_HEAD = 3
_FILL_TAIL = 2


def _fill_order(n_proj, n_attn):
    mid = n_proj - _FILL_HEAD - _FILL_TAIL
    order = ["p"] * _FILL_HEAD
    done = 0
    for i in range(n_attn):
        while done < mid and done * n_attn <= i * mid:
            order.append("p")
            done += 1
        order.append("a")
    return order + ["p"] * (mid - done + _FILL_TAIL)


def _prompt_layer_kernel(nblk, x_ref, p_ref, gn_ref, win_ref, *rest):
    vec_refs = rest[:9]
    (qw_ref, kw_ref, sinks_ref, wout_ref, gp_ref, wgate_ref, wproj_ref,
     y_ref, s_out_ref, shift_ref, kc_ref, vc_ref,
     hbuf, xbuf, s_scr, prev_scr, kprev, kprev_rot, vprev_t) = rest[9:]
    bb = x_ref.shape[0]
    n_rows = bb * WINDOW
    s = pl.program_id(0)
    slot = s % 2
    j = jnp.maximum(s - 1, 0) % nblk

    @pl.when(s == 0)
    def _():
        hbuf[1] = jnp.zeros(hbuf.shape[1:], F32)
        xbuf[1] = jnp.zeros(xbuf.shape[1:], F32)

    @pl.when(j == 0)
    def _():
        s_scr[...] = jnp.zeros(s_scr.shape, F32)
        prev_scr[...] = jnp.zeros(prev_scr.shape, F32)
        kprev[...] = jnp.zeros(kprev.shape, BF16)
        kprev_rot[...] = jnp.zeros(kprev_rot.shape, BF16)
        vprev_t[...] = jnp.zeros(vprev_t.shape, BF16)

    h_cur = hbuf.at[1 - slot]
    x_new = x_ref[...].reshape(n_rows, D_MODEL)
    xbuf[slot] = x_new
    xn = _rms_rows(x_new, gn_ref[...]).astype(BF16)
    h_new = hbuf.at[slot]

    def project(c0):
        c1 = min(c0 + _IN_CHUNK, D_IN)
        h_new[:, c0:c1] = jnp.dot(xn, win_ref[:, c0:c1], preferred_element_type=F32)

    offs = [0]
    for w in _IN_SPLITS:
        offs.append(offs[-1] + w)
    part = lambda i: h_cur[:, offs[i]:offs[i + 1]]
    q, k, v = part(2), part(3), part(4)
    attn = [{} for _ in range(bb)]
    attn_steps = []
    for b in range(bb):
        rows = slice(b * WINDOW, (b + 1) * WINDOW)
        attn_steps.append(_swa_prompt_steps(q[rows], k[rows], v[rows], qw_ref[...], kw_ref[...], sinks_ref,
                                            kprev.at[b], kprev_rot.at[b], vprev_t.at[b], j > 0, attn[b]))
    proj_items = [functools.partial(project, c0) for c0 in range(0, D_IN, _IN_CHUNK)]
    attn_items = [functools.partial(next, attn_steps[b], None) for _ in range(4) for b in range(bb)]
    queue = [proj_items.pop(0) if kind == "p" else attn_items.pop(0)
             for kind in _fill_order(len(proj_items), len(attn_items))]
    assert not proj_items and not attn_items

    def fill():
        if queue:
            queue.pop(0)()

    for _ in range(_FILL_HEAD):
        fill()
    f = part(0)
    o_r = _time_mix(f, [prev_scr[b] for b in range(bb)], WINDOW, PROMPT_CHUNK, HEAD_DIM, vec_refs, s_scr, fill)
    for b in range(bb):
        prev_scr[b] = f[(b + 1) * WINDOW - 1:(b + 1) * WINDOW, :]
    while len(queue) > _FILL_TAIL:
        fill()
    o_a = jnp.concatenate([attn[b]["o_a"] for b in range(bb)], axis=0)
    y = _merge_rows(xbuf[1 - slot], o_r, part(1), o_a, part(5), p_ref[...].reshape(n_rows, D_PLE), wout_ref, gp_ref,
                    wgate_ref, wproj_ref, fill)
    y_ref[...] = y.reshape(y_ref.shape)
    while queue:
        fill()

    @pl.when((s > 0) & (j == nblk - 1))
    def _():
        _store_states(s_scr, s_out_ref, bb)
        shift_ref[...] = prev_scr[...]
        for b in range(bb):
            kc_ref[b] = attn[b]["kn"].T
            vc_ref[b] = v[b * WINDOW:(b + 1) * WINDOW].T


def _prompt_layer(x, p, gn, w_in_b, vecs, qw, kw, sinks_b, w_out_b, gp, w_gate_b, w_proj_b):
    batch, seq, _ = x.shape
    bb = PROMPT_ROWS
    nblk = seq // WINDOW
    n_steps = (batch // bb) * nblk
    const = lambda r, w: pl.BlockSpec((r, w), lambda s: (0, 0))
    cur = lambda s: jnp.minimum(s, n_steps - 1)
    prv = lambda s: jnp.maximum(s - 1, 0)
    per_row = lambda *shape: pl.BlockSpec((bb,) + shape, lambda s: (prv(s) // nblk,) + (0,) * len(shape))
    in_specs = ([pl.BlockSpec((bb, WINDOW, D_MODEL), lambda s: (cur(s) // nblk, cur(s) % nblk, 0)),
                 pl.BlockSpec((bb, WINDOW, D_PLE), lambda s: (prv(s) // nblk, prv(s) % nblk, 0)),
                 const(1, D_MODEL), const(D_MODEL, D_IN)]
                + _vec_specs(lambda s: (0, 0))
                + [const(1, D_ATTN), const(1, D_KV), const(N_Q_HEADS, LANES), const(D_MODEL, D_MODEL),
                   const(1, D_MODEL), const(D_MODEL, D_MODEL), const(D_PLE, D_MODEL)])
    return pl.pallas_call(
        functools.partial(_prompt_layer_kernel, nblk),
        grid=(n_steps + 1,),
        in_specs=in_specs,
        out_specs=[pl.BlockSpec((bb, WINDOW, D_MODEL), lambda s: (prv(s) // nblk, prv(s) % nblk, 0)),
                   per_row(2 * N_PAIRS, HEAD_DIM, HEAD_DIM), per_row(1, D_SHIFT),
                   per_row(D_KV, WINDOW), per_row(D_KV, WINDOW)],
        out_shape=[jax.ShapeDtypeStruct((batch, seq, D_MODEL), F32),
                   jax.ShapeDtypeStruct((batch, 2 * N_PAIRS, HEAD_DIM, HEAD_DIM), F32),
                   jax.ShapeDtypeStruct((batch, 1, D_SHIFT), F32),
                   jax.ShapeDtypeStruct((batch, D_KV, WINDOW), F32),
                   jax.ShapeDtypeStruct((batch, D_KV, WINDOW), F32)],
        scratch_shapes=[pltpu.VMEM((2, bb * WINDOW, D_IN), F32), pltpu.VMEM((2, bb * WINDOW, D_MODEL), F32),
                        pltpu.VMEM((bb * N_PAIRS, LANES, LANES), F32), pltpu.VMEM((bb, 1, D_SHIFT), F32),
                        pltpu.VMEM((bb, WINDOW, D_KV), BF16), pltpu.VMEM((bb, WINDOW, D_KV), BF16),
                        pltpu.VMEM((bb, D_KV, WINDOW), BF16)],
        compiler_params=pltpu.CompilerParams(dimension_semantics=("arbitrary",), vmem_limit_bytes=VMEM_LIMIT),
        name="prompt_layer",
    )(x, p, gn, w_in_b, *vecs, qw, kw, sinks_b, w_out_b, gp, w_gate_b, w_proj_b)


def kernel(x_prompt, x_sample, state_rwkv, state_shift, cache_k, cache_v, p_prompt, p_sample, g_norm, w_in, mu_shift, w0, w_dec2, a0, w_a2, k_k, k_a, r_k, lnx_w, lnx_b, q_norm_w, k_norm_w, sinks, w_out, g_ple, w_ple_gate, w_ple_proj):
    depth = w_in.shape[0]
    bp, seq, _ = x_prompt.shape
    bs, dec, _ = x_sample.shape
    xp = x_prompt
    xs = x_sample.reshape(bs * dec, D_MODEL)
    outs = [[] for _ in range(8)]
    for i in range(depth):
        w_in_b = w_in[i].astype(BF16)
        w_out_b = w_out[i].astype(BF16)
        w_gate_b = w_ple_gate[i].astype(BF16)
        w_proj_b = w_ple_proj[i].astype(BF16)
        zl = jnp.zeros((D_LORA, D_RWKV), F32)
        lora_w = jnp.concatenate([jnp.concatenate([w_dec2[i], zl], axis=1),
                                  jnp.concatenate([zl, w_a2[i]], axis=1)], axis=0).astype(BF16)
        row = lambda t: t.reshape(1, -1)
        vecs = (row(mu_shift[i]), row(w0[i]), row(a0[i]), lora_w, row(k_k[i]), row(k_a[i]), row(r_k[i]),
                row(lnx_w[i]), row(lnx_b[i]))
        qw = row(jnp.tile(q_norm_w[i], N_Q_HEADS))
        kw = row(jnp.tile(k_norm_w[i], N_KV_HEADS))
        sinks_b = jnp.broadcast_to(sinks[i][:, None], (N_Q_HEADS, LANES))
        gn, gp = row(g_norm[i]), row(g_ple[i])

        to_t = lambda c: jnp.transpose(c, (0, 2, 3, 1)).reshape(c.shape[0], D_KV, c.shape[1])
        from_t = lambda c: jnp.transpose(c.reshape(c.shape[0], N_KV_HEADS, HEAD_DIM, c.shape[2]), (0, 3, 1, 2))

        xp, s_p, sh_p, kc, vc = _prompt_layer(xp, p_prompt[i], gn, w_in_b, vecs, qw, kw, sinks_b, w_out_b, gp,
                                              w_gate_b, w_proj_b)
        outs[0].append(s_p)
        outs[2].append(sh_p)
        outs[4].append(from_t(kc))
        outs[6].append(from_t(vc))

        f, z_r, q, k, v, z_a = _in_proj(xs, gn, w_in_b, 512)
        o_r, s_s, o_a, k_buf, v_buf = _sample_mix(
            f, state_shift[i], state_rwkv[i], vecs, q.reshape(bs, dec, D_ATTN), k.reshape(bs, dec, D_KV),
            v.reshape(bs, dec, D_KV), to_t(cache_k[i]), to_t(cache_v[i]), qw, kw, jnp.repeat(sinks_b, dec, axis=0))
        outs[1].append(s_s)
        outs[3].append(f.reshape(bs, dec, D_SHIFT)[:, -1:])
        outs[5].append(from_t(k_buf))
        outs[7].append(from_t(v_buf))
        xs = _merge(xs, o_r, z_r, o_a.reshape(bs * dec, D_ATTN), z_a, p_sample[i].reshape(bs * dec, D_PLE),
                    w_out_b, gp, w_gate_b, w_proj_b, 512)
    st = lambda l: jnp.stack(l)
    return (xp, xs.reshape(bs, dec, D_MODEL),
            st(outs[0]), st(outs[1]), st(outs[2]), st(outs[3]), st(outs[4]), st(outs[5]), st(outs[6]), st(outs[7]))
```
